```python
import jax, jax.numpy as jnp
from jax import lax
import numpy as np

D_MODEL = 1024
BATCH = 32
SEQ = 256
DEPTH = 4
DEC_BATCH = 8
DEC_SEQ = 4096
PAST_LEN = 256

GRID_W = 64
ROPE_THETA = 10000.0
EPS = 1e-6
Q_BLOCK = 128
MLA_HEADS = 4
Q_LORA = 256
KV_LORA = 128
MLA_NOPE = 64
MLA_ROPE = 32
MLA_V = 64
MLA_QK = MLA_NOPE + MLA_ROPE
GQA_HEADS = 8
GQA_KV_HEADS = 2
GQA_GROUP = GQA_HEADS // GQA_KV_HEADS
HEAD_DIM = 64
GLA_HEADS = 4
GLA_DK = 32
GLA_DV = 64
GLA_GATE_RANK = 16
GLA_GATE_NORM = 16.0
GLA_CHUNK = 64
MIX_WIDTH = MLA_HEADS * MLA_V + GQA_HEADS * HEAD_DIM + GLA_HEADS * GLA_DV
IN_SPLITS = (Q_LORA, KV_LORA, MLA_ROPE,
             GQA_HEADS * HEAD_DIM, GQA_KV_HEADS * HEAD_DIM, GQA_KV_HEADS * HEAD_DIM,
             GLA_HEADS * GLA_DK, GLA_HEADS * GLA_DK, GLA_HEADS * GLA_DV,
             GLA_GATE_RANK, GLA_GATE_RANK, GLA_HEADS * GLA_DV)
IN_DIM = sum(IN_SPLITS)
N_EXPERTS = 32
TOP_K = 4
D_EXPERT = D_MODEL
SWIGLU_LIMIT = 7.0
SWIGLU_ALPHA = 1.702
MOE_BLOCK = 256

kernel_name = "hybrid_mla_gqa_gla_moe_diffusion_step"

F32 = jnp.float32


def rms_norm(x, g):
    x32 = x.astype(F32)
    y = x32 * lax.rsqrt(jnp.mean(x32 * x32, axis=-1, keepdims=True) + EPS)
    return y.astype(x.dtype) * g


def axial_rope(x):
    T, R = x.shape[1], x.shape[-1]
    half, quarter = R // 2, R // 4
    pos = jnp.arange(T)
    row = (pos // GRID_W).astype(F32)
    col = (pos % GRID_W).astype(F32)
    inv_freq = ROPE_THETA ** (-jnp.arange(quarter, dtype=F32) / quarter)

    def rotate(xs, p):
        ang = p[:, None] * inv_freq[None, :]
        cos = jnp.cos(ang)[None, :, None, :]
        sin = jnp.sin(ang)[None, :, None, :]
        x1, x2 = xs[..., :quarter], xs[..., quarter:]
        return jnp.concatenate([x1 * cos - x2 * sin, x1 * sin + x2 * cos], axis=-1)

    x32 = x.astype(F32)
    out = jnp.concatenate([rotate(x32[..., :half], row), rotate(x32[..., half:], col)], axis=-1)
    return out.astype(x.dtype)


def split_columns(p):
    outs, off = [], 0
    for w in IN_SPLITS:
        outs.append(p[..., off:off + w])
        off += w
    return outs


def blocked_attention(q, k, v, scale):
    B, Tq, Hkv, G, dk = q.shape
    dv = v.shape[-1]
    nb = Tq // Q_BLOCK
    qb = q.reshape(B, nb, Q_BLOCK, Hkv, G, dk).transpose(1, 0, 2, 3, 4, 5)

    def one_block(q_blk):
        s = jnp.einsum('bqhgd,bkhd->bhgqk', q_blk, k).astype(F32) * scale
        p = jax.nn.softmax(s, axis=-1).astype(v.dtype)
        return jnp.einsum('bhgqk,bkhe->bqhge', p, v)

    o = lax.map(one_block, qb)
    return o.transpose(1, 0, 2, 3, 4, 5).reshape(B, Tq, Hkv * G * dv)


def gla_chunked(q, k, v, g, s0):
    B, T, H, DK = q.shape
    DV = v.shape[-1]
    n = T // GLA_CHUNK

    def chunks(a):
        return a.astype(F32).reshape(B, n, GLA_CHUNK, H, a.shape[-1]).transpose(1, 0, 3, 2, 4)

    lower = jnp.tril(jnp.ones((GLA_CHUNK, GLA_CHUNK), bool))[:, :, None]

    def step(S, inp):
        qc, kc, vc, gc = inp
        b = jnp.cumsum(gc, axis=2)
        o_inter = jnp.einsum('bhcd,bhde->bhce', qc * jnp.exp(b), S)
        rel = jnp.where(lower, b[:, :, :, None, :] - b[:, :, None, :, :], -jnp.inf)
        attn = jnp.einsum('bhid,bhjd,bhijd->bhij', qc, kc, jnp.exp(rel))
        o_intra = jnp.einsum('bhij,bhje->bhie', attn, vc)
        b_last = b[:, :, -1:, :]
        S_new = jnp.exp(b_last[:, :, 0, :])[..., None] * S + jnp.einsum(
            'bhcd,bhce->bhde', kc * jnp.exp(b_last - b), vc)
        return S_new, o_inter + o_intra

    S, o = lax.scan(step, s0.astype(F32), (chunks(q), chunks(k), chunks(v), chunks(g)))
    return o.transpose(1, 0, 3, 2, 4).reshape(B, T, H, DV).astype(v.dtype), S


def token_mixers(h, lw, ctx):
    B, T, _ = h.shape
    latent = ctx is not None
    (cq, ckv_raw, kr, gq, gk, gv, lq, lk, lv, lgf, lgb, lg) = split_columns(h @ lw['w_in'])

    q = (rms_norm(cq, lw['g_q_a']) @ lw['w_uq']).reshape(B, T, MLA_HEADS, MLA_QK)
    q_nope, q_rope = q[..., :MLA_NOPE], q[..., MLA_NOPE:]
    ckv = rms_norm(ckv_raw, lw['g_kv_a'])
    if latent:
        q_rope = axial_rope(q_rope)
        kr_lat = axial_rope(kr[:, :, None, :])[:, :, 0, :]
        ckv_all = jnp.concatenate([ctx[0], ckv], axis=1)
        kr_all = jnp.concatenate([ctx[1], kr_lat], axis=1)
    else:
        ckv_all, kr_all = ckv, kr
    kv = (ckv_all @ lw['w_ukv']).reshape(B, -1, MLA_HEADS, MLA_NOPE + MLA_V)
    k_m = jnp.concatenate(
        [kv[..., :MLA_NOPE], jnp.broadcast_to(kr_all[:, :, None, :], kv.shape[:3] + (MLA_ROPE,))], axis=-1)
    v_m = kv[..., MLA_NOPE:]
    q_m = jnp.concatenate([q_nope, q_rope], axis=-1)[:, :, :, None, :]
    o_mla = blocked_attention(q_m, k_m, v_m, MLA_QK ** -0.5)

    qg = rms_norm(gq.reshape(B, T, GQA_HEADS, HEAD_DIM), lw['g_q_head'])
    kg = rms_norm(gk.reshape(B, T, GQA_KV_HEADS, HEAD_DIM), lw['g_k_head'])
    vg = gv.reshape(B, T, GQA_KV_HEADS, HEAD_DIM)
    if latent:
        qg = axial_rope(qg)
        k_all = jnp.concatenate([ctx[2], axial_rope(kg)], axis=1)
        v_all = jnp.concatenate([ctx[3], vg], axis=1)
    else:
        k_all, v_all = kg, vg
    o_gqa = blocked_attention(qg.reshape(B, T, GQA_KV_HEADS, GQA_GROUP, HEAD_DIM), k_all, v_all,
                              HEAD_DIM ** -0.5)

    q_l = lq.reshape(B, T, GLA_HEADS, GLA_DK) * (GLA_DK ** -0.5)
    k_l = lk.reshape(B, T, GLA_HEADS, GLA_DK)
    v_l = lv.reshape(B, T, GLA_HEADS, GLA_DV)
    g_f = (jax.nn.log_sigmoid((lgf @ lw['w_gk_fwd'] + lw['b_gk_fwd']).astype(F32)) / GLA_GATE_NORM
           ).reshape(B, T, GLA_HEADS, GLA_DK)
    g_b = (jax.nn.log_sigmoid((lgb @ lw['w_gk_bwd'] + lw['b_gk_bwd']).astype(F32)) / GLA_GATE_NORM
           ).reshape(B, T, GLA_HEADS, GLA_DK)
    if latent:
        s0_f, s0_b = ctx[4][:, 0], ctx[4][:, 1]
    else:
        s0_f = s0_b = jnp.zeros((B, GLA_HEADS, GLA_DK, GLA_DV), F32)
    o_f, s_f = gla_chunked(q_l, k_l, v_l, g_f, s0_f)
    rev = lambda a: jnp.flip(a, axis=1)
    o_b, s_b = gla_chunked(rev(q_l), rev(k_l), rev(v_l), rev(g_b), s0_b)
    o_l = rms_norm(o_f + rev(o_b), lw['g_gla_out']) * jax.nn.silu(lg.reshape(B, T, GLA_HEADS, GLA_DV))
    o_gla = o_l.reshape(B, T, GLA_HEADS * GLA_DV)

    out = jnp.concatenate([o_mla, o_gqa, o_gla], axis=-1) @ lw['w_out']
    return out, (ckv, kr, kg, vg, jnp.stack([s_f, s_b], axis=1))


def moe_ffn(h, w_router, b_router, w_gate_up, b_gate_up, w_down, b_down):
    N, D = h.shape
    logits = (h @ w_router + b_router).astype(F32)
    top_vals, top_idx = lax.top_k(logits, TOP_K)
    gates = jax.nn.softmax(top_vals, axis=-1).astype(h.dtype)
    NK = N * TOP_K
    flat_e = top_idx.reshape(NK)
    order = jnp.argsort(flat_e)
    sorted_e = flat_e[order]
    tok = (order // TOP_K).astype(jnp.int32)
    counts = jnp.bincount(flat_e, length=N_EXPERTS)
    padded = (counts + MOE_BLOCK - 1) // MOE_BLOCK * MOE_BLOCK
    start = jnp.cumsum(counts) - counts
    pad_end = jnp.cumsum(padded)
    pad_start = pad_end - padded
    dest = pad_start[sorted_e] + jnp.arange(NK) - start[sorted_e]
    n_blocks = -(-NK // MOE_BLOCK) + N_EXPERTS
    slot_tok = jnp.full((n_blocks * MOE_BLOCK,), N, jnp.int32).at[dest].set(tok)
    block_e = jnp.minimum(jnp.searchsorted(pad_end, jnp.arange(n_blocks) * MOE_BLOCK, side='right'),
                          N_EXPERTS - 1)
    h_pad = jnp.concatenate([h, jnp.zeros((1, D), h.dtype)], axis=0)
    xb = h_pad[slot_tok].reshape(n_blocks, MOE_BLOCK, D)

    def expert_block(args):
        x_blk, e = args
        gu = x_blk @ w_gate_up[e] + b_gate_up[e]
        gate = jnp.minimum(gu[:, :D_EXPERT], SWIGLU_LIMIT)
        up = jnp.clip(gu[:, D_EXPERT:], -SWIGLU_LIMIT, SWIGLU_LIMIT)
        act = gate * jax.nn.sigmoid(SWIGLU_ALPHA * gate) * (up + 1)
        return act @ w_down[e] + b_down[e]

    yb = lax.map(expert_block, (xb, block_e)).reshape(n_blocks * MOE_BLOCK, D)
    y_sorted = yb[dest] * gates.reshape(NK)[order][:, None]
    return jax.ops.segment_sum(y_sorted, tok, num_segments=N)


def trunk_layer(x, mod, lw, ctx):
    shift1, scale1, gate1, shift2, scale2, gate2 = jnp.split(mod, 6, axis=-1)
    h = rms_norm(x, lw['g_attn']) * (1 + scale1) + shift1
    o, ctx_out = token_mixers(h, lw, ctx)
    x = x + gate1 * o
    h = rms_norm(x, lw['g_ffn']) * (1 + scale2) + shift2
    B, T, D = x.shape
    y = moe_ffn(h.reshape(B * T, D), lw['w_router'], lw['b_router'], lw['w_gate_up'], lw['b_gate_up'],
                lw['w_down'], lw['b_down']).reshape(B, T, D)
    return x + gate2 * y, ctx_out


def setup_inputs(seed: int = 0) -> dict:
    key = jax.random.key(seed)
    ks = iter(jax.random.split(key, 48))
    nrm = lambda shape, s: jax.random.normal(next(ks), shape, F32) * s
    gain = lambda shape: 1.0 + nrm(shape, 0.02)
    L, D = DEPTH, D_MODEL
    return {
        'x_prompt': nrm((BATCH, SEQ, D), 1.0),
        'x_sample': nrm((DEC_BATCH, DEC_SEQ, D), 1.0),
        'cache_mla_ckv': nrm((DEC_BATCH, L, PAST_LEN, KV_LORA), 1.0),
        'cache_mla_krope': nrm((DEC_BATCH, L, PAST_LEN, MLA_ROPE), 1.0),
        'cache_gqa_k': nrm((DEC_BATCH, L, PAST_LEN, GQA_KV_HEADS, HEAD_DIM), 1.0),
        'cache_gqa_v': nrm((DEC_BATCH, L, PAST_LEN, GQA_KV_HEADS, HEAD_DIM), 1.0),
        'state_gla': nrm((DEC_BATCH, L, 2, GLA_HEADS, GLA_DK, GLA_DV), 0.3),
        'c': nrm((DEC_BATCH, D), 1.0),
        'c_ctx': nrm((D,), 1.0),
        'w_mod': nrm((L, D, 6 * D), 0.5 * D ** -0.5),
        'b_mod': nrm((L, 6 * D), 0.02),
        'g_attn_norm': gain((L, D)),
        'g_ffn_norm': gain((L, D)),
        'w_in': nrm((L, D, IN_DIM), D ** -0.5),
        'g_q_a': gain((L, Q_LORA)),
        'w_uq': nrm((L, Q_LORA, MLA_HEADS * MLA_QK), Q_LORA ** -0.5),
        'g_kv_a': gain((L, KV_LORA)),
        'w_ukv': nrm((L, KV_LORA, MLA_HEADS * (MLA_NOPE + MLA_V)), KV_LORA ** -0.5),
        'g_q_head': gain((L, HEAD_DIM)),
        'g_k_head': gain((L, HEAD_DIM)),
        'w_gk_fwd': nrm((L, GLA_GATE_RANK, GLA_HEADS * GLA_DK), GLA_GATE_RANK ** -0.5),
        'b_gk_fwd': nrm((L, GLA_HEADS * GLA_DK), 0.02),
        'w_gk_bwd': nrm((L, GLA_GATE_RANK, GLA_HEADS * GLA_DK), GLA_GATE_RANK ** -0.5),
        'b_gk_bwd': nrm((L, GLA_HEADS * GLA_DK), 0.02),
        'g_gla_out': gain((L, GLA_DV)),
        'w_out': nrm((L, MIX_WIDTH, D), MIX_WIDTH ** -0.5),
        'w_router': nrm((L, D, N_EXPERTS), D ** -0.5),
        'b_router': nrm((L, N_EXPERTS), 0.01),
        'w_gate_up': nrm((L, N_EXPERTS, D, 2 * D_EXPERT), D ** -0.5),
        'b_gate_up': nrm((L, N_EXPERTS, 2 * D_EXPERT), 0.02),
        'w_down': nrm((L, N_EXPERTS, D_EXPERT, D), D_EXPERT ** -0.5),
        'b_down': nrm((L, N_EXPERTS, D), 0.02),
        'g_final': gain((D,)),
    }


def reference(x_prompt, x_sample, cache_mla_ckv, cache_mla_krope, cache_gqa_k, cache_gqa_v, state_gla,
              c, c_ctx, w_mod, b_mod, g_attn_norm, g_ffn_norm, w_in, g_q_a, w_uq, g_kv_a, w_ukv,
              g_q_head, g_k_head, w_gk_fwd, b_gk_fwd, w_gk_bwd, b_gk_bwd, g_gla_out, w_out,
              w_router, b_router, w_gate_up, b_gate_up, w_down, b_down, g_final):
    xp, xs = x_prompt, x_sample
    new_ckv, new_krope, new_k, new_v, new_s = [], [], [], [], []
    for l in range(DEPTH):
        lw = dict(w_in=w_in[l], g_q_a=g_q_a[l], w_uq=w_uq[l], g_kv_a=g_kv_a[l], w_ukv=w_ukv[l],
                  g_q_head=g_q_head[l], g_k_head=g_k_head[l], w_gk_fwd=w_gk_fwd[l], b_gk_fwd=b_gk_fwd[l],
                  w_gk_bwd=w_gk_bwd[l], b_gk_bwd=b_gk_bwd[l], g_gla_out=g_gla_out[l], w_out=w_out[l],
                  g_attn=g_attn_norm[l], g_ffn=g_ffn_norm[l], w_router=w_router[l], b_router=b_router[l],
                  w_gate_up=w_gate_up[l], b_gate_up=b_gate_up[l], w_down=w_down[l], b_down=b_down[l])
        mod_ctx = (jax.nn.silu(c_ctx) @ w_mod[l] + b_mod[l])[None, None, :]
        xp, ctx_out = trunk_layer(xp, mod_ctx, lw, None)
        new_ckv.append(ctx_out[0])
        new_krope.append(ctx_out[1])
        new_k.append(ctx_out[2])
        new_v.append(ctx_out[3])
        new_s.append(ctx_out[4])
        mod_lat = (jax.nn.silu(c) @ w_mod[l] + b_mod[l])[:, None, :]
        ctx_l = (cache_mla_ckv[:, l], cache_mla_krope[:, l], cache_gqa_k[:, l], cache_gqa_v[:, l],
                 state_gla[:, l])
        xs, _ = trunk_layer(xs, mod_lat, lw, ctx_l)
    y_prompt = rms_norm(xp, g_final)
    y_sample = rms_norm(xs, g_final)
    return (y_prompt, y_sample, jnp.stack(new_ckv, axis=1), jnp.stack(new_krope, axis=1),
            jnp.stack(new_k, axis=1), jnp.stack(new_v, axis=1), jnp.stack(new_s, axis=1))
```

```python
import functools

import jax
import jax.numpy as jnp
import numpy as np
from jax import lax
from jax.experimental import pallas as pl
from jax.experimental.pallas import tpu as pltpu

F32 = jnp.float32
BF16 = jnp.bfloat16
I32 = jnp.int32

D_MODEL = 1024
GRID_W = 64
ROPE_THETA = 10000.0
EPS = 1e-6
MLA_HEADS = 4
Q_LORA = 256
KV_LORA = 128
MLA_NOPE = 64
MLA_ROPE = 32
MLA_V = 64
MLA_QK = MLA_NOPE + MLA_ROPE
GQA_HEADS = 8
GQA_KV_HEADS = 2
GQA_GROUP = GQA_HEADS // GQA_KV_HEADS
HEAD_DIM = 64
GLA_HEADS = 4
GLA_DK = 32
GLA_DV = 64
GLA_GATE_RANK = 16
GLA_GATE_NORM = 16.0
N_EXPERTS = 32
TOP_K = 4
D_EXPERT = D_MODEL
SWIGLU_LIMIT = 7.0
SWIGLU_ALPHA = 1.702

LANES = 128
TOKEN_TILE = 256
GLA_STEP = 16
MOE_ROWS = 512
ATTN_TQ = 512
ATTN_TK = 512
VMEM_LIMIT = 56 * 1024 * 1024
NEG_BIG = -1e30

C_CQ, C_CKV, C_GQ, C_GK, C_GV = 0, 256, 384, 896, 1024
C_LQ, C_LK, C_LV, C_LG, C_KR, C_GATE = 1152, 1280, 1408, 1664, 1920, 2048
IN_COLS = 2176


def _dot(a, b):
    return jnp.dot(a, b, preferred_element_type=F32)


def _dot_nt(a, b):
    return lax.dot_general(a, b, (((1,), (1,)), ((), ())), preferred_element_type=F32)


def _dot_tn(a, b):
    return lax.dot_general(a, b, (((0,), (0,)), ((), ())), preferred_element_type=F32)


def _split_dot(x, ones_bf):
    hi = x.astype(BF16)
    lo = (x - hi.astype(F32)).astype(BF16)
    return _dot(hi, ones_bf) + _dot(lo, ones_bf)


def _rope(x, cos, sin_signed, quarter):
    width = x.shape[-1]
    lane = lax.broadcasted_iota(I32, x.shape, 1)
    first = (lane % (2 * quarter)) < quarter
    partner = jnp.where(first, pltpu.roll(x, width - quarter, 1), pltpu.roll(x, quarter, 1))
    return x * cos + partner * sin_signed


def _mod_kernel(c_ref, w_ref, b_ref, o_ref):
    c = c_ref[...]
    a = c * jax.nn.sigmoid(c)
    o_ref[0] = jnp.dot(a, w_ref[0], precision=lax.Precision.HIGHEST, preferred_element_type=F32) + b_ref[0]


def _modulation(cvecs, w_mod, b_mod):
    L, D, D6 = w_mod.shape
    tn = 1536
    return pl.pallas_call(
        _mod_kernel,
        grid=(L, D6 // tn),
        in_specs=[pl.BlockSpec((16, D), lambda l, j: (0, 0)),
                  pl.BlockSpec((1, D, tn), lambda l, j: (l, 0, j)),
                  pl.BlockSpec((1, 1, tn), lambda l, j: (l, 0, j))],
        out_specs=pl.BlockSpec((1, 16, tn), lambda l, j: (l, 0, j)),
        out_shape=jax.ShapeDtypeStruct((L, 16, D6), F32),
        compiler_params=pltpu.CompilerParams(dimension_semantics=("arbitrary", "arbitrary"),
                                             vmem_limit_bytes=VMEM_LIMIT),
        name="modulation",
    )(cvecs, w_mod, b_mod.reshape(L, 1, D6))


def _fold_kernel(a_ref, b_ref, o_ref):
    o_ref[0, 0] = lax.dot_general(a_ref[0, 0], b_ref[0, 0], (((1,), (1,)), ((), ())),
                                  precision=lax.Precision.HIGHEST, preferred_element_type=F32)


def _fold_q_weights(wq_nope, wk_nope):
    L, H, A, K = wq_nope.shape
    B = wk_nope.shape[2]
    return pl.pallas_call(
        _fold_kernel,
        grid=(L, H),
        in_specs=[pl.BlockSpec((1, 1, A, K), lambda l, h: (l, h, 0, 0)),
                  pl.BlockSpec((1, 1, B, K), lambda l, h: (l, h, 0, 0))],
        out_specs=pl.BlockSpec((1, 1, A, B), lambda l, h: (l, h, 0, 0)),
        out_shape=jax.ShapeDtypeStruct((L, H, A, B), F32),
        name="fold_q_weights",
    )(wq_nope, wk_nope)


def _pre_attn_kernel(modidx_ref, tabidx_ref, *refs, has_y):
    del modidx_ref, tabidx_ref
    if has_y:
        x_ref, y_ref, pmod_ref = refs[:3]
        refs = refs[3:]
    else:
        x_ref = refs[0]
        refs = refs[1:]
    (mod_ref, tab_ref, gattn_ref, win_ref, gqa_ref, wq2_ref, gkva_ref, gqh_ref, gkh_ref, ones_ref,
     wg_ref, bg_ref) = refs[:12]
    outs = refs[12:]
    if has_y:
        xres_ref = outs[0]
        outs = outs[1:]
    qm_ref, kvc_ref, qg_ref, kdup_ref, vdup_ref, gla_ref, cache_ref = outs

    x = x_ref[...]
    if has_y:
        x = x + pmod_ref[0, 5:6, :] * y_ref[...]
        xres_ref[...] = x
    mod = mod_ref[0]
    shift1, scale1 = mod[0:1, :], mod[1:2, :]
    xn = x * lax.rsqrt(jnp.mean(x * x, axis=-1, keepdims=True) + EPS) * gattn_ref[...]
    h = xn * (1.0 + scale1) + shift1
    p = _dot(h.astype(BF16), win_ref[...])

    tab = tab_ref[0]
    cos64, sin64 = tab[:, 0:128], tab[:, 128:256]
    cos32, sin32 = tab[:, 256:384], tab[:, 384:512]
    ones = ones_ref[...]

    cq = p[:, C_CQ:C_CQ + Q_LORA]
    cqn = cq * lax.rsqrt(jnp.mean(cq * cq, axis=-1, keepdims=True) + EPS) * gqa_ref[...]
    qm = _dot(cqn.astype(BF16), wq2_ref[...])
    q_rope = _rope(qm[:, 512:640], cos32, sin32, MLA_ROPE // 4)
    qm_ref[:, 0:512] = qm[:, 0:512].astype(BF16)
    qm_ref[:, 512:640] = q_rope.astype(BF16)
    ckv_raw = p[:, C_CKV:C_CKV + KV_LORA]
    ckv = ckv_raw * lax.rsqrt(jnp.mean(ckv_raw * ckv_raw, axis=-1, keepdims=True) + EPS) * gkva_ref[...]
    kr4 = p[:, C_KR:C_KR + 128]
    kr4_rot = _rope(kr4, cos32, sin32, MLA_ROPE // 4)
    kvc_ref[:, 0:128] = ckv.astype(BF16)
    kvc_ref[:, 128:256] = kr4_rot.astype(BF16)

    gq = p[:, C_GQ:C_GQ + 512]
    ms_q = _split_dot(gq * gq, ones) * (1.0 / HEAD_DIM)
    qn = gq * lax.rsqrt(ms_q + EPS) * gqh_ref[...]
    cos_q = jnp.concatenate([cos64] * 4, axis=1)
    sin_q = jnp.concatenate([sin64] * 4, axis=1)
    qg = _rope(qn, cos_q, sin_q, HEAD_DIM // 4) * (HEAD_DIM ** -0.5)
    qg_ref[...] = qg.astype(BF16)
    gk = p[:, C_GK:C_GK + 128]
    ms_k = _split_dot(gk * gk, ones[0:128, 0:128]) * (1.0 / HEAD_DIM)
    kn = gk * lax.rsqrt(ms_k + EPS) * gkh_ref[...]
    kg = _rope(kn, cos64, sin64, HEAD_DIM // 4)
    gv = p[:, C_GV:C_GV + 128]
    lane = lax.broadcasted_iota(I32, kg.shape, 1)
    low = lane < HEAD_DIM
    kg_sw = pltpu.roll(kg, HEAD_DIM, 1)
    gv_sw = pltpu.roll(gv, HEAD_DIM, 1)
    kdup_ref[:, 0:128] = jnp.where(low, kg, kg_sw).astype(BF16)
    kdup_ref[:, 128:256] = jnp.where(low, kg_sw, kg).astype(BF16)
    vdup_ref[:, 0:128] = jnp.where(low, gv, gv_sw).astype(BF16)
    vdup_ref[:, 128:256] = jnp.where(low, gv_sw, gv).astype(BF16)

    z = _dot(p[:, C_GATE:C_GATE + 128].astype(BF16), wg_ref[...]) + bg_ref[...]
    logsig = jnp.minimum(z, 0.0) - jnp.log1p(jnp.exp(-jnp.abs(z)))
    gla_ref[:, 0:128] = p[:, C_LQ:C_LQ + 128] * (GLA_DK ** -0.5)
    gla_ref[:, 128:256] = p[:, C_LK:C_LK + 128]
    gla_ref[:, 256:512] = logsig * (1.0 / GLA_GATE_NORM)
    gla_ref[:, 512:768] = p[:, C_LV:C_LV + 256]
    gla_ref[:, 768:1024] = p[:, C_LG:C_LG + 256]

    cache_ref[:, 0:128] = ckv
    cache_ref[:, 128:256] = kn
    cache_ref[:, 256:384] = gv
    cache_ref[:, 384:512] = kr4


def _pre_attn(x, y, pmod, mod, tab, modidx, tabidx, lw, n_ctx_tiles):
    N, D = x.shape
    tm = TOKEN_TILE
    nt = N // tm
    has_y = y is not None
    row = lambda i, mi, ti: (i, 0)
    const2 = lambda i, mi, ti: (0, 0)
    in_specs = [pl.BlockSpec((tm, D), row)]
    args = [x]
    if has_y:
        in_specs += [pl.BlockSpec((tm, D), row), pl.BlockSpec((1, 6, D), lambda i, mi, ti: (mi[i], 0, 0))]
        args += [y, pmod]
    in_specs += [
        pl.BlockSpec((1, 6, D), lambda i, mi, ti: (mi[i], 0, 0)),
        pl.BlockSpec((1, tm, 512), lambda i, mi, ti: (ti[i], 0, 0)),
        pl.BlockSpec((1, D), const2),
        pl.BlockSpec((D, IN_COLS), const2),
        pl.BlockSpec((1, Q_LORA), const2),
        pl.BlockSpec((Q_LORA, 640), const2),
        pl.BlockSpec((1, KV_LORA), const2),
        pl.BlockSpec((1, 512), const2),
        pl.BlockSpec((1, 128), const2),
        pl.BlockSpec((512, 512), const2),
        pl.BlockSpec((128, 256), const2),
        pl.BlockSpec((1, 256), const2),
    ]
    args += [mod, tab, lw['g_attn'], lw['w_in_r'], lw['g_q_a'], lw['w_q2'], lw['g_kv_a'], lw['g_qh'], lw['g_kh'],
             lw['ones512'], lw['w_gate'], lw['b_gate']]
    out_shape, out_specs = [], []
    if has_y:
        out_shape.append(jax.ShapeDtypeStruct((N, D), F32))
        out_specs.append(pl.BlockSpec((tm, D), row))
    for width, dt in ((640, BF16), (256, BF16), (512, BF16), (256, BF16), (256, BF16), (1024, F32)):
        out_shape.append(jax.ShapeDtypeStruct((N, width), dt))
        out_specs.append(pl.BlockSpec((tm, width), row))
    out_shape.append(jax.ShapeDtypeStruct(((n_ctx_tiles + 1) * tm, 512), F32))
    out_specs.append(pl.BlockSpec((tm, 512), lambda i, mi, ti: (jnp.minimum(i, n_ctx_tiles), 0)))
    outs = pl.pallas_call(
        functools.partial(_pre_attn_kernel, has_y=has_y),
        grid_spec=pltpu.PrefetchScalarGridSpec(num_scalar_prefetch=2, grid=(nt,), in_specs=in_specs,
                                               out_specs=out_specs),
        out_shape=out_shape,
        compiler_params=pltpu.CompilerParams(dimension_semantics=("arbitrary",), vmem_limit_bytes=VMEM_LIMIT),
        name="pre_attn",
    )(modidx, tabidx, *args)
    if not has_y:
        outs = [x] + list(outs)
    return outs


def _online_softmax_step(q, k, v, m_ref, l_ref, acc_ref, h):
    s = _dot_nt(q, k)
    m_old = m_ref[h]
    m_new = jnp.maximum(m_old, jnp.max(s, axis=-1, keepdims=True))
    alpha = jnp.exp(m_old - m_new)
    pr = jnp.exp(s - m_new)
    l_ref[h] = alpha * l_ref[h] + jnp.sum(pr, axis=-1, keepdims=True)
    acc_ref[h] = alpha * acc_ref[h] + _dot(pr.astype(BF16), v)
    m_ref[h] = m_new


def _attn_init(m_ref, l_ref, acc_ref):
    m_ref[...] = jnp.full(m_ref.shape, NEG_BIG, F32)
    l_ref[...] = jnp.zeros(l_ref.shape, F32)
    acc_ref[...] = jnp.zeros(acc_ref.shape, F32)


def _mla_attn_kernel(*refs, has_prev, has_ctx, n_chunks, tk):
    if has_prev:
        refs = refs[1:]
    if has_ctx:
        q_ref, kc_ref, k_ref, wuv_ref, o_ref, m_ref, l_ref, acc_ref = refs
    else:
        q_ref, k_ref, wuv_ref, o_ref, m_ref, l_ref, acc_ref = refs
        kc_ref = None
    _attn_init(m_ref, l_ref, acc_ref)
    q = q_ref[...]
    rope = q[:, 512:640]
    lane = lax.broadcasted_iota(I32, rope.shape, 1)
    zero = jnp.zeros_like(rope)
    qh = [jnp.concatenate([q[:, h * 128:(h + 1) * 128],
                           jnp.where((lane // MLA_ROPE) == h, rope, zero)], axis=1) for h in range(MLA_HEADS)]

    def process(kblk):
        for h in range(MLA_HEADS):
            _online_softmax_step(qh[h], kblk, kblk[:, 0:128], m_ref, l_ref, acc_ref, h)

    if has_ctx:
        process(kc_ref[0])

    def body(j, carry):
        process(k_ref[pl.ds(pl.multiple_of(j * tk, tk), tk), :])
        return carry

    lax.fori_loop(0, n_chunks, body, 0)
    out = jnp.zeros(o_ref.shape, F32)
    for h in range(MLA_HEADS):
        o_lat = acc_ref[h] / l_ref[h]
        out = out + _dot(o_lat.astype(BF16), wuv_ref[h])
    o_ref[...] = out.astype(o_ref.dtype)


def _gqa_attn_kernel(*refs, has_prev, has_ctx, n_chunks, tk):
    if has_prev:
        refs = refs[1:]
    if has_ctx:
        q_ref, kc_ref, vc_ref, k_ref, v_ref, o_ref, m_ref, l_ref, acc_ref = refs
    else:
        q_ref, k_ref, v_ref, o_ref, m_ref, l_ref, acc_ref = refs
        kc_ref = vc_ref = None
    _attn_init(m_ref, l_ref, acc_ref)
    q = q_ref[...]
    lane = lax.broadcasted_iota(I32, (q.shape[0], 128), 1)
    low = lane < HEAD_DIM
    qh = []
    for h in range(GQA_HEADS):
        pair = q[:, (h // 2) * 128:(h // 2 + 1) * 128]
        keep = low if h % 2 == 0 else jnp.logical_not(low)
        qh.append(jnp.where(keep, pair, jnp.zeros_like(pair)))

    def process(kblk, vblk):
        for h in range(GQA_HEADS):
            g = h // GQA_GROUP
            _online_softmax_step(qh[h], kblk[:, g * 128:(g + 1) * 128], vblk[:, g * 128:(g + 1) * 128],
                                 m_ref, l_ref, acc_ref, h)

    if has_ctx:
        process(kc_ref[0], vc_ref[0])

    def body(j, carry):
        sl = pl.ds(pl.multiple_of(j * tk, tk), tk)
        process(k_ref[sl, :], v_ref[sl, :])
        return carry

    lax.fori_loop(0, n_chunks, body, 0)
    for j in range(GQA_HEADS // 2):
        even = acc_ref[2 * j] / l_ref[2 * j]
        odd = acc_ref[2 * j + 1] / l_ref[2 * j + 1]
        o_ref[:, j * 128:(j + 1) * 128] = jnp.where(low, even, odd).astype(o_ref.dtype)


def _attention(kind, q, k, v, ctx, wuv, prev, *, n_seq, seq_len, tile_offset):
    has_ctx = ctx is not None
    tq = min(ATTN_TQ, seq_len)
    tk = min(ATTN_TK, seq_len)
    nq = seq_len // tq
    qw = q.shape[1]
    ow = 256 if kind == 'mla' else 512
    heads = MLA_HEADS if kind == 'mla' else GQA_HEADS
    qmap = lambda b, i: (tile_offset * nq + b * nq + i, 0)
    kmap = lambda b, i: (tile_offset + b, 0)
    cmap = lambda b, i: (b, 0, 0)
    in_specs = [pl.BlockSpec((tq, qw), qmap)]
    args = [q]
    aliases = {}
    if prev is not None:
        in_specs = [pl.BlockSpec(memory_space=pl.ANY)] + in_specs
        args = [prev] + args
        aliases = {0: 0}
    if kind == 'mla':
        if has_ctx:
            in_specs.append(pl.BlockSpec((1,) + ctx[0].shape[1:], cmap))
            args.append(ctx[0])
        in_specs += [pl.BlockSpec((seq_len, 256), kmap), pl.BlockSpec((MLA_HEADS, 128, 256), lambda b, i: (0, 0, 0))]
        args += [k, wuv]
        body = _mla_attn_kernel
    else:
        if has_ctx:
            in_specs += [pl.BlockSpec((1,) + ctx[0].shape[1:], cmap), pl.BlockSpec((1,) + ctx[1].shape[1:], cmap)]
            args += [ctx[0], ctx[1]]
        in_specs += [pl.BlockSpec((seq_len, 256), kmap), pl.BlockSpec((seq_len, 256), kmap)]
        args += [k, v]
        body = _gqa_attn_kernel
    return pl.pallas_call(
        functools.partial(body, has_prev=prev is not None, has_ctx=has_ctx, n_chunks=seq_len // tk, tk=tk),
        grid=(n_seq, nq),
        in_specs=in_specs,
        out_specs=pl.BlockSpec((tq, ow), qmap),
        out_shape=jax.ShapeDtypeStruct((q.shape[0], ow), BF16),
        input_output_aliases=aliases,
        scratch_shapes=[pltpu.VMEM((heads, tq, 1), F32), pltpu.VMEM((heads, tq, 1), F32),
                        pltpu.VMEM((heads, tq, 128), F32)],
        compiler_params=pltpu.CompilerParams(dimension_semantics=("arbitrary", "arbitrary"),
                                             vmem_limit_bytes=VMEM_LIMIT),
        name=kind + ("_attn_latent" if has_ctx else "_attn_context"),
    )(*args)


def _gla_kernel(blk_ref, seq_ref, first_ref, *refs, reverse):
    del blk_ref, seq_ref
    if reverse:
        gin_ref, s0_ref, he_ref, of_ref, gout_ref, ones_ref, o_ref, sout_ref, st_ref = refs
    else:
        gin_ref, s0_ref, he_ref, o_ref, sout_ref, st_ref = refs
    i = pl.program_id(0)

    @pl.when(first_ref[i] == 1)
    def _():
        st_ref[...] = s0_ref[0]

    R, S = TOKEN_TILE, GLA_STEP
    ns = R // S
    q = gin_ref[:, 0:128]
    k = gin_ref[:, 128:256]
    g = gin_ref[:, 384:512] if reverse else gin_ref[:, 256:384]
    v = gin_ref[:, 512:768]
    pos = lax.broadcasted_iota(I32, (R, 128), 0) % S
    b = g
    for s in (1, 2, 4, 8):
        if reverse:
            b = b + jnp.where(pos < S - s, pltpu.roll(b, R - s, 0), 0.0)
        else:
            b = b + jnp.where(pos >= s, pltpu.roll(b, s, 0), 0.0)

    def step_row(a, j):
        w = a.shape[-1]
        a3 = a.reshape(ns, S, w)
        return jnp.broadcast_to(a3[:, j:j + 1, :], (ns, S, w)).reshape(R, w)

    b_edge = step_row(b, 0 if reverse else S - 1)
    qt = (q * jnp.exp(b)).astype(BF16)
    kt = (k * jnp.exp(b_edge - b)).astype(BF16)
    d_edge = jnp.exp(b_edge)
    vb = v.astype(BF16)
    he = he_ref[...]

    o_intra = jnp.zeros((R, 256), F32)
    for j in range(S):
        cond = (pos <= j) if reverse else (pos >= j)
        bj, kj, vj = step_row(b, j), step_row(k, j), step_row(v, j)
        t = jnp.where(cond, q * kj * jnp.exp(jnp.where(cond, b - bj, 0.0)), 0.0)
        o_intra = o_intra + _dot(t.astype(BF16), he) * vj

    own_head = (lax.broadcasted_iota(I32, (256, 128), 0) // GLA_DV) == (lax.broadcasted_iota(I32, (256, 128), 1) // GLA_DK)
    st = st_ref[...]
    parts = [None] * ns
    for t in (range(ns - 1, -1, -1) if reverse else range(ns)):
        rows = slice(t * S, (t + 1) * S)
        parts[t] = _dot_nt(qt[rows], st.astype(BF16)) + o_intra[rows]
        ut = _dot_tn(vb[rows], kt[rows])
        st = d_edge[t * S:t * S + 1, :] * st + jnp.where(own_head, ut, 0.0)
    st_ref[...] = st
    sout_ref[0] = st
    o = jnp.concatenate(parts, axis=0)
    if reverse:
        o = o + of_ref[...]
        ms = _split_dot(o * o, ones_ref[...]) * (1.0 / GLA_DV)
        on = o * lax.rsqrt(ms + EPS) * gout_ref[...]
        lg = gin_ref[:, 768:1024]
        o_ref[...] = (on * (lg * jax.nn.sigmoid(lg))).astype(o_ref.dtype)
    else:
        o_ref[...] = o


def _gla(gin, s0, o_f, lw, order, *, reverse):
    blk, seq, first = order
    N = gin.shape[0]
    tm = TOKEN_TILE
    n_seq = s0.shape[0]
    tile = lambda i, b, s, f: (b[i], 0)
    state = lambda i, b, s, f: (s[i], 0, 0)
    const2 = lambda i, b, s, f: (0, 0)
    in_specs = [pl.BlockSpec((tm, 1024), tile), pl.BlockSpec((1, 256, 128), state), pl.BlockSpec((128, 256), const2)]
    args = [gin, s0, lw['head_expand']]
    if reverse:
        in_specs += [pl.BlockSpec((tm, 256), tile), pl.BlockSpec((1, 256), const2), pl.BlockSpec((256, 256), const2)]
        args += [o_f, lw['g_gla_out'], lw['ones256']]
    return pl.pallas_call(
        functools.partial(_gla_kernel, reverse=reverse),
        grid_spec=pltpu.PrefetchScalarGridSpec(
            num_scalar_prefetch=3, grid=(N // tm,), in_specs=in_specs,
            out_specs=[pl.BlockSpec((tm, 256), tile), pl.BlockSpec((1, 256, 128), state)],
            scratch_shapes=[pltpu.VMEM((256, 128), F32)]),
        out_shape=[jax.ShapeDtypeStruct((N, 256), BF16 if reverse else F32),
                   jax.ShapeDtypeStruct((n_seq, 256, 128), F32)],
        compiler_params=pltpu.CompilerParams(dimension_semantics=("arbitrary",), vmem_limit_bytes=VMEM_LIMIT),
        name="gla_bwd" if reverse else "gla_fwd",
    )(blk, seq, first, *args)


def _post_attn_kernel(modidx_ref, x_ref, om_ref, og_ref, ol_ref, mod_ref, wout_ref, gffn_ref, wrh_ref, wrl_ref,
                      br_ref, x1_ref, h2_ref, ti_ref, tg_ref):
    del modidx_ref
    mod = mod_ref[0]
    gate1, shift2, scale2 = mod[2:3, :], mod[3:4, :], mod[4:5, :]
    o = (_dot(om_ref[...], wout_ref[0:256, :]) + _dot(og_ref[...], wout_ref[256:768, :])
         + _dot(ol_ref[...], wout_ref[768:1024, :]))
    x1 = x_ref[...] + gate1 * o
    x1_ref[...] = x1
    xn = x1 * lax.rsqrt(jnp.mean(x1 * x1, axis=-1, keepdims=True) + EPS) * gffn_ref[...]
    h2 = xn * (1.0 + scale2) + shift2
    hi = h2.astype(BF16)
    h2_ref[...] = hi
    lo = (h2 - hi.astype(F32)).astype(BF16)
    wrh = wrh_ref[...]
    logits = _dot(hi, wrh) + _dot(hi, wrl_ref[...]) + _dot(lo, wrh) + br_ref[...]

    lane = lax.broadcasted_iota(I32, logits.shape, 1).astype(F32)
    work = logits
    vals, idxs = [], []
    for _ in range(TOP_K):
        m = jnp.max(work, axis=-1, keepdims=True)
        idx = jnp.min(jnp.where(work == m, lane, float(LANES)), axis=-1, keepdims=True)
        vals.append(m)
        idxs.append(idx)
        work = jnp.where(lane == idx, NEG_BIG * 4.0, work)
    es = [jnp.exp(vv - vals[0]) for vv in vals]
    denom = es[0] + es[1] + es[2] + es[3]
    ti = jnp.zeros(logits.shape, F32)
    tg = jnp.zeros(logits.shape, F32)
    for r in range(TOP_K):
        ti = jnp.where(lane == float(r), idxs[r], ti)
        tg = jnp.where(lane == float(r), es[r] / denom, tg)
    ti_ref[...] = ti.astype(I32)
    tg_ref[...] = tg


def _post_attn(x, o_mla, o_gqa, o_gla, mod, modidx, lw):
    N, D = x.shape
    tm = TOKEN_TILE
    row = lambda i, mi: (i, 0)
    const2 = lambda i, mi: (0, 0)
    return pl.pallas_call(
        _post_attn_kernel,
        grid_spec=pltpu.PrefetchScalarGridSpec(
            num_scalar_prefetch=1, grid=(N // tm,),
            in_specs=[pl.BlockSpec((tm, D), row), pl.BlockSpec((tm, 256), row), pl.BlockSpec((tm, 512), row),
                      pl.BlockSpec((tm, 256), row), pl.BlockSpec((1, 6, D), lambda i, mi: (mi[i], 0, 0)),
                      pl.BlockSpec((D, D), const2), pl.BlockSpec((1, D), const2),
                      pl.BlockSpec((D, LANES), const2), pl.BlockSpec((D, LANES), const2),
                      pl.BlockSpec((1, LANES), const2)],
            out_specs=[pl.BlockSpec((tm, D), row), pl.BlockSpec((tm, D), row), pl.BlockSpec((tm, LANES), row),
                       pl.BlockSpec((tm, LANES), row)]),
        out_shape=[jax.ShapeDtypeStruct((N, D), F32), jax.ShapeDtypeStruct((N, D), BF16),
                   jax.ShapeDtypeStruct((N, LANES), I32), jax.ShapeDtypeStruct((N, LANES), F32)],
        compiler_params=pltpu.CompilerParams(dimension_semantics=("arbitrary",), vmem_limit_bytes=VMEM_LIMIT),
        name="post_attn",
    )(modidx, x, o_mla, o_gqa, o_gla, mod, lw['w_out'], lw['g_ffn'], lw['w_router_hi'], lw['w_router_lo'],
      lw['b_router'])


def _moe_kernel(be_ref, bx_ref, valid_ref, first_ref, x_ref, wgu_ref, bgu_ref, wd_ref, bd_ref, o_ref,
                wgu_bf, wd_bf):
    del be_ref, bx_ref
    i = pl.program_id(0)

    @pl.when(first_ref[i] == 1)
    def _():
        wgu_bf[...] = wgu_ref[...].astype(BF16)
        wd_bf[...] = wd_ref[...].astype(BF16)

    @pl.when(valid_ref[i] == 1)
    def _():
        gu = _dot(x_ref[...], wgu_bf[...]) + bgu_ref[...]
        gate = jnp.minimum(gu[:, :D_EXPERT], SWIGLU_LIMIT)
        up = jnp.clip(gu[:, D_EXPERT:], -SWIGLU_LIMIT, SWIGLU_LIMIT)
        act = gate * jax.nn.sigmoid(SWIGLU_ALPHA * gate) * (up + 1.0)
        o_ref[...] = _dot(act.astype(BF16), wd_bf[...]) + bd_ref[...]

    @pl.when(valid_ref[i] == 0)
    def _():
        o_ref[...] = jnp.zeros(o_ref.shape, o_ref.dtype)


def _moe_experts(xs, sched, w_gate_up, b_gate_up, w_down, b_down, layer):
    block_e, block_x, valid, first = sched
    D = xs.shape[1]
    n_blocks = xs.shape[0] // MOE_ROWS
    L, E = w_gate_up.shape[:2]
    wmap = lambda i, be, bx, va, fi: (layer, be[i], 0, 0)
    xmap = lambda i, be, bx, va, fi: (bx[i], 0)
    return pl.pallas_call(
        _moe_kernel,
        grid_spec=pltpu.PrefetchScalarGridSpec(
            num_scalar_prefetch=4, grid=(n_blocks,),
            in_specs=[pl.BlockSpec((MOE_ROWS, D), xmap),
                      pl.BlockSpec((None, None, D, 2 * D_EXPERT), wmap),
                      pl.BlockSpec((None, None, 1, 2 * D_EXPERT), wmap),
                      pl.BlockSpec((None, None, D_EXPERT, D), wmap),
                      pl.BlockSpec((None, None, 1, D), wmap)],
            out_specs=pl.BlockSpec((MOE_ROWS, D), lambda i, be, bx, va, fi: (i, 0)),
            scratch_shapes=[pltpu.VMEM((D, 2 * D_EXPERT), BF16), pltpu.VMEM((D_EXPERT, D), BF16)]),
        out_shape=jax.ShapeDtypeStruct(xs.shape, F32),
        compiler_params=pltpu.CompilerParams(dimension_semantics=("arbitrary",), vmem_limit_bytes=VMEM_LIMIT),
        name="moe_experts",
    )(block_e, block_x, valid, first, xs, w_gate_up, b_gate_up.reshape(L, E, 1, 2 * D_EXPERT), w_down,
      b_down.reshape(L, E, 1, D))


def _route(top_idx, n_blocks):
    N = top_idx.shape[0]
    onehot = (top_idx[:, :, None] == jnp.arange(N_EXPERTS, dtype=I32)[None, None, :]).astype(I32).sum(axis=1)
    before = jnp.cumsum(onehot, axis=0) - onehot
    counts = onehot.sum(axis=0)
    padded = (counts + MOE_ROWS - 1) // MOE_ROWS * MOE_ROWS
    pad_end = jnp.cumsum(padded)
    pad_start = pad_end - padded
    dest = pad_start[top_idx] + jnp.take_along_axis(before, top_idx, axis=1)
    tok = jnp.broadcast_to(jnp.arange(N, dtype=I32)[:, None], dest.shape)
    slot_tok = jnp.zeros((n_blocks * MOE_ROWS,), I32).at[dest.reshape(-1)].set(tok.reshape(-1))
    blk_start = jnp.arange(n_blocks, dtype=I32) * MOE_ROWS
    valid = blk_start < pad_end[-1]
    block_e = jnp.minimum(jnp.searchsorted(pad_end, blk_start, side='right'), N_EXPERTS - 1).astype(I32)
    n_valid = pad_end[-1] // MOE_ROWS
    block_x = jnp.minimum(jnp.arange(n_blocks, dtype=I32), n_valid - 1).astype(I32)
    block_e = block_e[block_x]
    prev_e = jnp.concatenate([jnp.full((1,), -1, I32), block_e[:-1]])
    first = (valid & (block_e != prev_e)).astype(I32)
    return dest, slot_tok, (block_e, block_x, valid.astype(I32), first)


def _final_kernel(modidx_ref, x_ref, y_ref, pmod_ref, g_ref, o_ref):
    del modidx_ref
    x = x_ref[...] + pmod_ref[0, 5:6, :] * y_ref[...]
    o_ref[...] = x * lax.rsqrt(jnp.mean(x * x, axis=-1, keepdims=True) + EPS) * g_ref[...]


def _final_norm(x, y, pmod, modidx, g_final):
    N, D = x.shape
    tm = TOKEN_TILE
    row = lambda i, mi: (i, 0)
    return pl.pallas_call(
        _final_kernel,
        grid_spec=pltpu.PrefetchScalarGridSpec(
            num_scalar_prefetch=1, grid=(N // tm,),
            in_specs=[pl.BlockSpec((tm, D), row), pl.BlockSpec((tm, D), row),
                      pl.BlockSpec((1, 6, D), lambda i, mi: (mi[i], 0, 0)), pl.BlockSpec((1, D), lambda i, mi: (0, 0))],
            out_specs=pl.BlockSpec((tm, D), row)),
        out_shape=jax.ShapeDtypeStruct((N, D), F32),
        compiler_params=pltpu.CompilerParams(dimension_semantics=("arbitrary",), vmem_limit_bytes=VMEM_LIMIT),
        name="final_norm",
    )(modidx, x, y, pmod, g_final)


def _rope_tables(seq_len):
    pos = np.arange(seq_len)
    rowp = (pos // GRID_W).astype(np.float32)
    colp = (pos % GRID_W).astype(np.float32)

    def tables(rdim, copies):
        quarter = rdim // 4
        inv_freq = jnp.asarray(ROPE_THETA, F32) ** (-jnp.arange(quarter, dtype=F32) / quarter)
        ar = jnp.asarray(rowp)[:, None] * inv_freq[None, :]
        ac = jnp.asarray(colp)[:, None] * inv_freq[None, :]
        cos = jnp.concatenate([jnp.cos(ar), jnp.cos(ar), jnp.cos(ac), jnp.cos(ac)], axis=1)
        sin = jnp.concatenate([-jnp.sin(ar), jnp.sin(ar), -jnp.sin(ac), jnp.sin(ac)], axis=1)
        return jnp.tile(cos, (1, copies)), jnp.tile(sin, (1, copies))

    c64, s64 = tables(HEAD_DIM, 2)
    c32, s32 = tables(MLA_ROPE, 4)
    lat = jnp.concatenate([c64, s64, c32, s32], axis=1).reshape(seq_len // TOKEN_TILE, TOKEN_TILE, 512)
    ident = jnp.concatenate([jnp.ones((TOKEN_TILE, 128), F32), jnp.zeros((TOKEN_TILE, 128), F32)] * 2, axis=1)
    return jnp.concatenate([ident[None], lat], axis=0)


def _block_ones(n, blk):
    r = np.arange(n) // blk
    return jnp.asarray((r[:, None] == r[None, :]).astype(np.float32), BF16)


def kernel(x_prompt, x_sample, cache_mla_ckv, cache_mla_krope, cache_gqa_k, cache_gqa_v, state_gla, c, c_ctx, w_mod, b_mod, g_attn_norm, g_ffn_norm, w_in, g_q_a, w_uq, g_kv_a, w_ukv, g_q_head, g_k_head, w_gk_fwd, b_gk_fwd, w_gk_bwd, b_gk_bwd, g_gla_out, w_out, w_router, b_router, w_gate_up, b_gate_up, w_down, b_down, g_final):
    B, T, D = x_prompt.shape
    BD, TD, _ = x_sample.shape
    L = w_in.shape[0]
    P = cache_mla_ckv.shape[2]
    tm = TOKEN_TILE
    n_ctx = B * T
    N = n_ctx + BD * TD
    assert T == tm and TD % ATTN_TQ == 0 and n_ctx % TD == 0 and BD + 1 <= 16
    n_ctx_tiles = n_ctx // tm
    nt = N // tm
    tiles_per_lat = TD // tm

    tile_ids = np.arange(nt)
    lat_tile = np.maximum(tile_ids - n_ctx_tiles, 0)
    is_lat = tile_ids >= n_ctx_tiles
    modidx = jnp.asarray(np.where(is_lat, 1 + lat_tile // tiles_per_lat, 0), I32)
    tabidx = jnp.asarray(np.where(is_lat, 1 + lat_tile % tiles_per_lat, 0), I32)
    seq_of_tile = np.where(is_lat, B + lat_tile // tiles_per_lat, tile_ids)
    first_fwd = np.where(is_lat, lat_tile % tiles_per_lat == 0, True)
    last_fwd = np.where(is_lat, lat_tile % tiles_per_lat == tiles_per_lat - 1, True)
    order_fwd = (jnp.asarray(tile_ids, I32), jnp.asarray(seq_of_tile, I32), jnp.asarray(first_fwd, I32))
    rev = tile_ids[::-1]
    order_bwd = (jnp.asarray(rev, I32), jnp.asarray(seq_of_tile[rev], I32), jnp.asarray(last_fwd[rev], I32))
    n_seq = B + BD

    cvecs = jnp.zeros((16, D), F32).at[0].set(c_ctx).at[1:1 + BD].set(c)
    mods = _modulation(cvecs, w_mod, b_mod).reshape(L, 16, 6, D)

    o = np.cumsum([0, Q_LORA, KV_LORA, MLA_ROPE, 512, 128, 128, 128, 128, 256, 16, 16, 256])
    seg = lambda j: w_in[:, :, o[j]:o[j + 1]]
    w_in_r = jnp.concatenate(
        [seg(0), seg(1), seg(3), seg(4), seg(5), seg(6), seg(7), seg(8), seg(11), seg(2), seg(2), seg(2), seg(2),
         seg(9), seg(10), jnp.zeros((L, D, 128 - 2 * GLA_GATE_RANK), F32)], axis=-1).astype(BF16)
    uq = w_uq.reshape(L, Q_LORA, MLA_HEADS, MLA_QK)
    ukv = w_ukv.reshape(L, KV_LORA, MLA_HEADS, MLA_NOPE + MLA_V)
    w_comb = _fold_q_weights(uq[..., :MLA_NOPE].transpose(0, 2, 1, 3), ukv[..., :MLA_NOPE].transpose(0, 2, 1, 3))
    w_q2 = (jnp.concatenate([w_comb.transpose(0, 2, 1, 3).reshape(L, Q_LORA, MLA_HEADS * KV_LORA),
                             uq[..., MLA_NOPE:].reshape(L, Q_LORA, MLA_HEADS * MLA_ROPE)], axis=-1)
            * (MLA_QK ** -0.5)).astype(BF16)
    w_uv = ukv[..., MLA_NOPE:].transpose(0, 2, 1, 3)
    eye_h = jnp.eye(MLA_HEADS, dtype=F32)
    wuv_pad = jnp.einsum('lhkv,hg->lhkgv', w_uv, eye_h).reshape(L, MLA_HEADS, KV_LORA, MLA_HEADS * MLA_V).astype(BF16)
    w_gate = jnp.zeros((L, 128, 256), F32).at[:, 0:16, 0:128].set(w_gk_fwd).at[:, 16:32, 128:256].set(w_gk_bwd)
    w_gate = w_gate.astype(BF16)
    b_gate = jnp.concatenate([b_gk_fwd, b_gk_bwd], axis=-1)
    w_out_bf = w_out.astype(BF16)
    wr_pad = jnp.pad(w_router, ((0, 0), (0, 0), (0, LANES - N_EXPERTS)))
    wr_hi = wr_pad.astype(BF16)
    wr_lo = (wr_pad - wr_hi.astype(F32)).astype(BF16)
    br_pad = jnp.pad(b_router, ((0, 0), (0, LANES - N_EXPERTS)), constant_values=NEG_BIG)
    ones512 = _block_ones(512, HEAD_DIM)
    head_expand = jnp.asarray((np.arange(128)[:, None] // GLA_DK == np.arange(256)[None, :] // GLA_DV)
                              .astype(np.float32), BF16)
    tab = _rope_tables(TD)

    x = jnp.concatenate([x_prompt.reshape(n_ctx, D), x_sample.reshape(BD * TD, D)], axis=0)
    y = None
    n_blocks = (N * TOP_K) // MOE_ROWS + N_EXPERTS
    caches, states = [], []
    for l in range(L):
        lw = dict(g_attn=g_attn_norm[l][None], w_in_r=w_in_r[l], g_q_a=g_q_a[l][None], w_q2=w_q2[l],
                  g_kv_a=g_kv_a[l][None], g_qh=jnp.tile(g_q_head[l], GQA_HEADS)[None],
                  g_kh=jnp.tile(g_k_head[l], GQA_KV_HEADS)[None], ones512=ones512, w_gate=w_gate[l],
                  b_gate=b_gate[l][None], head_expand=head_expand, g_gla_out=jnp.tile(g_gla_out[l], GLA_HEADS)[None],
                  ones256=ones512[:256, :256], w_out=w_out_bf[l], g_ffn=g_ffn_norm[l][None],
                  w_router_hi=wr_hi[l], w_router_lo=wr_lo[l], b_router=br_pad[l][None])
        pmod = mods[l - 1] if l > 0 else None
        x, qm, kvc, qg, kdup, vdup, gin, cache = _pre_attn(x, y, pmod, mods[l], tab, modidx, tabidx, lw, n_ctx_tiles)
        caches.append(cache[:n_ctx])

        mla_ctx = jnp.concatenate([cache_mla_ckv[:, l]] + [cache_mla_krope[:, l]] * 4, axis=-1).astype(BF16)
        ck = cache_gqa_k[:, l]
        cv = cache_gqa_v[:, l]
        k_ctx = jnp.concatenate([ck[:, :, 0], ck[:, :, 0], ck[:, :, 1], ck[:, :, 1]], axis=-1).astype(BF16)
        v_ctx = jnp.concatenate([cv[:, :, 0], cv[:, :, 0], cv[:, :, 1], cv[:, :, 1]], axis=-1).astype(BF16)
        o_mla = _attention('mla', qm, kvc, None, None, wuv_pad[l], None, n_seq=B, seq_len=T, tile_offset=0)
        o_mla = _attention('mla', qm, kvc, None, (mla_ctx,), wuv_pad[l], o_mla, n_seq=BD, seq_len=TD,
                           tile_offset=n_ctx // TD)
        o_gqa = _attention('gqa', qg, kdup, vdup, None, None, None, n_seq=B, seq_len=T, tile_offset=0)
        o_gqa = _attention('gqa', qg, kdup, vdup, (k_ctx, v_ctx), None, o_gqa, n_seq=BD, seq_len=TD,
                           tile_offset=n_ctx // TD)

        eye_g = jnp.eye(GLA_HEADS, dtype=F32)
        st_lat = jnp.einsum('bshde,hg->bshegd', state_gla[:, l], eye_g).reshape(BD, 2, 256, 128)
        zeros_ctx = jnp.zeros((B, 256, 128), F32)
        o_f, s_f = _gla(gin, jnp.concatenate([zeros_ctx, st_lat[:, 0]], axis=0), None, lw, order_fwd, reverse=False)
        o_gla, s_b = _gla(gin, jnp.concatenate([zeros_ctx, st_lat[:, 1]], axis=0), o_f, lw, order_bwd, reverse=True)
        states.append((s_f[:B], s_b[:B]))

        x1, h2, ti, tg = _post_attn(x, o_mla, o_gqa, o_gla, mods[l], modidx, lw)
        top_idx, gates = ti[:, :TOP_K], tg[:, :TOP_K]
        dest, slot_tok, sched = _route(top_idx, n_blocks)
        xs = h2[slot_tok]
        yb = _moe_experts(xs, sched, w_gate_up, b_gate_up, w_down, b_down, l)
        y = jnp.sum(yb[dest] * gates[:, :, None], axis=1)
        x = x1

    out = _final_norm(x, y, mods[L - 1], modidx, g_final[None])
    y_prompt = out[:n_ctx].reshape(B, T, D)
    y_sample = out[n_ctx:].reshape(BD, TD, D)
    cache_all = jnp.stack([cc.reshape(B, T, 512) for cc in caches], axis=1)
    new_ckv = cache_all[..., 0:128]
    new_k = cache_all[..., 128:256].reshape(B, L, T, GQA_KV_HEADS, HEAD_DIM)
    new_v = cache_all[..., 256:384].reshape(B, L, T, GQA_KV_HEADS, HEAD_DIM)
    new_krope = cache_all[..., 384:384 + MLA_ROPE]

    def unpack_state(st):
        s5 = st.reshape(B, GLA_HEADS, GLA_DV, GLA_HEADS, GLA_DK)
        diag = jnp.stack([s5[:, h, :, h, :] for h in range(GLA_HEADS)], axis=1)
        return diag.transpose(0, 1, 3, 2)

    new_state = jnp.stack([jnp.stack([unpack_state(sf), unpack_state(sb)], axis=1) for sf, sb in states], axis=1)
    return (y_prompt, y_sample, new_ckv, new_krope, new_k, new_v, new_state)
```

```python
import functools

import jax
import jax.numpy as jnp
import numpy as np
from jax import lax
from jax.experimental import pallas as pl
from jax.experimental.pallas import tpu as pltpu

F32 = jnp.float32
BF16 = jnp.bfloat16
I32 = jnp.int32

D_MODEL = 1024
GRID_W = 64
ROPE_THETA = 10000.0
EPS = 1e-6
MLA_HEADS = 4
Q_LORA = 256
KV_LORA = 128
MLA_NOPE = 64
MLA_ROPE = 32
MLA_V = 64
MLA_QK = MLA_NOPE + MLA_ROPE
GQA_HEADS = 8
GQA_KV_HEADS = 2
GQA_GROUP = GQA_HEADS // GQA_KV_HEADS
HEAD_DIM = 64
GLA_HEADS = 4
GLA_DK = 32
GLA_DV = 64
GLA_GATE_RANK = 16
GLA_GATE_NORM = 16.0
N_EXPERTS = 32
TOP_K = 4
D_EXPERT = D_MODEL
SWIGLU_LIMIT = 7.0
SWIGLU_ALPHA = 1.702

LANES = 128
TOKEN_TILE = 256
GLA_STEP = 16
MOE_ROWS = 512
ATTN_TQ = 512
ATTN_TK = 512
VMEM_LIMIT = 56 * 1024 * 1024
NEG_BIG = -1e30
LOG2E = 1.4426950408889634

C_CQ, C_CKV, C_GQ, C_GK, C_GV = 0, 256, 384, 896, 1024
C_LQ, C_LK, C_LV, C_LG, C_KR, C_GATE = 1152, 1280, 1408, 1664, 1920, 2048
IN_COLS = 2176


def _dot(a, b):
    return jnp.dot(a, b, preferred_element_type=F32)


def _dot_nt(a, b):
    return lax.dot_general(a, b, (((1,), (1,)), ((), ())), preferred_element_type=F32)


def _dot_tn(a, b):
    return lax.dot_general(a, b, (((0,), (0,)), ((), ())), preferred_element_type=F32)


def _split_dot(x, ones_bf):
    hi = x.astype(BF16)
    lo = (x - hi.astype(F32)).astype(BF16)
    return _dot(hi, ones_bf) + _dot(lo, ones_bf)


def _rope(x, cos, sin_signed, quarter):
    width = x.shape[-1]
    lane = lax.broadcasted_iota(I32, x.shape, 1)
    first = (lane % (2 * quarter)) < quarter
    partner = jnp.where(first, pltpu.roll(x, width - quarter, 1), pltpu.roll(x, quarter, 1))
    return x * cos + partner * sin_signed


def _mod_kernel(c_ref, w_ref, b_ref, o_ref):
    c = c_ref[...]
    a = c * jax.nn.sigmoid(c)
    o_ref[0] = jnp.dot(a, w_ref[0], precision=lax.Precision.HIGHEST, preferred_element_type=F32) + b_ref[0]


def _modulation(cvecs, w_mod, b_mod):
    L, D, D6 = w_mod.shape
    tn = 1536
    return pl.pallas_call(
        _mod_kernel,
        grid=(L, D6 // tn),
        in_specs=[pl.BlockSpec((16, D), lambda l, j: (0, 0)),
                  pl.BlockSpec((1, D, tn), lambda l, j: (l, 0, j)),
                  pl.BlockSpec((1, 1, tn), lambda l, j: (l, 0, j))],
        out_specs=pl.BlockSpec((1, 16, tn), lambda l, j: (l, 0, j)),
        out_shape=jax.ShapeDtypeStruct((L, 16, D6), F32),
        compiler_params=pltpu.CompilerParams(dimension_semantics=("arbitrary", "arbitrary"),
                                             vmem_limit_bytes=VMEM_LIMIT),
        name="modulation",
    )(cvecs, w_mod, b_mod.reshape(L, 1, D6))


def _fold_kernel(a_ref, b_ref, o_ref):
    o_ref[0, 0] = lax.dot_general(a_ref[0, 0], b_ref[0, 0], (((1,), (1,)), ((), ())),
                                  precision=lax.Precision.HIGHEST, preferred_element_type=F32)


def _fold_q_weights(wq_nope, wk_nope):
    L, H, A, K = wq_nope.shape
    B = wk_nope.shape[2]
    return pl.pallas_call(
        _fold_kernel,
        grid=(L, H),
        in_specs=[pl.BlockSpec((1, 1, A, K), lambda l, h: (l, h, 0, 0)),
                  pl.BlockSpec((1, 1, B, K), lambda l, h: (l, h, 0, 0))],
        out_specs=pl.BlockSpec((1, 1, A, B), lambda l, h: (l, h, 0, 0)),
        out_shape=jax.ShapeDtypeStruct((L, H, A, B), F32),
        name="fold_q_weights",
    )(wq_nope, wk_nope)


def _pre_attn_kernel(modidx_ref, tabidx_ref, *refs, has_y):
    del modidx_ref, tabidx_ref
    if has_y:
        x_ref, y_ref, pmod_ref = refs[:3]
        refs = refs[3:]
    else:
        x_ref = refs[0]
        refs = refs[1:]
    (mod_ref, tab_ref, gattn_ref, win_ref, gqa_ref, wq2_ref, gkva_ref, gqh_ref, gkh_ref, ones_ref,
     wg_ref, bg_ref) = refs[:12]
    outs = refs[12:]
    if has_y:
        xres_ref = outs[0]
        outs = outs[1:]
    qm_ref, kvc_ref, qg_ref, kdup_ref, vdup_ref, gla_ref, cache_ref = outs

    x = x_ref[...]
    if has_y:
        x = x + pmod_ref[0, 5:6, :] * y_ref[...]
        xres_ref[...] = x
    mod = mod_ref[0]
    shift1, scale1 = mod[0:1, :], mod[1:2, :]
    xn = x * lax.rsqrt(jnp.mean(x * x, axis=-1, keepdims=True) + EPS) * gattn_ref[...]
    h = xn * (1.0 + scale1) + shift1
    p = _dot(h.astype(BF16), win_ref[...])

    tab = tab_ref[0]
    cos64, sin64 = tab[:, 0:128], tab[:, 128:256]
    cos32, sin32 = tab[:, 256:384], tab[:, 384:512]
    ones = ones_ref[...]

    cq = p[:, C_CQ:C_CQ + Q_LORA]
    cqn = cq * lax.rsqrt(jnp.mean(cq * cq, axis=-1, keepdims=True) + EPS) * gqa_ref[...]
    qm = _dot(cqn.astype(BF16), wq2_ref[...])
    q_rope = _rope(jnp.concatenate([qm[:, h * 256 + 128:(h + 1) * 256] for h in range(MLA_HEADS)], axis=1),
                   jnp.concatenate([cos32] * MLA_HEADS, axis=1), jnp.concatenate([sin32] * MLA_HEADS, axis=1),
                   MLA_ROPE // 4)
    for h in range(MLA_HEADS):
        qm_ref[:, h * 256:h * 256 + 128] = qm[:, h * 256:h * 256 + 128].astype(BF16)
        qm_ref[:, h * 256 + 128:(h + 1) * 256] = q_rope[:, h * 128:(h + 1) * 128].astype(BF16)
    ckv_raw = p[:, C_CKV:C_CKV + KV_LORA]
    ckv = ckv_raw * lax.rsqrt(jnp.mean(ckv_raw * ckv_raw, axis=-1, keepdims=True) + EPS) * gkva_ref[...]
    kr4 = p[:, C_KR:C_KR + 128]
    kr4_rot = _rope(kr4, cos32, sin32, MLA_ROPE // 4)
    kvc_ref[:, 0:128] = ckv.astype(BF16)
    kvc_ref[:, 128:256] = kr4_rot.astype(BF16)

    gq = p[:, C_GQ:C_GQ + 512]
    ms_q = _split_dot(gq * gq, ones) * (1.0 / HEAD_DIM)
    qn = gq * lax.rsqrt(ms_q + EPS) * gqh_ref[...]
    cos_q = jnp.concatenate([cos64] * 4, axis=1)
    sin_q = jnp.concatenate([sin64] * 4, axis=1)
    qg = _rope(qn, cos_q, sin_q, HEAD_DIM // 4) * (HEAD_DIM ** -0.5 * LOG2E)
    qg_ref[...] = qg.astype(BF16)
    gk = p[:, C_GK:C_GK + 128]
    ms_k = _split_dot(gk * gk, ones[0:128, 0:128]) * (1.0 / HEAD_DIM)
    kn = gk * lax.rsqrt(ms_k + EPS) * gkh_ref[...]
    kg = _rope(kn, cos64, sin64, HEAD_DIM // 4)
    gv = p[:, C_GV:C_GV + 128]
    lane = lax.broadcasted_iota(I32, kg.shape, 1)
    low = lane < HEAD_DIM
    kg_sw = pltpu.roll(kg, HEAD_DIM, 1)
    gv_sw = pltpu.roll(gv, HEAD_DIM, 1)
    kdup_ref[:, 0:128] = jnp.where(low, kg, kg_sw).astype(BF16)
    kdup_ref[:, 128:256] = jnp.where(low, kg_sw, kg).astype(BF16)
    vdup_ref[:, 0:128] = jnp.where(low, gv, 1.0).astype(BF16)
    vdup_ref[:, 128:256] = jnp.where(low, gv_sw, 1.0).astype(BF16)

    z = _dot(p[:, C_GATE:C_GATE + 128].astype(BF16), wg_ref[...]) + bg_ref[...]
    logsig = jnp.minimum(z, 0.0) - jnp.log1p(jnp.exp(-jnp.abs(z)))
    gla_ref[:, 0:128] = p[:, C_LQ:C_LQ + 128] * (GLA_DK ** -0.5)
    gla_ref[:, 128:256] = p[:, C_LK:C_LK + 128]
    gla_ref[:, 256:512] = logsig * (1.0 / GLA_GATE_NORM)
    gla_ref[:, 512:768] = p[:, C_LV:C_LV + 256]
    gla_ref[:, 768:1024] = p[:, C_LG:C_LG + 256]

    cache_ref[:, 0:128] = ckv
    cache_ref[:, 128:256] = kn
    cache_ref[:, 256:384] = gv
    cache_ref[:, 384:512] = kr4


def _pre_attn(x, y, pmod, mod, tab, modidx, tabidx, lw, n_ctx_tiles):
    N, D = x.shape
    tm = TOKEN_TILE
    nt = N // tm
    has_y = y is not None
    row = lambda i, mi, ti: (i, 0)
    const2 = lambda i, mi, ti: (0, 0)
    in_specs = [pl.BlockSpec((tm, D), row)]
    args = [x]
    if has_y:
        in_specs += [pl.BlockSpec((tm, D), row), pl.BlockSpec((1, 6, D), lambda i, mi, ti: (mi[i], 0, 0))]
        args += [y, pmod]
    in_specs += [
        pl.BlockSpec((1, 6, D), lambda i, mi, ti: (mi[i], 0, 0)),
        pl.BlockSpec((1, tm, 512), lambda i, mi, ti: (ti[i], 0, 0)),
        pl.BlockSpec((1, D), const2),
        pl.BlockSpec((D, IN_COLS), const2),
        pl.BlockSpec((1, Q_LORA), const2),
        pl.BlockSpec((Q_LORA, 1024), const2),
        pl.BlockSpec((1, KV_LORA), const2),
        pl.BlockSpec((1, 512), const2),
        pl.BlockSpec((1, 128), const2),
        pl.BlockSpec((512, 512), const2),
        pl.BlockSpec((128, 256), const2),
        pl.BlockSpec((1, 256), const2),
    ]
    args += [mod, tab, lw['g_attn'], lw['w_in_r'], lw['g_q_a'], lw['w_q2'], lw['g_kv_a'], lw['g_qh'], lw['g_kh'],
             lw['ones512'], lw['w_gate'], lw['b_gate']]
    out_shape, out_specs = [], []
    if has_y:
        out_shape.append(jax.ShapeDtypeStruct((N, D), F32))
        out_specs.append(pl.BlockSpec((tm, D), row))
    for width, dt in ((1024, BF16), (256, BF16), (512, BF16), (256, BF16), (256, BF16), (1024, F32)):
        out_shape.append(jax.ShapeDtypeStruct((N, width), dt))
        out_specs.append(pl.BlockSpec((tm, width), row))
    out_shape.append(jax.ShapeDtypeStruct(((n_ctx_tiles + 1) * tm, 512), F32))
    out_specs.append(pl.BlockSpec((tm, 512), lambda i, mi, ti: (jnp.minimum(i, n_ctx_tiles), 0)))
    outs = pl.pallas_call(
        functools.partial(_pre_attn_kernel, has_y=has_y),
        grid_spec=pltpu.PrefetchScalarGridSpec(num_scalar_prefetch=2, grid=(nt,), in_specs=in_specs,
                                               out_specs=out_specs),
        out_shape=out_shape,
        compiler_params=pltpu.CompilerParams(dimension_semantics=("arbitrary",), vmem_limit_bytes=VMEM_LIMIT),
        name="pre_attn",
    )(modidx, tabidx, *args)
    if not has_y:
        outs = [x] + list(outs)
    return outs


def _softmax_step(qs, k, m_ref, c):
    s = _dot_nt(qs, k)
    m_old = m_ref[c]
    m_new = jnp.maximum(m_old, jnp.broadcast_to(jnp.max(s, axis=-1, keepdims=True), m_old.shape))
    alpha = jnp.exp2(m_old - m_new)
    p = jnp.exp2(s - jnp.concatenate([m_new] * (k.shape[0] // LANES), axis=1))
    m_ref[c] = m_new
    return alpha, p


def _mla_attn_kernel(*refs, has_prev, has_ctx, n_chunks, tk):
    if has_prev:
        refs = refs[1:]
    if has_ctx:
        q_ref, kc_ref, k_ref, wuv_ref, o_ref, m_ref, l_ref, acc_ref = refs
    else:
        q_ref, k_ref, wuv_ref, o_ref, m_ref, l_ref, acc_ref = refs
        kc_ref = None
    tq = q_ref.shape[0]
    m_ref[...] = jnp.full(m_ref.shape, NEG_BIG, F32)
    l_ref[...] = jnp.zeros(l_ref.shape, F32)
    acc_ref[...] = jnp.zeros(acc_ref.shape, F32)
    q = q_ref[...]
    qs = [jnp.concatenate([q[:, h * 256:(h + 1) * 256] for h in (2 * c, 2 * c + 1)], axis=0) for c in range(2)]

    def process(kblk):
        for c in range(2):
            alpha, p = _softmax_step(qs[c], kblk, m_ref, c)
            l_ref[c] = alpha * l_ref[c] + jnp.broadcast_to(jnp.sum(p, axis=-1, keepdims=True), alpha.shape)
            acc_ref[c] = alpha * acc_ref[c] + _dot(p.astype(BF16), kblk[:, 0:KV_LORA])

    if has_ctx:
        process(kc_ref[0])

    def body(j, carry):
        process(k_ref[pl.ds(pl.multiple_of(j * tk, tk), tk), :])
        return carry

    lax.fori_loop(0, n_chunks, body, 0)
    out = jnp.zeros(o_ref.shape, F32)
    for h in range(MLA_HEADS):
        rows = slice((h % 2) * tq, (h % 2 + 1) * tq)
        o_lat = acc_ref[h // 2, rows, :] / l_ref[h // 2, rows, :]
        out = out + _dot(o_lat.astype(BF16), wuv_ref[h])
    o_ref[...] = out.astype(o_ref.dtype)


def _gqa_attn_kernel(*refs, has_prev, has_ctx, n_chunks, tk):
    if has_prev:
        refs = refs[1:]
    if has_ctx:
        q_ref, kc_ref, vc_ref, k_ref, v_ref, o_ref, m_ref, acc_ref = refs
    else:
        q_ref, k_ref, v_ref, o_ref, m_ref, acc_ref = refs
        kc_ref = vc_ref = None
    tq = q_ref.shape[0]
    m_ref[...] = jnp.full(m_ref.shape, NEG_BIG, F32)
    acc_ref[...] = jnp.zeros(acc_ref.shape, F32)
    q = q_ref[...]
    lane = lax.broadcasted_iota(I32, (tq, LANES), 1)
    low = lane < HEAD_DIM
    qs = []
    for g in range(GQA_KV_HEADS):
        parts = []
        for h in range(g * GQA_GROUP, (g + 1) * GQA_GROUP):
            pair = q[:, (h // 2) * LANES:(h // 2 + 1) * LANES]
            keep = low if h % 2 == 0 else jnp.logical_not(low)
            parts.append(jnp.where(keep, pair, jnp.zeros_like(pair)))
        qs.append(jnp.concatenate(parts, axis=0))

    def process(kblk, vblk):
        for g in range(GQA_KV_HEADS):
            alpha, p = _softmax_step(qs[g], kblk[:, g * LANES:(g + 1) * LANES], m_ref, g)
            acc_ref[g] = alpha * acc_ref[g] + _dot(p.astype(BF16), vblk[:, g * LANES:(g + 1) * LANES])

    if has_ctx:
        process(kc_ref[0], vc_ref[0])

    def body(j, carry):
        sl = pl.ds(pl.multiple_of(j * tk, tk), tk)
        process(k_ref[sl, :], v_ref[sl, :])
        return carry

    lax.fori_loop(0, n_chunks, body, 0)
    for j in range(GQA_HEADS // 2):
        outs = []
        for h in (2 * j, 2 * j + 1):
            a = acc_ref[h // GQA_GROUP, (h % GQA_GROUP) * tq:(h % GQA_GROUP + 1) * tq, :]
            outs.append(a / jnp.where(low, pltpu.roll(a, HEAD_DIM, 1), a))
        o_ref[:, j * LANES:(j + 1) * LANES] = jnp.where(low, outs[0], pltpu.roll(outs[1], HEAD_DIM, 1)).astype(
            o_ref.dtype)


def _attention(kind, q, k, v, ctx, wuv, prev, *, n_seq, seq_len, tile_offset):
    has_ctx = ctx is not None
    tq = min(ATTN_TQ, seq_len)
    tk = min(ATTN_TK, seq_len)
    nq = seq_len // tq
    qw = q.shape[1]
    ow = 256 if kind == 'mla' else 512
    heads = MLA_HEADS if kind == 'mla' else GQA_HEADS
    qmap = lambda b, i: (tile_offset * nq + b * nq + i, 0)
    kmap = lambda b, i: (tile_offset + b, 0)
    cmap = lambda b, i: (b, 0, 0)
    in_specs = [pl.BlockSpec((tq, qw), qmap)]
    args = [q]
    aliases = {}
    if prev is not None:
        in_specs = [pl.BlockSpec(memory_space=pl.ANY)] + in_specs
        args = [prev] + args
        aliases = {0: 0}
    if kind == 'mla':
        if has_ctx:
            in_specs.append(pl.BlockSpec((1,) + ctx[0].shape[1:], cmap))
            args.append(ctx[0])
        in_specs += [pl.BlockSpec((seq_len, 256), kmap), pl.BlockSpec((MLA_HEADS, 128, 256), lambda b, i: (0, 0, 0))]
        args += [k, wuv]
        body = _mla_attn_kernel
    else:
        if has_ctx:
            in_specs += [pl.BlockSpec((1,) + ctx[0].shape[1:], cmap), pl.BlockSpec((1,) + ctx[1].shape[1:], cmap)]
            args += [ctx[0], ctx[1]]
        in_specs += [pl.BlockSpec((seq_len, 256), kmap), pl.BlockSpec((seq_len, 256), kmap)]
        args += [k, v]
        body = _gqa_attn_kernel
    return pl.pallas_call(
        functools.partial(body, has_prev=prev is not None, has_ctx=has_ctx, n_chunks=seq_len // tk, tk=tk),
        grid=(n_seq, nq),
        in_specs=in_specs,
        out_specs=pl.BlockSpec((tq, ow), qmap),
        out_shape=jax.ShapeDtypeStruct((q.shape[0], ow), BF16),
        input_output_aliases=aliases,
        scratch_shapes=[pltpu.VMEM((2, heads // 2 * tq, LANES), F32)] * (3 if kind == 'mla' else 2),
        compiler_params=pltpu.CompilerParams(dimension_semantics=("arbitrary", "arbitrary"),
                                             vmem_limit_bytes=VMEM_LIMIT),
        name=kind + ("_attn_latent" if has_ctx else "_attn_context"),
    )(*args)


def _gla_kernel(blk_ref, seq_ref, first_ref, *refs, reverse):
    del blk_ref, seq_ref
    if reverse:
        gin_ref, s0_ref, he_ref, of_ref, gout_ref, ones_ref, o_ref, sout_ref, st_ref = refs
    else:
        gin_ref, s0_ref, he_ref, o_ref, sout_ref, st_ref = refs
    i = pl.program_id(0)

    @pl.when(first_ref[i] == 1)
    def _():
        st_ref[...] = s0_ref[0]

    R, S = TOKEN_TILE, GLA_STEP
    ns = R // S
    q = gin_ref[:, 0:128]
    k = gin_ref[:, 128:256]
    g = gin_ref[:, 384:512] if reverse else gin_ref[:, 256:384]
    v = gin_ref[:, 512:768]
    pos = lax.broadcasted_iota(I32, (R, 128), 0) % S
    b = g
    for s in (1, 2, 4, 8):
        if reverse:
            b = b + jnp.where(pos < S - s, pltpu.roll(b, R - s, 0), 0.0)
        else:
            b = b + jnp.where(pos >= s, pltpu.roll(b, s, 0), 0.0)

    def step_row(a, j):
        w = a.shape[-1]
        a3 = a.reshape(ns, S, w)
        return jnp.broadcast_to(a3[:, j:j + 1, :], (ns, S, w)).reshape(R, w)

    b_edge = step_row(b, 0 if reverse else S - 1)
    qt = (q * jnp.exp(b)).astype(BF16)
    kt = (k * jnp.exp(b_edge - b)).astype(BF16)
    d_edge = jnp.exp(b_edge)
    vb = v.astype(BF16)
    he = he_ref[...]

    o_intra = jnp.zeros((R, 256), F32)
    for j in range(S):
        cond = (pos <= j) if reverse else (pos >= j)
        bj, kj, vj = step_row(b, j), step_row(k, j), step_row(v, j)
        t = jnp.where(cond, q * kj * jnp.exp(jnp.where(cond, b - bj, 0.0)), 0.0)
        o_intra = o_intra + _dot(t.astype(BF16), he) * vj

    own_head = (lax.broadcasted_iota(I32, (256, 128), 0) // GLA_DV) == (lax.broadcasted_iota(I32, (256, 128), 1) // GLA_DK)
    st = st_ref[...]
    parts = [None] * ns
    for t in (range(ns - 1, -1, -1) if reverse else range(ns)):
        rows = slice(t * S, (t + 1) * S)
        parts[t] = _dot_nt(qt[rows], st.astype(BF16)) + o_intra[rows]
        ut = _dot_tn(vb[rows], kt[rows])
        st = d_edge[t * S:t * S + 1, :] * st + jnp.where(own_head, ut, 0.0)
    st_ref[...] = st
    sout_ref[0] = st
    o = jnp.concatenate(parts, axis=0)
    if reverse:
        o = o + of_ref[...]
        ms = _split_dot(o * o, ones_ref[...]) * (1.0 / GLA_DV)
        on = o * lax.rsqrt(ms + EPS) * gout_ref[...]
        lg = gin_ref[:, 768:1024]
        o_ref[...] = (on * (lg * jax.nn.sigmoid(lg))).astype(o_ref.dtype)
    else:
        o_ref[...] = o


def _gla(gin, s0, o_f, lw, order, *, reverse):
    blk, seq, first = order
    N = gin.shape[0]
    tm = TOKEN_TILE
    n_seq = s0.shape[0]
    tile = lambda i, b, s, f: (b[i], 0)
    state = lambda i, b, s, f: (s[i], 0, 0)
    const2 = lambda i, b, s, f: (0, 0)
    in_specs = [pl.BlockSpec((tm, 1024), tile), pl.BlockSpec((1, 256, 128), state), pl.BlockSpec((128, 256), const2)]
    args = [gin, s0, lw['head_expand']]
    if reverse:
        in_specs += [pl.BlockSpec((tm, 256), tile), pl.BlockSpec((1, 256), const2), pl.BlockSpec((256, 256), const2)]
        args += [o_f, lw['g_gla_out'], lw['ones256']]
    return pl.pallas_call(
        functools.partial(_gla_kernel, reverse=reverse),
        grid_spec=pltpu.PrefetchScalarGridSpec(
            num_scalar_prefetch=3, grid=(N // tm,), in_specs=in_specs,
            out_specs=[pl.BlockSpec((tm, 256), tile), pl.BlockSpec((1, 256, 128), state)],
            scratch_shapes=[pltpu.VMEM((256, 128), F32)]),
        out_shape=[jax.ShapeDtypeStruct((N, 256), BF16 if reverse else F32),
                   jax.ShapeDtypeStruct((n_seq, 256, 128), F32)],
        compiler_params=pltpu.CompilerParams(dimension_semantics=("arbitrary",), vmem_limit_bytes=VMEM_LIMIT),
        name="gla_bwd" if reverse else "gla_fwd",
    )(blk, seq, first, *args)


def _post_attn_kernel(modidx_ref, x_ref, om_ref, og_ref, ol_ref, mod_ref, wout_ref, gffn_ref, wrh_ref, wrl_ref,
                      br_ref, x1_ref, h2_ref, ti_ref, tg_ref):
    del modidx_ref
    mod = mod_ref[0]
    gate1, shift2, scale2 = mod[2:3, :], mod[3:4, :], mod[4:5, :]
    o = (_dot(om_ref[...], wout_ref[0:256, :]) + _dot(og_ref[...], wout_ref[256:768, :])
         + _dot(ol_ref[...], wout_ref[768:1024, :]))
    x1 = x_ref[...] + gate1 * o
    x1_ref[...] = x1
    xn = x1 * lax.rsqrt(jnp.mean(x1 * x1, axis=-1, keepdims=True) + EPS) * gffn_ref[...]
    h2 = xn * (1.0 + scale2) + shift2
    hi = h2.astype(BF16)
    h2_ref[...] = hi
    lo = (h2 - hi.astype(F32)).astype(BF16)
    wrh = wrh_ref[...]
    logits = _dot(hi, wrh) + _dot(hi, wrl_ref[...]) + _dot(lo, wrh) + br_ref[...]

    lane = lax.broadcasted_iota(I32, logits.shape, 1).astype(F32)
    work = logits
    vals, idxs = [], []
    for _ in range(TOP_K):
        m = jnp.max(work, axis=-1, keepdims=True)
        idx = jnp.min(jnp.where(work == m, lane, float(LANES)), axis=-1, keepdims=True)
        vals.append(m)
        idxs.append(idx)
        work = jnp.where(lane == idx, NEG_BIG * 4.0, work)
    es = [jnp.exp(vv - vals[0]) for vv in vals]
    denom = es[0] + es[1] + es[2] + es[3]
    ti = jnp.zeros(logits.shape, F32)
    tg = jnp.zeros(logits.shape, F32)
    for r in range(TOP_K):
        ti = jnp.where(lane == float(r), idxs[r], ti)
        tg = jnp.where(lane == float(r), es[r] / denom, tg)
    ti_ref[...] = ti.astype(I32)
    tg_ref[...] = tg


def _post_attn(x, o_mla, o_gqa, o_gla, mod, modidx, lw):
    N, D = x.shape
    tm = TOKEN_TILE
    row = lambda i, mi: (i, 0)
    const2 = lambda i, mi: (0, 0)
    return pl.pallas_call(
        _post_attn_kernel,
        grid_spec=pltpu.PrefetchScalarGridSpec(
            num_scalar_prefetch=1, grid=(N // tm,),
            in_specs=[pl.BlockSpec((tm, D), row), pl.BlockSpec((tm, 256), row), pl.BlockSpec((tm, 512), row),
                      pl.BlockSpec((tm, 256), row), pl.BlockSpec((1, 6, D), lambda i, mi: (mi[i], 0, 0)),
                      pl.BlockSpec((D, D), const2), pl.BlockSpec((1, D), const2),
                      pl.BlockSpec((D, LANES), const2), pl.BlockSpec((D, LANES), const2),
                      pl.BlockSpec((1, LANES), const2)],
            out_specs=[pl.BlockSpec((tm, D), row), pl.BlockSpec((tm, D), row), pl.BlockSpec((tm, LANES), row),
                       pl.BlockSpec((tm, LANES), row)]),
        out_shape=[jax.ShapeDtypeStruct((N, D), F32), jax.ShapeDtypeStruct((N, D), BF16),
                   jax.ShapeDtypeStruct((N, LANES), I32), jax.ShapeDtypeStruct((N, LANES), F32)],
        compiler_params=pltpu.CompilerParams(dimension_semantics=("arbitrary",), vmem_limit_bytes=VMEM_LIMIT),
        name="post_attn",
    )(modidx, x, o_mla, o_gqa, o_gla, mod, lw['w_out'], lw['g_ffn'], lw['w_router_hi'], lw['w_router_lo'],
      lw['b_router'])


def _moe_kernel(be_ref, bx_ref, valid_ref, first_ref, x_ref, wgu_ref, bgu_ref, wd_ref, bd_ref, o_ref,
                wgu_bf, wd_bf):
    del be_ref, bx_ref
    i = pl.program_id(0)

    @pl.when(first_ref[i] == 1)
    def _():
        wgu_bf[...] = wgu_ref[...].astype(BF16)
        wd_bf[...] = wd_ref[...].astype(BF16)

    @pl.when(valid_ref[i] == 1)
    def _():
        gu = _dot(x_ref[...], wgu_bf[...]) + bgu_ref[...]
        gate = jnp.minimum(gu[:, :D_EXPERT], SWIGLU_LIMIT)
        up = jnp.clip(gu[:, D_EXPERT:], -SWIGLU_LIMIT, SWIGLU_LIMIT)
        act = gate * jax.nn.sigmoid(SWIGLU_ALPHA * gate) * (up + 1.0)
        o_ref[...] = _dot(act.astype(BF16), wd_bf[...]) + bd_ref[...]

    @pl.when(valid_ref[i] == 0)
    def _():
        o_ref[...] = jnp.zeros(o_ref.shape, o_ref.dtype)


def _moe_experts(xs, sched, w_gate_up, b_gate_up, w_down, b_down, layer):
    block_e, block_x, valid, first = sched
    D = xs.shape[1]
    n_blocks = xs.shape[0] // MOE_ROWS
    L, E = w_gate_up.shape[:2]
    wmap = lambda i, be, bx, va, fi: (layer, be[i], 0, 0)
    xmap = lambda i, be, bx, va, fi: (bx[i], 0)
    return pl.pallas_call(
        _moe_kernel,
        grid_spec=pltpu.PrefetchScalarGridSpec(
            num_scalar_prefetch=4, grid=(n_blocks,),
            in_specs=[pl.BlockSpec((MOE_ROWS, D), xmap),
                      pl.BlockSpec((None, None, D, 2 * D_EXPERT), wmap),
                      pl.BlockSpec((None, None, 1, 2 * D_EXPERT), wmap),
                      pl.BlockSpec((None, None, D_EXPERT, D), wmap),
                      pl.BlockSpec((None, None, 1, D), wmap)],
            out_specs=pl.BlockSpec((MOE_ROWS, D), lambda i, be, bx, va, fi: (i, 0)),
            scratch_shapes=[pltpu.VMEM((D, 2 * D_EXPERT), BF16), pltpu.VMEM((D_EXPERT, D), BF16)]),
        out_shape=jax.ShapeDtypeStruct(xs.shape, F32),
        compiler_params=pltpu.CompilerParams(dimension_semantics=("arbitrary",), vmem_limit_bytes=VMEM_LIMIT),
        name="moe_experts",
    )(block_e, block_x, valid, first, xs, w_gate_up, b_gate_up.reshape(L, E, 1, 2 * D_EXPERT), w_down,
      b_down.reshape(L, E, 1, D))


def _route(top_idx, n_blocks):
    N = top_idx.shape[0]
    onehot = (top_idx[:, :, None] == jnp.arange(N_EXPERTS, dtype=I32)[None, None, :]).astype(I32).sum(axis=1)
    before = jnp.cumsum(onehot, axis=0) - onehot
    counts = onehot.sum(axis=0)
    padded = (counts + MOE_ROWS - 1) // MOE_ROWS * MOE_ROWS
    pad_end = jnp.cumsum(padded)
    pad_start = pad_end - padded
    dest = pad_start[top_idx] + jnp.take_along_axis(before, top_idx, axis=1)
    tok = jnp.broadcast_to(jnp.arange(N, dtype=I32)[:, None], dest.shape)
    slot_tok = jnp.zeros((n_blocks * MOE_ROWS,), I32).at[dest.reshape(-1)].set(tok.reshape(-1))
    blk_start = jnp.arange(n_blocks, dtype=I32) * MOE_ROWS
    valid = blk_start < pad_end[-1]
    block_e = jnp.minimum(jnp.searchsorted(pad_end, blk_start, side='right'), N_EXPERTS - 1).astype(I32)
    n_valid = pad_end[-1] // MOE_ROWS
    block_x = jnp.minimum(jnp.arange(n_blocks, dtype=I32), n_valid - 1).astype(I32)
    block_e = block_e[block_x]
    prev_e = jnp.concatenate([jnp.full((1,), -1, I32), block_e[:-1]])
    first = (valid & (block_e != prev_e)).astype(I32)
    return dest, slot_tok, (block_e, block_x, valid.astype(I32), first)


def _final_kernel(modidx_ref, x_ref, y_ref, pmod_ref, g_ref, o_ref):
    del modidx_ref
    x = x_ref[...] + pmod_ref[0, 5:6, :] * y_ref[...]
    o_ref[...] = x * lax.rsqrt(jnp.mean(x * x, axis=-1, keepdims=True) + EPS) * g_ref[...]


def _final_norm(x, y, pmod, modidx, g_final):
    N, D = x.shape
    tm = TOKEN_TILE
    row = lambda i, mi: (i, 0)
    return pl.pallas_call(
        _final_kernel,
        grid_spec=pltpu.PrefetchScalarGridSpec(
            num_scalar_prefetch=1, grid=(N // tm,),
            in_specs=[pl.BlockSpec((tm, D), row), pl.BlockSpec((tm, D), row),
                      pl.BlockSpec((1, 6, D), lambda i, mi: (mi[i], 0, 0)), pl.BlockSpec((1, D), lambda i, mi: (0, 0))],
            out_specs=pl.BlockSpec((tm, D), row)),
        out_shape=jax.ShapeDtypeStruct((N, D), F32),
        compiler_params=pltpu.CompilerParams(dimension_semantics=("arbitrary",), vmem_limit_bytes=VMEM_LIMIT),
        name="final_norm",
    )(modidx, x, y, pmod, g_final)


def _rope_tables(seq_len):
    pos = np.arange(seq_len)
    rowp = (pos // GRID_W).astype(np.float32)
    colp = (pos % GRID_W).astype(np.float32)

    def tables(rdim, copies):
        quarter = rdim // 4
        inv_freq = jnp.asarray(ROPE_THETA, F32) ** (-jnp.arange(quarter, dtype=F32) / quarter)
        ar = jnp.asarray(rowp)[:, None] * inv_freq[None, :]
        ac = jnp.asarray(colp)[:, None] * inv_freq[None, :]
        cos = jnp.concatenate([jnp.cos(ar), jnp.cos(ar), jnp.cos(ac), jnp.cos(ac)], axis=1)
        sin = jnp.concatenate([-jnp.sin(ar), jnp.sin(ar), -jnp.sin(ac), jnp.sin(ac)], axis=1)
        return jnp.tile(cos, (1, copies)), jnp.tile(sin, (1, copies))

    c64, s64 = tables(HEAD_DIM, 2)
    c32, s32 = tables(MLA_ROPE, 4)
    lat = jnp.concatenate([c64, s64, c32, s32], axis=1).reshape(seq_len // TOKEN_TILE, TOKEN_TILE, 512)
    ident = jnp.concatenate([jnp.ones((TOKEN_TILE, 128), F32), jnp.zeros((TOKEN_TILE, 128), F32)] * 2, axis=1)
    return jnp.concatenate([ident[None], lat], axis=0)


def _block_ones(n, blk):
    r = np.arange(n) // blk
    return jnp.asarray((r[:, None] == r[None, :]).astype(np.float32), BF16)


def kernel(x_prompt, x_sample, cache_mla_ckv, cache_mla_krope, cache_gqa_k, cache_gqa_v, state_gla, c, c_ctx, w_mod, b_mod, g_attn_norm, g_ffn_norm, w_in, g_q_a, w_uq, g_kv_a, w_ukv, g_q_head, g_k_head, w_gk_fwd, b_gk_fwd, w_gk_bwd, b_gk_bwd, g_gla_out, w_out, w_router, b_router, w_gate_up, b_gate_up, w_down, b_down, g_final):
    B, T, D = x_prompt.shape
    BD, TD, _ = x_sample.shape
    L = w_in.shape[0]
    P = cache_mla_ckv.shape[2]
    tm = TOKEN_TILE
    n_ctx = B * T
    N = n_ctx + BD * TD
    assert T == tm and TD % ATTN_TQ == 0 and n_ctx % TD == 0 and BD + 1 <= 16
    n_ctx_tiles = n_ctx // tm
    nt = N // tm
    tiles_per_lat = TD // tm

    tile_ids = np.arange(nt)
    lat_tile = np.maximum(tile_ids - n_ctx_tiles, 0)
    is_lat = tile_ids >= n_ctx_tiles
    modidx = jnp.asarray(np.where(is_lat, 1 + lat_tile // tiles_per_lat, 0), I32)
    tabidx = jnp.asarray(np.where(is_lat, 1 + lat_tile % tiles_per_lat, 0), I32)
    seq_of_tile = np.where(is_lat, B + lat_tile // tiles_per_lat, tile_ids)
    first_fwd = np.where(is_lat, lat_tile % tiles_per_lat == 0, True)
    last_fwd = np.where(is_lat, lat_tile % tiles_per_lat == tiles_per_lat - 1, True)
    order_fwd = (jnp.asarray(tile_ids, I32), jnp.asarray(seq_of_tile, I32), jnp.asarray(first_fwd, I32))
    rev = tile_ids[::-1]
    order_bwd = (jnp.asarray(rev, I32), jnp.asarray(seq_of_tile[rev], I32), jnp.asarray(last_fwd[rev], I32))
    n_seq = B + BD

    cvecs = jnp.zeros((16, D), F32).at[0].set(c_ctx).at[1:1 + BD].set(c)
    mods = _modulation(cvecs, w_mod, b_mod).reshape(L, 16, 6, D)

    o = np.cumsum([0, Q_LORA, KV_LORA, MLA_ROPE, 512, 128, 128, 128, 128, 256, 16, 16, 256])
    seg = lambda j: w_in[:, :, o[j]:o[j + 1]]
    w_in_r = jnp.concatenate(
        [seg(0), seg(1), seg(3), seg(4), seg(5), seg(6), seg(7), seg(8), seg(11), seg(2), seg(2), seg(2), seg(2),
         seg(9), seg(10), jnp.zeros((L, D, 128 - 2 * GLA_GATE_RANK), F32)], axis=-1).astype(BF16)
    uq = w_uq.reshape(L, Q_LORA, MLA_HEADS, MLA_QK)
    ukv = w_ukv.reshape(L, KV_LORA, MLA_HEADS, MLA_NOPE + MLA_V)
    w_comb = _fold_q_weights(uq[..., :MLA_NOPE].transpose(0, 2, 1, 3), ukv[..., :MLA_NOPE].transpose(0, 2, 1, 3))
    q_pad = jnp.zeros((L, Q_LORA, 256 - KV_LORA - MLA_ROPE), F32)
    w_q2 = (jnp.concatenate([part for h in range(MLA_HEADS)
                             for part in (w_comb[:, h], uq[:, :, h, MLA_NOPE:], q_pad)], axis=-1)
            * (MLA_QK ** -0.5 * LOG2E)).astype(BF16)
    w_uv = ukv[..., MLA_NOPE:].transpose(0, 2, 1, 3)
    eye_h = jnp.eye(MLA_HEADS, dtype=F32)
    wuv_pad = jnp.einsum('lhkv,hg->lhkgv', w_uv, eye_h).reshape(L, MLA_HEADS, KV_LORA, MLA_HEADS * MLA_V).astype(BF16)
    w_gate = jnp.zeros((L, 128, 256), F32).at[:, 0:16, 0:128].set(w_gk_fwd).at[:, 16:32, 128:256].set(w_gk_bwd)
    w_gate = w_gate.astype(BF16)
    b_gate = jnp.concatenate([b_gk_fwd, b_gk_bwd], axis=-1)
    w_out_bf = w_out.astype(BF16)
    wr_pad = jnp.pad(w_router, ((0, 0), (0, 0), (0, LANES - N_EXPERTS)))
    wr_hi = wr_pad.astype(BF16)
    wr_lo = (wr_pad - wr_hi.astype(F32)).astype(BF16)
    br_pad = jnp.pad(b_router, ((0, 0), (0, LANES - N_EXPERTS)), constant_values=NEG_BIG)
    ones512 = _block_ones(512, HEAD_DIM)
    head_expand = jnp.asarray((np.arange(128)[:, None] // GLA_DK == np.arange(256)[None, :] // GLA_DV)
                              .astype(np.float32), BF16)
    tab = _rope_tables(TD)

    x = jnp.concatenate([x_prompt.reshape(n_ctx, D), x_sample.reshape(BD * TD, D)], axis=0)
    y = None
    n_blocks = (N * TOP_K) // MOE_ROWS + N_EXPERTS
    caches, states = [], []
    for l in range(L):
        lw = dict(g_attn=g_attn_norm[l][None], w_in_r=w_in_r[l], g_q_a=g_q_a[l][None], w_q2=w_q2[l],
                  g_kv_a=g_kv_a[l][None], g_qh=jnp.tile(g_q_head[l], GQA_HEADS)[None],
                  g_kh=jnp.tile(g_k_head[l], GQA_KV_HEADS)[None], ones512=ones512, w_gate=w_gate[l],
                  b_gate=b_gate[l][None], head_expand=head_expand, g_gla_out=jnp.tile(g_gla_out[l], GLA_HEADS)[None],
                  ones256=ones512[:256, :256], w_out=w_out_bf[l], g_ffn=g_ffn_norm[l][None],
                  w_router_hi=wr_hi[l], w_router_lo=wr_lo[l], b_router=br_pad[l][None])
        pmod = mods[l - 1] if l > 0 else None
        x, qm, kvc, qg, kdup, vdup, gin, cache = _pre_attn(x, y, pmod, mods[l], tab, modidx, tabidx, lw, n_ctx_tiles)
        caches.append(cache[:n_ctx])

        mla_ctx = jnp.concatenate([cache_mla_ckv[:, l]] + [cache_mla_krope[:, l]] * 4, axis=-1).astype(BF16)
        ck = cache_gqa_k[:, l]
        cv = cache_gqa_v[:, l]
        k_ctx = jnp.concatenate([ck[:, :, 0], ck[:, :, 0], ck[:, :, 1], ck[:, :, 1]], axis=-1).astype(BF16)
        ones_v = jnp.ones(cv.shape[:2] + (HEAD_DIM,), F32)
        v_ctx = jnp.concatenate([cv[:, :, 0], ones_v, cv[:, :, 1], ones_v], axis=-1).astype(BF16)
        o_mla = _attention('mla', qm, kvc, None, None, wuv_pad[l], None, n_seq=B, seq_len=T, tile_offset=0)
        o_mla = _attention('mla', qm, kvc, None, (mla_ctx,), wuv_pad[l], o_mla, n_seq=BD, seq_len=TD,
                           tile_offset=n_ctx // TD)
        o_gqa = _attention('gqa', qg, kdup, vdup, None, None, None, n_seq=B, seq_len=T, tile_offset=0)
        o_gqa = _attention('gqa', qg, kdup, vdup, (k_ctx, v_ctx), None, o_gqa, n_seq=BD, seq_len=TD,
                           tile_offset=n_ctx // TD)

        eye_g = jnp.eye(GLA_HEADS, dtype=F32)
        st_lat = jnp.einsum('bshde,hg->bshegd', state_gla[:, l], eye_g).reshape(BD, 2, 256, 128)
        zeros_ctx = jnp.zeros((B, 256, 128), F32)
        o_f, s_f = _gla(gin, jnp.concatenate([zeros_ctx, st_lat[:, 0]], axis=0), None, lw, order_fwd, reverse=False)
        o_gla, s_b = _gla(gin, jnp.concatenate([zeros_ctx, st_lat[:, 1]], axis=0), o_f, lw, order_bwd, reverse=True)
        states.append((s_f[:B], s_b[:B]))

        x1, h2, ti, tg = _post_attn(x, o_mla, o_gqa, o_gla, mods[l], modidx, lw)
        top_idx, gates = ti[:, :TOP_K], tg[:, :TOP_K]
        dest, slot_tok, sched = _route(top_idx, n_blocks)
        xs = h2[slot_tok]
        yb = _moe_experts(xs, sched, w_gate_up, b_gate_up, w_down, b_down, l)
        y = jnp.sum(yb[dest] * gates[:, :, None], axis=1)
        x = x1

    out = _final_norm(x, y, mods[L - 1], modidx, g_final[None])
    y_prompt = out[:n_ctx].reshape(B, T, D)
    y_sample = out[n_ctx:].reshape(BD, TD, D)
    cache_all = jnp.stack([cc.reshape(B, T, 512) for cc in caches], axis=1)
    new_ckv = cache_all[..., 0:128]
    new_k = cache_all[..., 128:256].reshape(B, L, T, GQA_KV_HEADS, HEAD_DIM)
    new_v = cache_all[..., 256:384].reshape(B, L, T, GQA_KV_HEADS, HEAD_DIM)
    new_krope = cache_all[..., 384:384 + MLA_ROPE]

    def unpack_state(st):
        s5 = st.reshape(B, GLA_HEADS, GLA_DV, GLA_HEADS, GLA_DK)
        diag = jnp.stack([s5[:, h, :, h, :] for h in range(GLA_HEADS)], axis=1)
        return diag.transpose(0, 1, 3, 2)

    new_state = jnp.stack([jnp.stack([unpack_state(sf), unpack_state(sb)], axis=1) for sf, sb in states], axis=1)
    return (y_prompt, y_sample, new_ckv, new_krope, new_k, new_v, new_state)
```

```python
import functools

import jax
import jax.numpy as jnp
import numpy as np
from jax import lax
from jax.experimental import pallas as pl
from jax.experimental.pallas import tpu as pltpu

F32 = jnp.float32
BF16 = jnp.bfloat16
I32 = jnp.int32

D_MODEL = 1024
GRID_W = 64
ROPE_THETA = 10000.0
EPS = 1e-6
MLA_HEADS = 4
Q_LORA = 256
KV_LORA = 128
MLA_NOPE = 64
MLA_ROPE = 32
MLA_V = 64
MLA_QK = MLA_NOPE + MLA_ROPE
GQA_HEADS = 8
GQA_KV_HEADS = 2
GQA_GROUP = GQA_HEADS // GQA_KV_HEADS
HEAD_DIM = 64
GLA_HEADS = 4
GLA_DK = 32
GLA_DV = 64
GLA_GATE_RANK = 16
GLA_GATE_NORM = 16.0
N_EXPERTS = 32
TOP_K = 4
D_EXPERT = D_MODEL
SWIGLU_LIMIT = 7.0
SWIGLU_ALPHA = 1.702

LANES = 128
TOKEN_TILE = 256
GLA_STEP = 16
MOE_ROWS = 512
ATTN_TQ = 512
ATTN_TK = 512
VMEM_LIMIT = 56 * 1024 * 1024
NEG_BIG = -1e30
LOG2E = 1.4426950408889634

C_CQ, C_CKV, C_GQ, C_GK, C_GV = 0, 256, 384, 896, 1024
C_LQ, C_LK, C_LV, C_LG, C_KR, C_GATE = 1152, 1280, 1408, 1664, 1920, 2048
IN_COLS = 2176


def _dot(a, b):
    return jnp.dot(a, b, preferred_element_type=F32)


def _dot_nt(a, b):
    return lax.dot_general(a, b, (((1,), (1,)), ((), ())), preferred_element_type=F32)


def _dot_tn(a, b):
    return lax.dot_general(a, b, (((0,), (0,)), ((), ())), preferred_element_type=F32)


def _split_dot(x, ones_bf):
    hi = x.astype(BF16)
    lo = (x - hi.astype(F32)).astype(BF16)
    return _dot(hi, ones_bf) + _dot(lo, ones_bf)


def _rope(x, cos, sin_signed, quarter):
    width = x.shape[-1]
    lane = lax.broadcasted_iota(I32, x.shape, 1)
    first = (lane % (2 * quarter)) < quarter
    partner = jnp.where(first, pltpu.roll(x, width - quarter, 1), pltpu.roll(x, quarter, 1))
    return x * cos + partner * sin_signed


def _mod_kernel(c_ref, w_ref, b_ref, o_ref):
    c = c_ref[...]
    a = c * jax.nn.sigmoid(c)
    o_ref[0] = jnp.dot(a, w_ref[0], precision=lax.Precision.HIGHEST, preferred_element_type=F32) + b_ref[0]


def _modulation(cvecs, w_mod, b_mod):
    L, D, D6 = w_mod.shape
    tn = 1536
    return pl.pallas_call(
        _mod_kernel,
        grid=(L, D6 // tn),
        in_specs=[pl.BlockSpec((16, D), lambda l, j: (0, 0)),
                  pl.BlockSpec((1, D, tn), lambda l, j: (l, 0, j)),
                  pl.BlockSpec((1, 1, tn), lambda l, j: (l, 0, j))],
        out_specs=pl.BlockSpec((1, 16, tn), lambda l, j: (l, 0, j)),
        out_shape=jax.ShapeDtypeStruct((L, 16, D6), F32),
        compiler_params=pltpu.CompilerParams(dimension_semantics=("arbitrary", "arbitrary"),
                                             vmem_limit_bytes=VMEM_LIMIT),
        name="modulation",
    )(cvecs, w_mod, b_mod.reshape(L, 1, D6))


def _fold_kernel(a_ref, b_ref, o_ref):
    o_ref[0, 0] = lax.dot_general(a_ref[0, 0], b_ref[0, 0], (((1,), (1,)), ((), ())),
                                  precision=lax.Precision.HIGHEST, preferred_element_type=F32)


def _fold_q_weights(wq_nope, wk_nope):
    L, H, A, K = wq_nope.shape
    B = wk_nope.shape[2]
    return pl.pallas_call(
        _fold_kernel,
        grid=(L, H),
        in_specs=[pl.BlockSpec((1, 1, A, K), lambda l, h: (l, h, 0, 0)),
                  pl.BlockSpec((1, 1, B, K), lambda l, h: (l, h, 0, 0))],
        out_specs=pl.BlockSpec((1, 1, A, B), lambda l, h: (l, h, 0, 0)),
        out_shape=jax.ShapeDtypeStruct((L, H, A, B), F32),
        name="fold_q_weights",
    )(wq_nope, wk_nope)


def _combine_experts(y_ref, tg_ref):
    tg = tg_ref[...]
    acc = tg[:, 0:1] * y_ref[0].astype(F32)
    for k in range(1, TOP_K):
        acc = acc + tg[:, k:k + 1] * y_ref[k].astype(F32)
    return acc


def _pre_attn_kernel(modidx_ref, tabidx_ref, *refs, has_y):
    del modidx_ref, tabidx_ref
    if has_y:
        x_ref, y_ref, tg_ref, pmod_ref = refs[:4]
        refs = refs[4:]
    else:
        x_ref = refs[0]
        refs = refs[1:]
    (mod_ref, tab_ref, gattn_ref, win_ref, gqa_ref, wq2_ref, gkva_ref, gqh_ref, gkh_ref, ones_ref,
     wg_ref, bg_ref) = refs[:12]
    outs = refs[12:]
    if has_y:
        xres_ref = outs[0]
        outs = outs[1:]
    qm_ref, kvc_ref, qg_ref, kdup_ref, vdup_ref, gla_ref, cache_ref = outs

    x = x_ref[...]
    if has_y:
        x = x + pmod_ref[0, 5:6, :] * _combine_experts(y_ref, tg_ref)
        xres_ref[...] = x
    mod = mod_ref[0]
    shift1, scale1 = mod[0:1, :], mod[1:2, :]
    xn = x * lax.rsqrt(jnp.mean(x * x, axis=-1, keepdims=True) + EPS) * gattn_ref[...]
    h = xn * (1.0 + scale1) + shift1
    p = _dot(h.astype(BF16), win_ref[...])

    tab = tab_ref[0]
    cos64, sin64 = tab[:, 0:128], tab[:, 128:256]
    cos32, sin32 = tab[:, 256:384], tab[:, 384:512]
    ones = ones_ref[...]

    cq = p[:, C_CQ:C_CQ + Q_LORA]
    cqn = cq * lax.rsqrt(jnp.mean(cq * cq, axis=-1, keepdims=True) + EPS) * gqa_ref[...]
    qm = _dot(cqn.astype(BF16), wq2_ref[...])
    q_rope = _rope(jnp.concatenate([qm[:, h * 256 + 128:(h + 1) * 256] for h in range(MLA_HEADS)], axis=1),
                   jnp.concatenate([cos32] * MLA_HEADS, axis=1), jnp.concatenate([sin32] * MLA_HEADS, axis=1),
                   MLA_ROPE // 4)
    for h in range(MLA_HEADS):
        qm_ref[:, h * 256:h * 256 + 128] = qm[:, h * 256:h * 256 + 128].astype(BF16)
        qm_ref[:, h * 256 + 128:(h + 1) * 256] = q_rope[:, h * 128:(h + 1) * 128].astype(BF16)
    ckv_raw = p[:, C_CKV:C_CKV + KV_LORA]
    ckv = ckv_raw * lax.rsqrt(jnp.mean(ckv_raw * ckv_raw, axis=-1, keepdims=True) + EPS) * gkva_ref[...]
    kr4 = p[:, C_KR:C_KR + 128]
    kr4_rot = _rope(kr4, cos32, sin32, MLA_ROPE // 4)
    kvc_ref[:, 0:128] = ckv.astype(BF16)
    kvc_ref[:, 128:256] = kr4_rot.astype(BF16)

    gq = p[:, C_GQ:C_GQ + 512]
    ms_q = _split_dot(gq * gq, ones) * (1.0 / HEAD_DIM)
    qn = gq * lax.rsqrt(ms_q + EPS) * gqh_ref[...]
    cos_q = jnp.concatenate([cos64] * 4, axis=1)
    sin_q = jnp.concatenate([sin64] * 4, axis=1)
    qg = _rope(qn, cos_q, sin_q, HEAD_DIM // 4) * (HEAD_DIM ** -0.5 * LOG2E)
    qg_ref[...] = qg.astype(BF16)
    gk = p[:, C_GK:C_GK + 128]
    ms_k = _split_dot(gk * gk, ones[0:128, 0:128]) * (1.0 / HEAD_DIM)
    kn = gk * lax.rsqrt(ms_k + EPS) * gkh_ref[...]
    kg = _rope(kn, cos64, sin64, HEAD_DIM // 4)
    gv = p[:, C_GV:C_GV + 128]
    lane = lax.broadcasted_iota(I32, kg.shape, 1)
    low = lane < HEAD_DIM
    kg_sw = pltpu.roll(kg, HEAD_DIM, 1)
    gv_sw = pltpu.roll(gv, HEAD_DIM, 1)
    kdup_ref[:, 0:128] = jnp.where(low, kg, kg_sw).astype(BF16)
    kdup_ref[:, 128:256] = jnp.where(low, kg_sw, kg).astype(BF16)
    vdup_ref[:, 0:128] = jnp.where(low, gv, 1.0).astype(BF16)
    vdup_ref[:, 128:256] = jnp.where(low, gv_sw, 1.0).astype(BF16)

    z = _dot(p[:, C_GATE:C_GATE + 128].astype(BF16), wg_ref[...]) + bg_ref[...]
    logsig = jnp.minimum(z, 0.0) - jnp.log1p(jnp.exp(-jnp.abs(z)))
    gla_ref[:, 0:128] = p[:, C_LQ:C_LQ + 128] * (GLA_DK ** -0.5)
    gla_ref[:, 128:256] = p[:, C_LK:C_LK + 128]
    gla_ref[:, 256:512] = logsig * (1.0 / GLA_GATE_NORM)
    gla_ref[:, 512:768] = p[:, C_LV:C_LV + 256]
    gla_ref[:, 768:1024] = p[:, C_LG:C_LG + 256]

    cache_ref[:, 0:128] = ckv
    cache_ref[:, 128:256] = kn
    cache_ref[:, 256:384] = gv
    cache_ref[:, 384:512] = kr4


def _pre_attn(x, y, pmod, mod, tab, modidx, tabidx, lw, n_ctx_tiles):
    N, D = x.shape
    tm = TOKEN_TILE
    nt = N // tm
    has_y = y is not None
    row = lambda i, mi, ti: (i, 0)
    const2 = lambda i, mi, ti: (0, 0)
    in_specs = [pl.BlockSpec((tm, D), row)]
    args = [x]
    if has_y:
        yk, gates = y
        in_specs += [pl.BlockSpec((TOP_K, tm, D), lambda i, mi, ti: (0, i, 0)), pl.BlockSpec((tm, LANES), row),
                     pl.BlockSpec((1, 6, D), lambda i, mi, ti: (mi[i], 0, 0))]
        args += [yk, gates, pmod]
    in_specs += [
        pl.BlockSpec((1, 6, D), lambda i, mi, ti: (mi[i], 0, 0)),
        pl.BlockSpec((1, tm, 512), lambda i, mi, ti: (ti[i], 0, 0)),
        pl.BlockSpec((1, D), const2),
        pl.BlockSpec((D, IN_COLS), const2),
        pl.BlockSpec((1, Q_LORA), const2),
        pl.BlockSpec((Q_LORA, 1024), const2),
        pl.BlockSpec((1, KV_LORA), const2),
        pl.BlockSpec((1, 512), const2),
        pl.BlockSpec((1, 128), const2),
        pl.BlockSpec((512, 512), const2),
        pl.BlockSpec((128, 256), const2),
        pl.BlockSpec((1, 256), const2),
    ]
    args += [mod, tab, lw['g_attn'], lw['w_in_r'], lw['g_q_a'], lw['w_q2'], lw['g_kv_a'], lw['g_qh'], lw['g_kh'],
             lw['ones512'], lw['w_gate'], lw['b_gate']]
    out_shape, out_specs = [], []
    if has_y:
        out_shape.append(jax.ShapeDtypeStruct((N, D), F32))
        out_specs.append(pl.BlockSpec((tm, D), row))
    for width, dt in ((1024, BF16), (256, BF16), (512, BF16), (256, BF16), (256, BF16), (1024, F32)):
        out_shape.append(jax.ShapeDtypeStruct((N, width), dt))
        out_specs.append(pl.BlockSpec((tm, width), row))
    out_shape.append(jax.ShapeDtypeStruct(((n_ctx_tiles + 1) * tm, 512), F32))
    out_specs.append(pl.BlockSpec((tm, 512), lambda i, mi, ti: (jnp.minimum(i, n_ctx_tiles), 0)))
    outs = pl.pallas_call(
        functools.partial(_pre_attn_kernel, has_y=has_y),
        grid_spec=pltpu.PrefetchScalarGridSpec(num_scalar_prefetch=2, grid=(nt,), in_specs=in_specs,
                                               out_specs=out_specs),
        out_shape=out_shape,
        compiler_params=pltpu.CompilerParams(dimension_semantics=("arbitrary",), vmem_limit_bytes=VMEM_LIMIT),
        name="pre_attn",
    )(modidx, tabidx, *args)
    if not has_y:
        outs = [x] + list(outs)
    return outs


def _softmax_step(qs, k, m_ref, c):
    s = _dot_nt(qs, k)
    m_old = m_ref[c]
    m_new = jnp.maximum(m_old, jnp.broadcast_to(jnp.max(s, axis=-1, keepdims=True), m_old.shape))
    alpha = jnp.exp2(m_old - m_new)
    p = jnp.exp2(s - jnp.concatenate([m_new] * (k.shape[0] // LANES), axis=1))
    m_ref[c] = m_new
    return alpha, p


def _mla_attn_kernel(*refs, has_ctx, n_chunks, tk):
    if has_ctx:
        q_ref, kc_ref, k_ref, wuv_ref, o_ref, m_ref, l_ref, acc_ref = refs
    else:
        q_ref, k_ref, wuv_ref, o_ref, m_ref, l_ref, acc_ref = refs
        kc_ref = None
    tq = q_ref.shape[0]
    m_ref[...] = jnp.full(m_ref.shape, NEG_BIG, F32)
    l_ref[...] = jnp.zeros(l_ref.shape, F32)
    acc_ref[...] = jnp.zeros(acc_ref.shape, F32)
    q = q_ref[...]
    qs = [jnp.concatenate([q[:, h * 256:(h + 1) * 256] for h in (2 * c, 2 * c + 1)], axis=0) for c in range(2)]

    def process(kblk):
        for c in range(2):
            alpha, p = _softmax_step(qs[c], kblk, m_ref, c)
            l_ref[c] = alpha * l_ref[c] + jnp.broadcast_to(jnp.sum(p, axis=-1, keepdims=True), alpha.shape)
            acc_ref[c] = alpha * acc_ref[c] + _dot(p.astype(BF16), kblk[:, 0:KV_LORA])

    if has_ctx:
        process(kc_ref[0])

    def body(j, carry):
        process(k_ref[pl.ds(pl.multiple_of(j * tk, tk), tk), :])
        return carry

    lax.fori_loop(0, n_chunks, body, 0)
    out = jnp.zeros(o_ref.shape, F32)
    for h in range(MLA_HEADS):
        rows = slice((h % 2) * tq, (h % 2 + 1) * tq)
        o_lat = acc_ref[h // 2, rows, :] / l_ref[h // 2, rows, :]
        out = out + _dot(o_lat.astype(BF16), wuv_ref[h])
    o_ref[...] = out.astype(o_ref.dtype)


def _gqa_attn_kernel(*refs, has_ctx, n_chunks, tk):
    if has_ctx:
        q_ref, kc_ref, vc_ref, k_ref, v_ref, o_ref, m_ref, acc_ref = refs
    else:
        q_ref, k_ref, v_ref, o_ref, m_ref, acc_ref = refs
        kc_ref = vc_ref = None
    tq = q_ref.shape[0]
    m_ref[...] = jnp.full(m_ref.shape, NEG_BIG, F32)
    acc_ref[...] = jnp.zeros(acc_ref.shape, F32)
    q = q_ref[...]
    lane = lax.broadcasted_iota(I32, (tq, LANES), 1)
    low = lane < HEAD_DIM
    qs = []
    for g in range(GQA_KV_HEADS):
        parts = []
        for h in range(g * GQA_GROUP, (g + 1) * GQA_GROUP):
            pair = q[:, (h // 2) * LANES:(h // 2 + 1) * LANES]
            keep = low if h % 2 == 0 else jnp.logical_not(low)
            parts.append(jnp.where(keep, pair, jnp.zeros_like(pair)))
        qs.append(jnp.concatenate(parts, axis=0))

    def process(kblk, vblk):
        for g in range(GQA_KV_HEADS):
            alpha, p = _softmax_step(qs[g], kblk[:, g * LANES:(g + 1) * LANES], m_ref, g)
            acc_ref[g] = alpha * acc_ref[g] + _dot(p.astype(BF16), vblk[:, g * LANES:(g + 1) * LANES])

    if has_ctx:
        process(kc_ref[0], vc_ref[0])

    def body(j, carry):
        sl = pl.ds(pl.multiple_of(j * tk, tk), tk)
        process(k_ref[sl, :], v_ref[sl, :])
        return carry

    lax.fori_loop(0, n_chunks, body, 0)
    for j in range(GQA_HEADS // 2):
        outs = []
        for h in (2 * j, 2 * j + 1):
            a = acc_ref[h // GQA_GROUP, (h % GQA_GROUP) * tq:(h % GQA_GROUP + 1) * tq, :]
            outs.append(a / jnp.where(low, pltpu.roll(a, HEAD_DIM, 1), a))
        o_ref[:, j * LANES:(j + 1) * LANES] = jnp.where(low, outs[0], pltpu.roll(outs[1], HEAD_DIM, 1)).astype(
            o_ref.dtype)


def _attention(kind, q, k, v, ctx, wuv, *, n_seq, seq_len, tile_offset):
    has_ctx = ctx is not None
    tq = min(ATTN_TQ, seq_len)
    tk = min(ATTN_TK, seq_len)
    nq = seq_len // tq
    qw = q.shape[1]
    ow = 256 if kind == 'mla' else 512
    heads = MLA_HEADS if kind == 'mla' else GQA_HEADS
    qmap = lambda b, i: (tile_offset * nq + b * nq + i, 0)
    kmap = lambda b, i: (tile_offset + b, 0)
    cmap = lambda b, i: (b, 0, 0)
    in_specs = [pl.BlockSpec((tq, qw), qmap)]
    args = [q]
    if kind == 'mla':
        if has_ctx:
            in_specs.append(pl.BlockSpec((1,) + ctx[0].shape[1:], cmap))
            args.append(ctx[0])
        in_specs += [pl.BlockSpec((seq_len, 256), kmap), pl.BlockSpec((MLA_HEADS, 128, 256), lambda b, i: (0, 0, 0))]
        args += [k, wuv]
        body = _mla_attn_kernel
    else:
        if has_ctx:
            in_specs += [pl.BlockSpec((1,) + ctx[0].shape[1:], cmap), pl.BlockSpec((1,) + ctx[1].shape[1:], cmap)]
            args += [ctx[0], ctx[1]]
        in_specs += [pl.BlockSpec((seq_len, 256), kmap), pl.BlockSpec((seq_len, 256), kmap)]
        args += [k, v]
        body = _gqa_attn_kernel
    return pl.pallas_call(
        functools.partial(body, has_ctx=has_ctx, n_chunks=seq_len // tk, tk=tk),
        grid=(n_seq, nq),
        in_specs=in_specs,
        out_specs=pl.BlockSpec((tq, ow), lambda b, i: (b * nq + i, 0)),
        out_shape=jax.ShapeDtypeStruct((n_seq * seq_len, ow), BF16),
        scratch_shapes=[pltpu.VMEM((2, heads // 2 * tq, LANES), F32)] * (3 if kind == 'mla' else 2),
        compiler_params=pltpu.CompilerParams(dimension_semantics=("arbitrary", "arbitrary"),
                                             vmem_limit_bytes=VMEM_LIMIT),
        name=kind + ("_attn_latent" if has_ctx else "_attn_context"),
    )(*args)


def _gla_kernel(blk_ref, seq_ref, first_ref, *refs, reverse):
    del blk_ref, seq_ref
    if reverse:
        gin_ref, s0_ref, he_ref, of_ref, gout_ref, ones_ref, o_ref, sout_ref, st_ref = refs
    else:
        gin_ref, s0_ref, he_ref, o_ref, sout_ref, st_ref = refs
    i = pl.program_id(0)

    @pl.when(first_ref[i] == 1)
    def _():
        st_ref[...] = s0_ref[0]

    R, S = TOKEN_TILE, GLA_STEP
    ns = R // S
    q = gin_ref[:, 0:128]
    k = gin_ref[:, 128:256]
    g = gin_ref[:, 384:512] if reverse else gin_ref[:, 256:384]
    v = gin_ref[:, 512:768]
    pos = lax.broadcasted_iota(I32, (R, 128), 0) % S
    b = g
    for s in (1, 2, 4, 8):
        if reverse:
            b = b + jnp.where(pos < S - s, pltpu.roll(b, R - s, 0), 0.0)
        else:
            b = b + jnp.where(pos >= s, pltpu.roll(b, s, 0), 0.0)

    def step_row(a, j):
        w = a.shape[-1]
        a3 = a.reshape(ns, S, w)
        return jnp.broadcast_to(a3[:, j:j + 1, :], (ns, S, w)).reshape(R, w)

    b_edge = step_row(b, 0 if reverse else S - 1)
    qt = (q * jnp.exp(b)).astype(BF16)
    kt = (k * jnp.exp(b_edge - b)).astype(BF16)
    d_edge = jnp.exp(b_edge)
    vb = v.astype(BF16)
    he = he_ref[...]

    o_intra = jnp.zeros((R, 256), F32)
    for j in range(S):
        cond = (pos <= j) if reverse else (pos >= j)
        bj, kj, vj = step_row(b, j), step_row(k, j), step_row(v, j)
        t = jnp.where(cond, q * kj * jnp.exp(jnp.where(cond, b - bj, 0.0)), 0.0)
        o_intra = o_intra + _dot(t.astype(BF16), he) * vj

    own_head = (lax.broadcasted_iota(I32, (256, 128), 0) // GLA_DV) == (lax.broadcasted_iota(I32, (256, 128), 1) // GLA_DK)
    st = st_ref[...]
    parts = [None] * ns
    for t in (range(ns - 1, -1, -1) if reverse else range(ns)):
        rows = slice(t * S, (t + 1) * S)
        parts[t] = _dot_nt(qt[rows], st.astype(BF16)) + o_intra[rows]
        ut = _dot_tn(vb[rows], kt[rows])
        st = d_edge[t * S:t * S + 1, :] * st + jnp.where(own_head, ut, 0.0)
    st_ref[...] = st
    sout_ref[0] = st
    o = jnp.concatenate(parts, axis=0)
    if reverse:
        o = o + of_ref[...]
        ms = _split_dot(o * o, ones_ref[...]) * (1.0 / GLA_DV)
        on = o * lax.rsqrt(ms + EPS) * gout_ref[...]
        lg = gin_ref[:, 768:1024]
        o_ref[...] = (on * (lg * jax.nn.sigmoid(lg))).astype(o_ref.dtype)
    else:
        o_ref[...] = o


def _gla(gin, s0, o_f, lw, order, *, reverse):
    blk, seq, first = order
    N = gin.shape[0]
    tm = TOKEN_TILE
    n_seq = s0.shape[0]
    tile = lambda i, b, s, f: (b[i], 0)
    state = lambda i, b, s, f: (s[i], 0, 0)
    const2 = lambda i, b, s, f: (0, 0)
    in_specs = [pl.BlockSpec((tm, 1024), tile), pl.BlockSpec((1, 256, 128), state), pl.BlockSpec((128, 256), const2)]
    args = [gin, s0, lw['head_expand']]
    if reverse:
        in_specs += [pl.BlockSpec((tm, 256), tile), pl.BlockSpec((1, 256), const2), pl.BlockSpec((256, 256), const2)]
        args += [o_f, lw['g_gla_out'], lw['ones256']]
    return pl.pallas_call(
        functools.partial(_gla_kernel, reverse=reverse),
        grid_spec=pltpu.PrefetchScalarGridSpec(
            num_scalar_prefetch=3, grid=(N // tm,), in_specs=in_specs,
            out_specs=[pl.BlockSpec((tm, 256), tile), pl.BlockSpec((1, 256, 128), state)],
            scratch_shapes=[pltpu.VMEM((256, 128), F32)]),
        out_shape=[jax.ShapeDtypeStruct((N, 256), BF16 if reverse else F32),
                   jax.ShapeDtypeStruct((n_seq, 256, 128), F32)],
        compiler_params=pltpu.CompilerParams(dimension_semantics=("arbitrary",), vmem_limit_bytes=VMEM_LIMIT),
        name="gla_bwd" if reverse else "gla_fwd",
    )(blk, seq, first, *args)


def _post_attn_kernel(modidx_ref, x_ref, omc_ref, oml_ref, ogc_ref, ogl_ref, ol_ref, mod_ref, wout_ref, gffn_ref,
                      wrh_ref, wrl_ref, br_ref, x1_ref, h2_ref, ti_ref, tg_ref, *, n_ctx_tiles):
    del modidx_ref
    mod = mod_ref[0]
    gate1, shift2, scale2 = mod[2:3, :], mod[3:4, :], mod[4:5, :]
    is_ctx = pl.program_id(0) < n_ctx_tiles
    om = jnp.where(is_ctx, omc_ref[...], oml_ref[...])
    og = jnp.where(is_ctx, ogc_ref[...], ogl_ref[...])
    o = _dot(om, wout_ref[0:256, :]) + _dot(og, wout_ref[256:768, :]) + _dot(ol_ref[...], wout_ref[768:1024, :])
    x1 = x_ref[...] + gate1 * o
    x1_ref[...] = x1
    xn = x1 * lax.rsqrt(jnp.mean(x1 * x1, axis=-1, keepdims=True) + EPS) * gffn_ref[...]
    h2 = xn * (1.0 + scale2) + shift2
    hi = h2.astype(BF16)
    h2_ref[...] = hi
    lo = (h2 - hi.astype(F32)).astype(BF16)
    wrh = wrh_ref[...]
    logits = _dot(hi, wrh) + _dot(hi, wrl_ref[...]) + _dot(lo, wrh) + br_ref[...]

    lane = lax.broadcasted_iota(I32, logits.shape, 1).astype(F32)
    work = logits
    vals, idxs = [], []
    for _ in range(TOP_K):
        m = jnp.max(work, axis=-1, keepdims=True)
        idx = jnp.min(jnp.where(work == m, lane, float(LANES)), axis=-1, keepdims=True)
        vals.append(m)
        idxs.append(idx)
        work = jnp.where(lane == idx, NEG_BIG * 4.0, work)
    es = [jnp.exp(vv - vals[0]) for vv in vals]
    denom = es[0] + es[1] + es[2] + es[3]
    ti = jnp.zeros(logits.shape, F32)
    tg = jnp.zeros(logits.shape, F32)
    for r in range(TOP_K):
        ti = jnp.where(lane == float(r), idxs[r], ti)
        tg = jnp.where(lane == float(r), es[r] / denom, tg)
    ti_ref[...] = ti.astype(I32)
    tg_ref[...] = tg


def _post_attn(x, o_mla, o_gqa, o_gla, mod, modidx, lw, n_ctx_tiles):
    N, D = x.shape
    tm = TOKEN_TILE
    row = lambda i, mi: (i, 0)
    ctx_row = lambda i, mi: (jnp.minimum(i, n_ctx_tiles - 1), 0)
    lat_row = lambda i, mi: (jnp.maximum(i - n_ctx_tiles, 0), 0)
    const2 = lambda i, mi: (0, 0)
    return pl.pallas_call(
        functools.partial(_post_attn_kernel, n_ctx_tiles=n_ctx_tiles),
        grid_spec=pltpu.PrefetchScalarGridSpec(
            num_scalar_prefetch=1, grid=(N // tm,),
            in_specs=[pl.BlockSpec((tm, D), row), pl.BlockSpec((tm, 256), ctx_row), pl.BlockSpec((tm, 256), lat_row),
                      pl.BlockSpec((tm, 512), ctx_row), pl.BlockSpec((tm, 512), lat_row),
                      pl.BlockSpec((tm, 256), row), pl.BlockSpec((1, 6, D), lambda i, mi: (mi[i], 0, 0)),
                      pl.BlockSpec((D, D), const2), pl.BlockSpec((1, D), const2),
                      pl.BlockSpec((D, LANES), const2), pl.BlockSpec((D, LANES), const2),
                      pl.BlockSpec((1, LANES), const2)],
            out_specs=[pl.BlockSpec((tm, D), row), pl.BlockSpec((tm, D), row), pl.BlockSpec((tm, LANES), row),
                       pl.BlockSpec((tm, LANES), row)]),
        out_shape=[jax.ShapeDtypeStruct((N, D), F32), jax.ShapeDtypeStruct((N, D), BF16),
                   jax.ShapeDtypeStruct((N, LANES), I32), jax.ShapeDtypeStruct((N, LANES), F32)],
        compiler_params=pltpu.CompilerParams(dimension_semantics=("arbitrary",), vmem_limit_bytes=VMEM_LIMIT),
        name="post_attn",
    )(modidx, x, o_mla[0], o_mla[1], o_gqa[0], o_gqa[1], o_gla, mod, lw['w_out'], lw['g_ffn'], lw['w_router_hi'],
      lw['w_router_lo'], lw['b_router'])


def _moe_kernel(be_ref, bx_ref, valid_ref, first_ref, x_ref, wgu_ref, bgu_ref, wd_ref, bd_ref, o_ref,
                wgu_bf, wd_bf):
    del be_ref, bx_ref
    i = pl.program_id(0)

    @pl.when(first_ref[i] == 1)
    def _():
        wgu_bf[...] = wgu_ref[...].astype(BF16)
        wd_bf[...] = wd_ref[...].astype(BF16)

    @pl.when(valid_ref[i] == 1)
    def _():
        gu = _dot(x_ref[...], wgu_bf[...]) + bgu_ref[...]
        gate = jnp.minimum(gu[:, :D_EXPERT], SWIGLU_LIMIT)
        up = jnp.clip(gu[:, D_EXPERT:], -SWIGLU_LIMIT, SWIGLU_LIMIT)
        act = gate * jax.nn.sigmoid(SWIGLU_ALPHA * gate) * (up + 1.0)
        o_ref[...] = (_dot(act.astype(BF16), wd_bf[...]) + bd_ref[...]).astype(o_ref.dtype)

    @pl.when(valid_ref[i] == 0)
    def _():
        o_ref[...] = jnp.zeros(o_ref.shape, o_ref.dtype)


def _moe_experts(xs, sched, w_gate_up, b_gate_up, w_down, b_down, layer):
    block_e, block_x, valid, first = sched
    D = xs.shape[1]
    n_blocks = xs.shape[0] // MOE_ROWS
    L, E = w_gate_up.shape[:2]
    wmap = lambda i, be, bx, va, fi: (layer, be[i], 0, 0)
    xmap = lambda i, be, bx, va, fi: (bx[i], 0)
    return pl.pallas_call(
        _moe_kernel,
        grid_spec=pltpu.PrefetchScalarGridSpec(
            num_scalar_prefetch=4, grid=(n_blocks,),
            in_specs=[pl.BlockSpec((MOE_ROWS, D), xmap),
                      pl.BlockSpec((None, None, D, 2 * D_EXPERT), wmap),
                      pl.BlockSpec((None, None, 1, 2 * D_EXPERT), wmap),
                      pl.BlockSpec((None, None, D_EXPERT, D), wmap),
                      pl.BlockSpec((None, None, 1, D), wmap)],
            out_specs=pl.BlockSpec((MOE_ROWS, D), lambda i, be, bx, va, fi: (i, 0)),
            scratch_shapes=[pltpu.VMEM((D, 2 * D_EXPERT), BF16), pltpu.VMEM((D_EXPERT, D), BF16)]),
        out_shape=jax.ShapeDtypeStruct(xs.shape, BF16),
        compiler_params=pltpu.CompilerParams(dimension_semantics=("arbitrary",), vmem_limit_bytes=VMEM_LIMIT),
        name="moe_experts",
    )(block_e, block_x, valid, first, xs, w_gate_up, b_gate_up.reshape(L, E, 1, 2 * D_EXPERT), w_down,
      b_down.reshape(L, E, 1, D))


def _route(top_idx, n_blocks):
    N = top_idx.shape[0]
    onehot = (top_idx[:, :, None] == jnp.arange(N_EXPERTS, dtype=I32)[None, None, :]).astype(I32).sum(axis=1)
    before = jnp.cumsum(onehot, axis=0) - onehot
    counts = onehot.sum(axis=0)
    padded = (counts + MOE_ROWS - 1) // MOE_ROWS * MOE_ROWS
    pad_end = jnp.cumsum(padded)
    pad_start = pad_end - padded
    dest = pad_start[top_idx] + jnp.take_along_axis(before, top_idx, axis=1)
    tok = jnp.broadcast_to(jnp.arange(N, dtype=I32)[:, None], dest.shape)
    slot_tok = jnp.zeros((n_blocks * MOE_ROWS,), I32).at[dest.reshape(-1)].set(tok.reshape(-1))
    blk_start = jnp.arange(n_blocks, dtype=I32) * MOE_ROWS
    valid = blk_start < pad_end[-1]
    block_e = jnp.minimum(jnp.searchsorted(pad_end, blk_start, side='right'), N_EXPERTS - 1).astype(I32)
    n_valid = pad_end[-1] // MOE_ROWS
    block_x = jnp.minimum(jnp.arange(n_blocks, dtype=I32), n_valid - 1).astype(I32)
    block_e = block_e[block_x]
    prev_e = jnp.concatenate([jnp.full((1,), -1, I32), block_e[:-1]])
    first = (valid & (block_e != prev_e)).astype(I32)
    return dest, slot_tok, (block_e, block_x, valid.astype(I32), first)


def _final_kernel(modidx_ref, x_ref, y_ref, tg_ref, pmod_ref, g_ref, o_ref):
    del modidx_ref
    x = x_ref[...] + pmod_ref[0, 5:6, :] * _combine_experts(y_ref, tg_ref)
    o_ref[...] = x * lax.rsqrt(jnp.mean(x * x, axis=-1, keepdims=True) + EPS) * g_ref[...]


def _final_norm(x, y, pmod, modidx, g_final):
    N, D = x.shape
    tm = TOKEN_TILE
    row = lambda i, mi: (i, 0)
    return pl.pallas_call(
        _final_kernel,
        grid_spec=pltpu.PrefetchScalarGridSpec(
            num_scalar_prefetch=1, grid=(N // tm,),
            in_specs=[pl.BlockSpec((tm, D), row), pl.BlockSpec((TOP_K, tm, D), lambda i, mi: (0, i, 0)),
                      pl.BlockSpec((tm, LANES), row),
                      pl.BlockSpec((1, 6, D), lambda i, mi: (mi[i], 0, 0)), pl.BlockSpec((1, D), lambda i, mi: (0, 0))],
            out_specs=pl.BlockSpec((tm, D), row)),
        out_shape=jax.ShapeDtypeStruct((N, D), F32),
        compiler_params=pltpu.CompilerParams(dimension_semantics=("arbitrary",), vmem_limit_bytes=VMEM_LIMIT),
        name="final_norm",
    )(modidx, x, y[0], y[1], pmod, g_final)


def _rope_tables(seq_len):
    pos = np.arange(seq_len)
    rowp = (pos // GRID_W).astype(np.float32)
    colp = (pos % GRID_W).astype(np.float32)

    def tables(rdim, copies):
        quarter = rdim // 4
        inv_freq = jnp.asarray(ROPE_THETA, F32) ** (-jnp.arange(quarter, dtype=F32) / quarter)
        ar = jnp.asarray(rowp)[:, None] * inv_freq[None, :]
        ac = jnp.asarray(colp)[:, None] * inv_freq[None, :]
        cos = jnp.concatenate([jnp.cos(ar), jnp.cos(ar), jnp.cos(ac), jnp.cos(ac)], axis=1)
        sin = jnp.concatenate([-jnp.sin(ar), jnp.sin(ar), -jnp.sin(ac), jnp.sin(ac)], axis=1)
        return jnp.tile(cos, (1, copies)), jnp.tile(sin, (1, copies))

    c64, s64 = tables(HEAD_DIM, 2)
    c32, s32 = tables(MLA_ROPE, 4)
    lat = jnp.concatenate([c64, s64, c32, s32], axis=1).reshape(seq_len // TOKEN_TILE, TOKEN_TILE, 512)
    ident = jnp.concatenate([jnp.ones((TOKEN_TILE, 128), F32), jnp.zeros((TOKEN_TILE, 128), F32)] * 2, axis=1)
    return jnp.concatenate([ident[None], lat], axis=0)


def _block_ones(n, blk):
    r = np.arange(n) // blk
    return jnp.asarray((r[:, None] == r[None, :]).astype(np.float32), BF16)


def kernel(x_prompt, x_sample, cache_mla_ckv, cache_mla_krope, cache_gqa_k, cache_gqa_v, state_gla, c, c_ctx, w_mod, b_mod, g_attn_norm, g_ffn_norm, w_in, g_q_a, w_uq, g_kv_a, w_ukv, g_q_head, g_k_head, w_gk_fwd, b_gk_fwd, w_gk_bwd, b_gk_bwd, g_gla_out, w_out, w_router, b_router, w_gate_up, b_gate_up, w_down, b_down, g_final):
    B, T, D = x_prompt.shape
    BD, TD, _ = x_sample.shape
    L = w_in.shape[0]
    P = cache_mla_ckv.shape[2]
    tm = TOKEN_TILE
    n_ctx = B * T
    N = n_ctx + BD * TD
    assert T == tm and TD % ATTN_TQ == 0 and n_ctx % TD == 0 and BD + 1 <= 16
    n_ctx_tiles = n_ctx // tm
    nt = N // tm
    tiles_per_lat = TD // tm

    tile_ids = np.arange(nt)
    lat_tile = np.maximum(tile_ids - n_ctx_tiles, 0)
    is_lat = tile_ids >= n_ctx_tiles
    modidx = jnp.asarray(np.where(is_lat, 1 + lat_tile // tiles_per_lat, 0), I32)
    tabidx = jnp.asarray(np.where(is_lat, 1 + lat_tile % tiles_per_lat, 0), I32)
    seq_of_tile = np.where(is_lat, B + lat_tile // tiles_per_lat, tile_ids)
    first_fwd = np.where(is_lat, lat_tile % tiles_per_lat == 0, True)
    last_fwd = np.where(is_lat, lat_tile % tiles_per_lat == tiles_per_lat - 1, True)
    order_fwd = (jnp.asarray(tile_ids, I32), jnp.asarray(seq_of_tile, I32), jnp.asarray(first_fwd, I32))
    rev = tile_ids[::-1]
    order_bwd = (jnp.asarray(rev, I32), jnp.asarray(seq_of_tile[rev], I32), jnp.asarray(last_fwd[rev], I32))
    n_seq = B + BD

    cvecs = jnp.zeros((16, D), F32).at[0].set(c_ctx).at[1:1 + BD].set(c)
    mods = _modulation(cvecs, w_mod, b_mod).reshape(L, 16, 6, D)

    o = np.cumsum([0, Q_LORA, KV_LORA, MLA_ROPE, 512, 128, 128, 128, 128, 256, 16, 16, 256])
    seg = lambda j: w_in[:, :, o[j]:o[j + 1]]
    w_in_r = jnp.concatenate(
        [seg(0), seg(1), seg(3), seg(4), seg(5), seg(6), seg(7), seg(8), seg(11), seg(2), seg(2), seg(2), seg(2),
         seg(9), seg(10), jnp.zeros((L, D, 128 - 2 * GLA_GATE_RANK), F32)], axis=-1).astype(BF16)
    uq = w_uq.reshape(L, Q_LORA, MLA_HEADS, MLA_QK)
    ukv = w_ukv.reshape(L, KV_LORA, MLA_HEADS, MLA_NOPE + MLA_V)
    w_comb = _fold_q_weights(uq[..., :MLA_NOPE].transpose(0, 2, 1, 3), ukv[..., :MLA_NOPE].transpose(0, 2, 1, 3))
    q_pad = jnp.zeros((L, Q_LORA, 256 - KV_LORA - MLA_ROPE), F32)
    w_q2 = (jnp.concatenate([part for h in range(MLA_HEADS)
                             for part in (w_comb[:, h], uq[:, :, h, MLA_NOPE:], q_pad)], axis=-1)
            * (MLA_QK ** -0.5 * LOG2E)).astype(BF16)
    w_uv = ukv[..., MLA_NOPE:].transpose(0, 2, 1, 3)
    eye_h = jnp.eye(MLA_HEADS, dtype=F32)
    wuv_pad = jnp.einsum('lhkv,hg->lhkgv', w_uv, eye_h).reshape(L, MLA_HEADS, KV_LORA, MLA_HEADS * MLA_V).astype(BF16)
    w_gate = jnp.zeros((L, 128, 256), F32).at[:, 0:16, 0:128].set(w_gk_fwd).at[:, 16:32, 128:256].set(w_gk_bwd)
    w_gate = w_gate.astype(BF16)
    b_gate = jnp.concatenate([b_gk_fwd, b_gk_bwd], axis=-1)
    w_out_bf = w_out.astype(BF16)
    wr_pad = jnp.pad(w_router, ((0, 0), (0, 0), (0, LANES - N_EXPERTS)))
    wr_hi = wr_pad.astype(BF16)
    wr_lo = (wr_pad - wr_hi.astype(F32)).astype(BF16)
    br_pad = jnp.pad(b_router, ((0, 0), (0, LANES - N_EXPERTS)), constant_values=NEG_BIG)
    ones512 = _block_ones(512, HEAD_DIM)
    head_expand = jnp.asarray((np.arange(128)[:, None] // GLA_DK == np.arange(256)[None, :] // GLA_DV)
                              .astype(np.float32), BF16)
    tab = _rope_tables(TD)

    x = jnp.concatenate([x_prompt.reshape(n_ctx, D), x_sample.reshape(BD * TD, D)], axis=0)
    y = None
    n_blocks = (N * TOP_K) // MOE_ROWS + N_EXPERTS
    caches, states = [], []
    for l in range(L):
        lw = dict(g_attn=g_attn_norm[l][None], w_in_r=w_in_r[l], g_q_a=g_q_a[l][None], w_q2=w_q2[l],
                  g_kv_a=g_kv_a[l][None], g_qh=jnp.tile(g_q_head[l], GQA_HEADS)[None],
                  g_kh=jnp.tile(g_k_head[l], GQA_KV_HEADS)[None], ones512=ones512, w_gate=w_gate[l],
                  b_gate=b_gate[l][None], head_expand=head_expand, g_gla_out=jnp.tile(g_gla_out[l], GLA_HEADS)[None],
                  ones256=ones512[:256, :256], w_out=w_out_bf[l], g_ffn=g_ffn_norm[l][None],
                  w_router_hi=wr_hi[l], w_router_lo=wr_lo[l], b_router=br_pad[l][None])
        pmod = mods[l - 1] if l > 0 else None
        x, qm, kvc, qg, kdup, vdup, gin, cache = _pre_attn(x, y, pmod, mods[l], tab, modidx, tabidx, lw, n_ctx_tiles)
        caches.append(cache[:n_ctx])

        mla_ctx = jnp.concatenate([cache_mla_ckv[:, l]] + [cache_mla_krope[:, l]] * 4, axis=-1).astype(BF16)
        ck = cache_gqa_k[:, l]
        cv = cache_gqa_v[:, l]
        k_ctx = jnp.concatenate([ck[:, :, 0], ck[:, :, 0], ck[:, :, 1], ck[:, :, 1]], axis=-1).astype(BF16)
        ones_v = jnp.ones(cv.shape[:2] + (HEAD_DIM,), F32)
        v_ctx = jnp.concatenate([cv[:, :, 0], ones_v, cv[:, :, 1], ones_v], axis=-1).astype(BF16)
        o_mla = (_attention('mla', qm, kvc, None, None, wuv_pad[l], n_seq=B, seq_len=T, tile_offset=0),
                 _attention('mla', qm, kvc, None, (mla_ctx,), wuv_pad[l], n_seq=BD, seq_len=TD,
                            tile_offset=n_ctx // TD))
        o_gqa = (_attention('gqa', qg, kdup, vdup, None, None, n_seq=B, seq_len=T, tile_offset=0),
                 _attention('gqa', qg, kdup, vdup, (k_ctx, v_ctx), None, n_seq=BD, seq_len=TD,
                            tile_offset=n_ctx // TD))

        eye_g = jnp.eye(GLA_HEADS, dtype=F32)
        st_lat = jnp.einsum('bshde,hg->bshegd', state_gla[:, l], eye_g).reshape(BD, 2, 256, 128)
        zeros_ctx = jnp.zeros((B, 256, 128), F32)
        o_f, s_f = _gla(gin, jnp.concatenate([zeros_ctx, st_lat[:, 0]], axis=0), None, lw, order_fwd, reverse=False)
        o_gla, s_b = _gla(gin, jnp.concatenate([zeros_ctx, st_lat[:, 1]], axis=0), o_f, lw, order_bwd, reverse=True)
        states.append((s_f[:B], s_b[:B]))

        x1, h2, ti, tg = _post_attn(x, o_mla, o_gqa, o_gla, mods[l], modidx, lw, n_ctx_tiles)
        dest, slot_tok, sched = _route(ti[:, :TOP_K], n_blocks)
        xs = h2[slot_tok]
        yb = _moe_experts(xs, sched, w_gate_up, b_gate_up, w_down, b_down, l)
        y = (yb[dest.T.reshape(-1)].reshape(TOP_K, N, D), tg)
        x = x1

    out = _final_norm(x, y, mods[L - 1], modidx, g_final[None])
    y_prompt = out[:n_ctx].reshape(B, T, D)
    y_sample = out[n_ctx:].reshape(BD, TD, D)
    cache_all = jnp.stack([cc.reshape(B, T, 512) for cc in caches], axis=1)
    new_ckv = cache_all[..., 0:128]
    new_k = cache_all[..., 128:256].reshape(B, L, T, GQA_KV_HEADS, HEAD_DIM)
    new_v = cache_all[..., 256:384].reshape(B, L, T, GQA_KV_HEADS, HEAD_DIM)
    new_krope = cache_all[..., 384:384 + MLA_ROPE]

    def unpack_state(st):
        s5 = st.reshape(B, GLA_HEADS, GLA_DV, GLA_HEADS, GLA_DK)
        diag = jnp.stack([s5[:, h, :, h, :] for h in range(GLA_HEADS)], axis=1)
        return diag.transpose(0, 1, 3, 2)

    new_state = jnp.stack([jnp.stack([unpack_state(sf), unpack_state(sb)], axis=1) for sf, sb in states], axis=1)
    return (y_prompt, y_sample, new_ckv, new_krope, new_k, new_v, new_state)
```

```python
import functools

import jax
import jax.numpy as jnp
import numpy as np
from jax import lax
from jax.experimental import pallas as pl
from jax.experimental.pallas import tpu as pltpu

F32 = jnp.float32
BF16 = jnp.bfloat16
I32 = jnp.int32

D_MODEL = 1024
GRID_W = 64
ROPE_THETA = 10000.0
EPS = 1e-6
MLA_HEADS = 4
Q_LORA = 256
KV_LORA = 128
MLA_NOPE = 64
MLA_ROPE = 32
MLA_V = 64
MLA_QK = MLA_NOPE + MLA_ROPE
GQA_HEADS = 8
GQA_KV_HEADS = 2
GQA_GROUP = GQA_HEADS // GQA_KV_HEADS
HEAD_DIM = 64
GLA_HEADS = 4
GLA_DK = 32
GLA_DV = 64
GLA_GATE_RANK = 16
GLA_GATE_NORM = 16.0
N_EXPERTS = 32
TOP_K = 4
D_EXPERT = D_MODEL
SWIGLU_LIMIT = 7.0
SWIGLU_ALPHA = 1.702

LANES = 128
SUBLANES = 8
TOKEN_TILE = 256
GLA_STEP = 16
MOE_ROWS = 512
ATTN_TQ = 512
ATTN_TK = 512
VMEM_LIMIT = 56 * 1024 * 1024
NEG_BIG = -1e30
LOG2E = 1.4426950408889634

C_CQ, C_CKV, C_GQ, C_GK, C_GV = 0, 256, 384, 896, 1024
C_LQ, C_LK, C_LV, C_LG, C_KR, C_GATE = 1152, 1280, 1408, 1664, 1920, 2048
IN_COLS = 2176


def _dot(a, b):
    return jnp.dot(a, b, preferred_element_type=F32)


def _dot_nt(a, b):
    return lax.dot_general(a, b, (((1,), (1,)), ((), ())), preferred_element_type=F32)


def _dot_tn(a, b):
    return lax.dot_general(a, b, (((0,), (0,)), ((), ())), preferred_element_type=F32)


def _split_dot(x, ones_bf):
    hi = x.astype(BF16)
    lo = (x - hi.astype(F32)).astype(BF16)
    return _dot(hi, ones_bf) + _dot(lo, ones_bf)


def _rope(x, cos, sin_signed, quarter):
    width = x.shape[-1]
    lane = lax.broadcasted_iota(I32, x.shape, 1)
    first = (lane % (2 * quarter)) < quarter
    partner = jnp.where(first, pltpu.roll(x, width - quarter, 1), pltpu.roll(x, quarter, 1))
    return x * cos + partner * sin_signed


def _mod_kernel(c_ref, w_ref, b_ref, o_ref):
    c = c_ref[...]
    a = c * jax.nn.sigmoid(c)
    o_ref[0] = jnp.dot(a, w_ref[0], precision=lax.Precision.HIGHEST, preferred_element_type=F32) + b_ref[0]


def _modulation(cvecs, w_mod, b_mod):
    L, D, D6 = w_mod.shape
    tn = 1536
    return pl.pallas_call(
        _mod_kernel,
        grid=(L, D6 // tn),
        in_specs=[pl.BlockSpec((16, D), lambda l, j: (0, 0)),
                  pl.BlockSpec((1, D, tn), lambda l, j: (l, 0, j)),
                  pl.BlockSpec((1, 1, tn), lambda l, j: (l, 0, j))],
        out_specs=pl.BlockSpec((1, 16, tn), lambda l, j: (l, 0, j)),
        out_shape=jax.ShapeDtypeStruct((L, 16, D6), F32),
        compiler_params=pltpu.CompilerParams(dimension_semantics=("arbitrary", "arbitrary"),
                                             vmem_limit_bytes=VMEM_LIMIT),
        name="modulation",
    )(cvecs, w_mod, b_mod.reshape(L, 1, D6))


def _fold_kernel(a_ref, b_ref, o_ref):
    o_ref[0, 0] = lax.dot_general(a_ref[0, 0], b_ref[0, 0], (((1,), (1,)), ((), ())),
                                  precision=lax.Precision.HIGHEST, preferred_element_type=F32)


def _fold_q_weights(wq_nope, wk_nope):
    L, H, A, K = wq_nope.shape
    B = wk_nope.shape[2]
    return pl.pallas_call(
        _fold_kernel,
        grid=(L, H),
        in_specs=[pl.BlockSpec((1, 1, A, K), lambda l, h: (l, h, 0, 0)),
                  pl.BlockSpec((1, 1, B, K), lambda l, h: (l, h, 0, 0))],
        out_specs=pl.BlockSpec((1, 1, A, B), lambda l, h: (l, h, 0, 0)),
        out_shape=jax.ShapeDtypeStruct((L, H, A, B), F32),
        name="fold_q_weights",
    )(wq_nope, wk_nope)


def _from_token_tiles(ref, rows):
    return jnp.concatenate([ref[pl.ds(c, rows, stride=SUBLANES), :] for c in range(D_MODEL // LANES)], axis=1)


def _to_token_tiles(ref, x):
    for c in range(D_MODEL // LANES):
        ref[pl.ds(c, x.shape[0], stride=SUBLANES), :] = x[:, c * LANES:(c + 1) * LANES]


def _combine_experts(y_refs, tg_ref):
    tg = tg_ref[...]
    acc = None
    for k in range(TOP_K):
        term = tg[:, k:k + 1] * _from_token_tiles(y_refs[k], tg.shape[0])
        acc = term if acc is None else acc + term
    return acc


def _pre_attn_kernel(modidx_ref, tabidx_ref, *refs, has_y):
    del modidx_ref, tabidx_ref
    if has_y:
        x_ref, y_ref, tg_ref, pmod_ref = refs[0], refs[1:1 + TOP_K], refs[1 + TOP_K], refs[2 + TOP_K]
        refs = refs[3 + TOP_K:]
    else:
        x_ref = refs[0]
        refs = refs[1:]
    (mod_ref, tab_ref, gattn_ref, win_ref, gqa_ref, wq2_ref, gkva_ref, gqh_ref, gkh_ref, ones_ref,
     wg_ref, bg_ref) = refs[:12]
    outs = refs[12:]
    if has_y:
        xres_ref = outs[0]
        outs = outs[1:]
    qm_ref, kvc_ref, qg_ref, kdup_ref, vdup_ref, gla_ref, cache_ref = outs

    x = x_ref[...]
    if has_y:
        x = x + pmod_ref[0, 5:6, :] * _combine_experts(y_ref, tg_ref)
        xres_ref[...] = x
    mod = mod_ref[0]
    shift1, scale1 = mod[0:1, :], mod[1:2, :]
    xn = x * lax.rsqrt(jnp.mean(x * x, axis=-1, keepdims=True) + EPS) * gattn_ref[...]
    h = xn * (1.0 + scale1) + shift1
    p = _dot(h.astype(BF16), win_ref[...])

    tab = tab_ref[0]
    cos64, sin64 = tab[:, 0:128], tab[:, 128:256]
    cos32, sin32 = tab[:, 256:384], tab[:, 384:512]
    ones = ones_ref[...]

    cq = p[:, C_CQ:C_CQ + Q_LORA]
    cqn = cq * lax.rsqrt(jnp.mean(cq * cq, axis=-1, keepdims=True) + EPS) * gqa_ref[...]
    qm = _dot(cqn.astype(BF16), wq2_ref[...])
    q_rope = _rope(jnp.concatenate([qm[:, h * 256 + 128:(h + 1) * 256] for h in range(MLA_HEADS)], axis=1),
                   jnp.concatenate([cos32] * MLA_HEADS, axis=1), jnp.concatenate([sin32] * MLA_HEADS, axis=1),
                   MLA_ROPE // 4)
    for h in range(MLA_HEADS):
        qm_ref[:, h * 256:h * 256 + 128] = qm[:, h * 256:h * 256 + 128].astype(BF16)
        qm_ref[:, h * 256 + 128:(h + 1) * 256] = q_rope[:, h * 128:(h + 1) * 128].astype(BF16)
    ckv_raw = p[:, C_CKV:C_CKV + KV_LORA]
    ckv = ckv_raw * lax.rsqrt(jnp.mean(ckv_raw * ckv_raw, axis=-1, keepdims=True) + EPS) * gkva_ref[...]
    kr4 = p[:, C_KR:C_KR + 128]
    kr4_rot = _rope(kr4, cos32, sin32, MLA_ROPE // 4)
    kvc_ref[:, 0:128] = ckv.astype(BF16)
    kvc_ref[:, 128:256] = kr4_rot.astype(BF16)

    gq = p[:, C_GQ:C_GQ + 512]
    ms_q = _split_dot(gq * gq, ones) * (1.0 / HEAD_DIM)
    qn = gq * lax.rsqrt(ms_q + EPS) * gqh_ref[...]
    cos_q = jnp.concatenate([cos64] * 4, axis=1)
    sin_q = jnp.concatenate([sin64] * 4, axis=1)
    qg = _rope(qn, cos_q, sin_q, HEAD_DIM // 4) * (HEAD_DIM ** -0.5 * LOG2E)
    qg_ref[...] = qg.astype(BF16)
    gk = p[:, C_GK:C_GK + 128]
    ms_k = _split_dot(gk * gk, ones[0:128, 0:128]) * (1.0 / HEAD_DIM)
    kn = gk * lax.rsqrt(ms_k + EPS) * gkh_ref[...]
    kg = _rope(kn, cos64, sin64, HEAD_DIM // 4)
    gv = p[:, C_GV:C_GV + 128]
    lane = lax.broadcasted_iota(I32, kg.shape, 1)
    low = lane < HEAD_DIM
    kg_sw = pltpu.roll(kg, HEAD_DIM, 1)
    gv_sw = pltpu.roll(gv, HEAD_DIM, 1)
    kdup_ref[:, 0:128] = jnp.where(low, kg, kg_sw).astype(BF16)
    kdup_ref[:, 128:256] = jnp.where(low, kg_sw, kg).astype(BF16)
    vdup_ref[:, 0:128] = jnp.where(low, gv, 1.0).astype(BF16)
    vdup_ref[:, 128:256] = jnp.where(low, gv_sw, 1.0).astype(BF16)

    z = _dot(p[:, C_GATE:C_GATE + 128].astype(BF16), wg_ref[...]) + bg_ref[...]
    logsig = jnp.minimum(z, 0.0) - jnp.log1p(jnp.exp(-jnp.abs(z)))
    gla_ref[:, 0:128] = p[:, C_LQ:C_LQ + 128] * (GLA_DK ** -0.5)
    gla_ref[:, 128:256] = p[:, C_LK:C_LK + 128]
    gla_ref[:, 256:512] = logsig * (1.0 / GLA_GATE_NORM)
    gla_ref[:, 512:768] = p[:, C_LV:C_LV + 256]
    gla_ref[:, 768:1024] = p[:, C_LG:C_LG + 256]

    cache_ref[:, 0:128] = ckv
    cache_ref[:, 128:256] = kn
    cache_ref[:, 256:384] = gv
    cache_ref[:, 384:512] = kr4


def _pre_attn(x, y, pmod, mod, tab, modidx, tabidx, lw, n_ctx_tiles):
    N, D = x.shape
    tm = TOKEN_TILE
    nt = N // tm
    has_y = y is not None
    row = lambda i, mi, ti: (i, 0)
    const2 = lambda i, mi, ti: (0, 0)
    in_specs = [pl.BlockSpec((tm, D), row)]
    args = [x]
    if has_y:
        yk, gates = y
        in_specs += [pl.BlockSpec((tm * SUBLANES, LANES), functools.partial(lambda k, i, mi, ti: (k * nt + i, 0), k))
                     for k in range(TOP_K)]
        in_specs += [pl.BlockSpec((tm, LANES), row), pl.BlockSpec((1, 6, D), lambda i, mi, ti: (mi[i], 0, 0))]
        args += [yk] * TOP_K + [gates, pmod]
    in_specs += [
        pl.BlockSpec((1, 6, D), lambda i, mi, ti: (mi[i], 0, 0)),
        pl.BlockSpec((1, tm, 512), lambda i, mi, ti: (ti[i], 0, 0)),
        pl.BlockSpec((1, D), const2),
        pl.BlockSpec((D, IN_COLS), const2),
        pl.BlockSpec((1, Q_LORA), const2),
        pl.BlockSpec((Q_LORA, 1024), const2),
        pl.BlockSpec((1, KV_LORA), const2),
        pl.BlockSpec((1, 512), const2),
        pl.BlockSpec((1, 128), const2),
        pl.BlockSpec((512, 512), const2),
        pl.BlockSpec((128, 256), const2),
        pl.BlockSpec((1, 256), const2),
    ]
    args += [mod, tab, lw['g_attn'], lw['w_in_r'], lw['g_q_a'], lw['w_q2'], lw['g_kv_a'], lw['g_qh'], lw['g_kh'],
             lw['ones512'], lw['w_gate'], lw['b_gate']]
    out_shape, out_specs = [], []
    if has_y:
        out_shape.append(jax.ShapeDtypeStruct((N, D), F32))
        out_specs.append(pl.BlockSpec((tm, D), row))
    for width, dt in ((1024, BF16), (256, BF16), (512, BF16), (256, BF16), (256, BF16), (1024, F32)):
        out_shape.append(jax.ShapeDtypeStruct((N, width), dt))
        out_specs.append(pl.BlockSpec((tm, width), row))
    out_shape.append(jax.ShapeDtypeStruct(((n_ctx_tiles + 1) * tm, 512), F32))
    out_specs.append(pl.BlockSpec((tm, 512), lambda i, mi, ti: (jnp.minimum(i, n_ctx_tiles), 0)))
    outs = pl.pallas_call(
        functools.partial(_pre_attn_kernel, has_y=has_y),
        grid_spec=pltpu.PrefetchScalarGridSpec(num_scalar_prefetch=2, grid=(nt,), in_specs=in_specs,
                                               out_specs=out_specs),
        out_shape=out_shape,
        compiler_params=pltpu.CompilerParams(dimension_semantics=("arbitrary",), vmem_limit_bytes=VMEM_LIMIT),
        name="pre_attn",
    )(modidx, tabidx, *args)
    if not has_y:
        outs = [x] + list(outs)
    return outs


def _softmax_step(qs, k, m_ref, c):
    s = _dot_nt(qs, k)
    m_old = m_ref[c]
    m_new = jnp.maximum(m_old, jnp.broadcast_to(jnp.max(s, axis=-1, keepdims=True), m_old.shape))
    alpha = jnp.exp2(m_old - m_new)
    p = jnp.exp2(s - jnp.concatenate([m_new] * (k.shape[0] // LANES), axis=1))
    m_ref[c] = m_new
    return alpha, p


def _mla_attn_kernel(*refs, has_ctx, n_chunks, tk):
    if has_ctx:
        q_ref, kc_ref, k_ref, wuv_ref, o_ref, m_ref, l_ref, acc_ref = refs
    else:
        q_ref, k_ref, wuv_ref, o_ref, m_ref, l_ref, acc_ref = refs
        kc_ref = None
    tq = q_ref.shape[0]
    m_ref[...] = jnp.full(m_ref.shape, NEG_BIG, F32)
    l_ref[...] = jnp.zeros(l_ref.shape, F32)
    acc_ref[...] = jnp.zeros(acc_ref.shape, F32)
    q = q_ref[...]
    qs = [jnp.concatenate([q[:, h * 256:(h + 1) * 256] for h in (2 * c, 2 * c + 1)], axis=0) for c in range(2)]

    def process(kblk):
        for c in range(2):
            alpha, p = _softmax_step(qs[c], kblk, m_ref, c)
            l_ref[c] = alpha * l_ref[c] + jnp.broadcast_to(jnp.sum(p, axis=-1, keepdims=True), alpha.shape)
            acc_ref[c] = alpha * acc_ref[c] + _dot(p.astype(BF16), kblk[:, 0:KV_LORA])

    if has_ctx:
        process(kc_ref[0])

    def body(j, carry):
        process(k_ref[pl.ds(pl.multiple_of(j * tk, tk), tk), :])
        return carry

    lax.fori_loop(0, n_chunks, body, 0)
    out = jnp.zeros(o_ref.shape, F32)
    for h in range(MLA_HEADS):
        rows = slice((h % 2) * tq, (h % 2 + 1) * tq)
        o_lat = acc_ref[h // 2, rows, :] / l_ref[h // 2, rows, :]
        out = out + _dot(o_lat.astype(BF16), wuv_ref[h])
    o_ref[...] = out.astype(o_ref.dtype)


def _gqa_attn_kernel(*refs, has_ctx, n_chunks, tk):
    if has_ctx:
        q_ref, kc_ref, vc_ref, k_ref, v_ref, o_ref, m_ref, acc_ref = refs
    else:
        q_ref, k_ref, v_ref, o_ref, m_ref, acc_ref = refs
        kc_ref = vc_ref = None
    tq = q_ref.shape[0]
    m_ref[...] = jnp.full(m_ref.shape, NEG_BIG, F32)
    acc_ref[...] = jnp.zeros(acc_ref.shape, F32)
    q = q_ref[...]
    lane = lax.broadcasted_iota(I32, (tq, LANES), 1)
    low = lane < HEAD_DIM
    qs = []
    for g in range(GQA_KV_HEADS):
        parts = []
        for h in range(g * GQA_GROUP, (g + 1) * GQA_GROUP):
            pair = q[:, (h // 2) * LANES:(h // 2 + 1) * LANES]
            keep = low if h % 2 == 0 else jnp.logical_not(low)
            parts.append(jnp.where(keep, pair, jnp.zeros_like(pair)))
        qs.append(jnp.concatenate(parts, axis=0))

    def process(kblk, vblk):
        for g in range(GQA_KV_HEADS):
            alpha, p = _softmax_step(qs[g], kblk[:, g * LANES:(g + 1) * LANES], m_ref, g)
            acc_ref[g] = alpha * acc_ref[g] + _dot(p.astype(BF16), vblk[:, g * LANES:(g + 1) * LANES])

    if has_ctx:
        process(kc_ref[0], vc_ref[0])

    def body(j, carry):
        sl = pl.ds(pl.multiple_of(j * tk, tk), tk)
        process(k_ref[sl, :], v_ref[sl, :])
        return carry

    lax.fori_loop(0, n_chunks, body, 0)
    for j in range(GQA_HEADS // 2):
        outs = []
        for h in (2 * j, 2 * j + 1):
            a = acc_ref[h // GQA_GROUP, (h % GQA_GROUP) * tq:(h % GQA_GROUP + 1) * tq, :]
            outs.append(a / jnp.where(low, pltpu.roll(a, HEAD_DIM, 1), a))
        o_ref[:, j * LANES:(j + 1) * LANES] = jnp.where(low, outs[0], pltpu.roll(outs[1], HEAD_DIM, 1)).astype(
            o_ref.dtype)


def _attention(kind, q, k, v, ctx, wuv, *, n_seq, seq_len, tile_offset):
    has_ctx = ctx is not None
    tq = min(ATTN_TQ, seq_len)
    tk = min(ATTN_TK, seq_len)
    nq = seq_len // tq
    qw = q.shape[1]
    ow = 256 if kind == 'mla' else 512
    heads = MLA_HEADS if kind == 'mla' else GQA_HEADS
    qmap = lambda b, i: (tile_offset * nq + b * nq + i, 0)
    kmap = lambda b, i: (tile_offset + b, 0)
    cmap = lambda b, i: (b, 0, 0)
    in_specs = [pl.BlockSpec((tq, qw), qmap)]
    args = [q]
    if kind == 'mla':
        if has_ctx:
            in_specs.append(pl.BlockSpec((1,) + ctx[0].shape[1:], cmap))
            args.append(ctx[0])
        in_specs += [pl.BlockSpec((seq_len, 256), kmap), pl.BlockSpec((MLA_HEADS, 128, 256), lambda b, i: (0, 0, 0))]
        args += [k, wuv]
        body = _mla_attn_kernel
    else:
        if has_ctx:
            in_specs += [pl.BlockSpec((1,) + ctx[0].shape[1:], cmap), pl.BlockSpec((1,) + ctx[1].shape[1:], cmap)]
            args += [ctx[0], ctx[1]]
        in_specs += [pl.BlockSpec((seq_len, 256), kmap), pl.BlockSpec((seq_len, 256), kmap)]
        args += [k, v]
        body = _gqa_attn_kernel
    return pl.pallas_call(
        functools.partial(body, has_ctx=has_ctx, n_chunks=seq_len // tk, tk=tk),
        grid=(n_seq, nq),
        in_specs=in_specs,
        out_specs=pl.BlockSpec((tq, ow), lambda b, i: (b * nq + i, 0)),
        out_shape=jax.ShapeDtypeStruct((n_seq * seq_len, ow), BF16),
        scratch_shapes=[pltpu.VMEM((2, heads // 2 * tq, LANES), F32)] * (3 if kind == 'mla' else 2),
        compiler_params=pltpu.CompilerParams(dimension_semantics=("arbitrary", "arbitrary"),
                                             vmem_limit_bytes=VMEM_LIMIT),
        name=kind + ("_attn_latent" if has_ctx else "_attn_context"),
    )(*args)


def _gla_kernel(blk_ref, seq_ref, first_ref, *refs, reverse):
    del blk_ref, seq_ref
    if reverse:
        gin_ref, s0_ref, he_ref, of_ref, gout_ref, ones_ref, o_ref, sout_ref, st_ref = refs
    else:
        gin_ref, s0_ref, he_ref, o_ref, sout_ref, st_ref = refs
    i = pl.program_id(0)

    @pl.when(first_ref[i] == 1)
    def _():
        st_ref[...] = s0_ref[0]

    R, S = TOKEN_TILE, GLA_STEP
    ns = R // S
    q = gin_ref[:, 0:128]
    k = gin_ref[:, 128:256]
    g = gin_ref[:, 384:512] if reverse else gin_ref[:, 256:384]
    v = gin_ref[:, 512:768]
    pos = lax.broadcasted_iota(I32, (R, 128), 0) % S
    b = g
    for s in (1, 2, 4, 8):
        if reverse:
            b = b + jnp.where(pos < S - s, pltpu.roll(b, R - s, 0), 0.0)
        else:
            b = b + jnp.where(pos >= s, pltpu.roll(b, s, 0), 0.0)

    def step_row(a, j):
        w = a.shape[-1]
        a3 = a.reshape(ns, S, w)
        return jnp.broadcast_to(a3[:, j:j + 1, :], (ns, S, w)).reshape(R, w)

    b_edge = step_row(b, 0 if reverse else S - 1)
    qt = (q * jnp.exp(b)).astype(BF16)
    kt = (k * jnp.exp(b_edge - b)).astype(BF16)
    d_edge = jnp.exp(b_edge)
    vb = v.astype(BF16)
    he = he_ref[...]

    o_intra = jnp.zeros((R, 256), F32)
    for j in range(S):
        cond = (pos <= j) if reverse else (pos >= j)
        bj, kj, vj = step_row(b, j), step_row(k, j), step_row(v, j)
        t = jnp.where(cond, q * kj * jnp.exp(jnp.where(cond, b - bj, 0.0)), 0.0)
        o_intra = o_intra + _dot(t.astype(BF16), he) * vj

    own_head = (lax.broadcasted_iota(I32, (256, 128), 0) // GLA_DV) == (lax.broadcasted_iota(I32, (256, 128), 1) // GLA_DK)
    st = st_ref[...]
    parts = [None] * ns
    for t in (range(ns - 1, -1, -1) if reverse else range(ns)):
        rows = slice(t * S, (t + 1) * S)
        parts[t] = _dot_nt(qt[rows], st.astype(BF16)) + o_intra[rows]
        ut = _dot_tn(vb[rows], kt[rows])
        st = d_edge[t * S:t * S + 1, :] * st + jnp.where(own_head, ut, 0.0)
    st_ref[...] = st
    sout_ref[0] = st
    o = jnp.concatenate(parts, axis=0)
    if reverse:
        o = o + of_ref[...]
        ms = _split_dot(o * o, ones_ref[...]) * (1.0 / GLA_DV)
        on = o * lax.rsqrt(ms + EPS) * gout_ref[...]
        lg = gin_ref[:, 768:1024]
        o_ref[...] = (on * (lg * jax.nn.sigmoid(lg))).astype(o_ref.dtype)
    else:
        o_ref[...] = o


def _gla(gin, s0, o_f, lw, order, *, reverse):
    blk, seq, first = order
    N = gin.shape[0]
    tm = TOKEN_TILE
    n_seq = s0.shape[0]
    tile = lambda i, b, s, f: (b[i], 0)
    state = lambda i, b, s, f: (s[i], 0, 0)
    const2 = lambda i, b, s, f: (0, 0)
    in_specs = [pl.BlockSpec((tm, 1024), tile), pl.BlockSpec((1, 256, 128), state), pl.BlockSpec((128, 256), const2)]
    args = [gin, s0, lw['head_expand']]
    if reverse:
        in_specs += [pl.BlockSpec((tm, 256), tile), pl.BlockSpec((1, 256), const2), pl.BlockSpec((256, 256), const2)]
        args += [o_f, lw['g_gla_out'], lw['ones256']]
    return pl.pallas_call(
        functools.partial(_gla_kernel, reverse=reverse),
        grid_spec=pltpu.PrefetchScalarGridSpec(
            num_scalar_prefetch=3, grid=(N // tm,), in_specs=in_specs,
            out_specs=[pl.BlockSpec((tm, 256), tile), pl.BlockSpec((1, 256, 128), state)],
            scratch_shapes=[pltpu.VMEM((256, 128), F32)]),
        out_shape=[jax.ShapeDtypeStruct((N, 256), BF16 if reverse else F32),
                   jax.ShapeDtypeStruct((n_seq, 256, 128), F32)],
        compiler_params=pltpu.CompilerParams(dimension_semantics=("arbitrary",), vmem_limit_bytes=VMEM_LIMIT),
        name="gla_bwd" if reverse else "gla_fwd",
    )(blk, seq, first, *args)


def _post_attn_kernel(modidx_ref, x_ref, omc_ref, oml_ref, ogc_ref, ogl_ref, ol_ref, mod_ref, wout_ref, gffn_ref,
                      wrh_ref, wrl_ref, br_ref, x1_ref, h2_ref, ti_ref, tg_ref, *, n_ctx_tiles):
    del modidx_ref
    mod = mod_ref[0]
    gate1, shift2, scale2 = mod[2:3, :], mod[3:4, :], mod[4:5, :]
    is_ctx = pl.program_id(0) < n_ctx_tiles
    om = jnp.where(is_ctx, omc_ref[...], oml_ref[...])
    og = jnp.where(is_ctx, ogc_ref[...], ogl_ref[...])
    o = _dot(om, wout_ref[0:256, :]) + _dot(og, wout_ref[256:768, :]) + _dot(ol_ref[...], wout_ref[768:1024, :])
    x1 = x_ref[...] + gate1 * o
    x1_ref[...] = x1
    xn = x1 * lax.rsqrt(jnp.mean(x1 * x1, axis=-1, keepdims=True) + EPS) * gffn_ref[...]
    h2 = xn * (1.0 + scale2) + shift2
    _to_token_tiles(h2_ref, h2)
    hi = h2.astype(BF16)
    lo = (h2 - hi.astype(F32)).astype(BF16)
    wrh = wrh_ref[...]
    logits = _dot(hi, wrh) + _dot(hi, wrl_ref[...]) + _dot(lo, wrh) + br_ref[...]

    lane = lax.broadcasted_iota(I32, logits.shape, 1).astype(F32)
    work = logits
    vals, idxs = [], []
    for _ in range(TOP_K):
        m = jnp.max(work, axis=-1, keepdims=True)
        idx = jnp.min(jnp.where(work == m, lane, float(LANES)), axis=-1, keepdims=True)
        vals.append(m)
        idxs.append(idx)
        work = jnp.where(lane == idx, NEG_BIG * 4.0, work)
    es = [jnp.exp(vv - vals[0]) for vv in vals]
    denom = es[0] + es[1] + es[2] + es[3]
    ti = jnp.zeros(logits.shape, F32)
    tg = jnp.zeros(logits.shape, F32)
    for r in range(TOP_K):
        ti = jnp.where(lane == float(r), idxs[r], ti)
        tg = jnp.where(lane == float(r), es[r] / denom, tg)
    ti_ref[...] = ti.astype(I32)
    tg_ref[...] = tg


def _post_attn(x, o_mla, o_gqa, o_gla, mod, modidx, lw, n_ctx_tiles):
    N, D = x.shape
    tm = TOKEN_TILE
    row = lambda i, mi: (i, 0)
    ctx_row = lambda i, mi: (jnp.minimum(i, n_ctx_tiles - 1), 0)
    lat_row = lambda i, mi: (jnp.maximum(i - n_ctx_tiles, 0), 0)
    const2 = lambda i, mi: (0, 0)
    return pl.pallas_call(
        functools.partial(_post_attn_kernel, n_ctx_tiles=n_ctx_tiles),
        grid_spec=pltpu.PrefetchScalarGridSpec(
            num_scalar_prefetch=1, grid=(N // tm,),
            in_specs=[pl.BlockSpec((tm, D), row), pl.BlockSpec((tm, 256), ctx_row), pl.BlockSpec((tm, 256), lat_row),
                      pl.BlockSpec((tm, 512), ctx_row), pl.BlockSpec((tm, 512), lat_row),
                      pl.BlockSpec((tm, 256), row), pl.BlockSpec((1, 6, D), lambda i, mi: (mi[i], 0, 0)),
                      pl.BlockSpec((D, D), const2), pl.BlockSpec((1, D), const2),
                      pl.BlockSpec((D, LANES), const2), pl.BlockSpec((D, LANES), const2),
                      pl.BlockSpec((1, LANES), const2)],
            out_specs=[pl.BlockSpec((tm, D), row), pl.BlockSpec((tm * SUBLANES, LANES), row),
                       pl.BlockSpec((tm, LANES), row), pl.BlockSpec((tm, LANES), row)]),
        out_shape=[jax.ShapeDtypeStruct((N, D), F32), jax.ShapeDtypeStruct((N * SUBLANES, LANES), F32),
                   jax.ShapeDtypeStruct((N, LANES), I32), jax.ShapeDtypeStruct((N, LANES), F32)],
        compiler_params=pltpu.CompilerParams(dimension_semantics=("arbitrary",), vmem_limit_bytes=VMEM_LIMIT),
        name="post_attn",
    )(modidx, x, o_mla[0], o_mla[1], o_gqa[0], o_gqa[1], o_gla, mod, lw['w_out'], lw['g_ffn'], lw['w_router_hi'],
      lw['w_router_lo'], lw['b_router'])


def _moe_kernel(be_ref, first_ref, src0_ref, src_ref, dst_ref, dstp_ref, h_hbm, wgu_ref, bgu_ref, wd_ref, bd_ref,
                y_hbm, wgu_bf, wd_bf, xbuf, ybuf, xsem, ysem):
    del be_ref
    i = pl.program_id(0)
    slot = i % 2
    other = 1 - slot
    M = MOE_ROWS

    def gather(idx_ref, r, s):
        tok = idx_ref[0, 0, r]
        return pltpu.make_async_copy(h_hbm.at[pl.ds(pl.multiple_of(tok * SUBLANES, SUBLANES), SUBLANES), :],
                                     xbuf.at[s, pl.ds(r * SUBLANES, SUBLANES), :], xsem.at[s])

    def scatter(idx_ref, r, s):
        row = idx_ref[0, 0, r]
        return pltpu.make_async_copy(ybuf.at[s, pl.ds(r * SUBLANES, SUBLANES), :],
                                     y_hbm.at[pl.ds(pl.multiple_of(row * SUBLANES, SUBLANES), SUBLANES), :],
                                     ysem.at[s])

    def all_rows_in(s):
        return pltpu.make_async_copy(h_hbm.at[pl.ds(0, M * SUBLANES), :], xbuf.at[s], xsem.at[s])

    def all_rows_out(s):
        return pltpu.make_async_copy(ybuf.at[s], y_hbm.at[pl.ds(0, M * SUBLANES), :], ysem.at[s])

    @pl.when(i == 0)
    def _():
        ybuf[...] = jnp.zeros(ybuf.shape, F32)

        def prime(r, carry):
            gather(src0_ref, r, 0).start()
            scatter(dstp_ref, r, 0).start()
            return carry
        lax.fori_loop(0, M, prime, 0)

    @pl.when(first_ref[i] == 1)
    def _():
        wgu_bf[...] = wgu_ref[...].astype(BF16)
        wd_bf[...] = wd_ref[...].astype(BF16)

    all_rows_in(slot).wait()
    for r in range(M):
        scatter(dst_ref, r, other).start()
        gather(src_ref, r, other).start()
    xb = xbuf.at[slot]
    x = jnp.concatenate([xb[pl.ds(c, M, stride=SUBLANES), :] for c in range(D_MODEL // LANES)], axis=1)
    gu = _dot(x.astype(BF16), wgu_bf[...]) + bgu_ref[...]
    gate = jnp.minimum(gu[:, :D_EXPERT], SWIGLU_LIMIT)
    up = jnp.clip(gu[:, D_EXPERT:], -SWIGLU_LIMIT, SWIGLU_LIMIT)
    act = gate * jax.nn.sigmoid(SWIGLU_ALPHA * gate) * (up + 1.0)
    y = _dot(act.astype(BF16), wd_bf[...]) + bd_ref[...]
    all_rows_out(slot).wait()
    yb = ybuf.at[slot]
    for c in range(D_MODEL // LANES):
        yb[pl.ds(c, M, stride=SUBLANES), :] = y[:, c * LANES:(c + 1) * LANES]

    @pl.when(i == pl.num_programs(0) - 1)
    def _():
        all_rows_out(other).wait()
        all_rows_in(other).wait()


def _moe_experts(h2t, sched, w_gate_up, b_gate_up, w_down, b_down, layer, n_out_rows):
    block_e, first, src, dst, dst_prime = sched
    n_blocks = block_e.shape[0]
    D = D_MODEL
    M = MOE_ROWS
    L, E = w_gate_up.shape[:2]
    wmap = lambda i, be, fi: (layer, be[i], 0, 0)
    smem = lambda imap: pl.BlockSpec((1, 1, M), imap, memory_space=pltpu.SMEM)
    return pl.pallas_call(
        _moe_kernel,
        grid_spec=pltpu.PrefetchScalarGridSpec(
            num_scalar_prefetch=2, grid=(n_blocks,),
            in_specs=[smem(lambda i, be, fi: (0, 0, 0)), smem(lambda i, be, fi: (i + 1, 0, 0)),
                      smem(lambda i, be, fi: (i, 0, 0)), smem(lambda i, be, fi: (0, 0, 0)),
                      pl.BlockSpec(memory_space=pl.ANY),
                      pl.BlockSpec((None, None, D, 2 * D_EXPERT), wmap),
                      pl.BlockSpec((None, None, 1, 2 * D_EXPERT), wmap),
                      pl.BlockSpec((None, None, D_EXPERT, D), wmap),
                      pl.BlockSpec((None, None, 1, D), wmap)],
            out_specs=pl.BlockSpec(memory_space=pl.ANY),
            scratch_shapes=[pltpu.VMEM((D, 2 * D_EXPERT), BF16), pltpu.VMEM((D_EXPERT, D), BF16),
                            pltpu.VMEM((2, M * SUBLANES, LANES), F32), pltpu.VMEM((2, M * SUBLANES, LANES), F32),
                            pltpu.SemaphoreType.DMA((2,)), pltpu.SemaphoreType.DMA((2,))]),
        out_shape=jax.ShapeDtypeStruct((n_out_rows * SUBLANES, LANES), F32),
        compiler_params=pltpu.CompilerParams(dimension_semantics=("arbitrary",), vmem_limit_bytes=VMEM_LIMIT),
        name="moe_experts",
    )(block_e, first, src, src, dst, dst_prime, h2t, w_gate_up, b_gate_up.reshape(L, E, 1, 2 * D_EXPERT), w_down,
      b_down.reshape(L, E, 1, D))


def _route(top_idx, n_blocks):
    N = top_idx.shape[0]
    onehot = (top_idx[:, :, None] == jnp.arange(N_EXPERTS, dtype=I32)[None, None, :]).astype(I32).sum(axis=1)
    before = jnp.cumsum(onehot, axis=0) - onehot
    counts = onehot.sum(axis=0)
    padded = (counts + MOE_ROWS - 1) // MOE_ROWS * MOE_ROWS
    pad_end = jnp.cumsum(padded)
    pad_start = pad_end - padded
    dest = pad_start[top_idx] + jnp.take_along_axis(before, top_idx, axis=1)
    M, K = MOE_ROWS, TOP_K
    asg = jnp.arange(N * K, dtype=I32)
    slot_asg = jnp.full((n_blocks * M,), -1, I32).at[dest.reshape(-1)].set(asg)
    real = slot_asg >= 0
    tok = jnp.where(real, slot_asg // K, 0)
    slot_id = jnp.arange(n_blocks * M, dtype=I32)
    spare = K * N + ((slot_id // M) % 2) * M + slot_id % M
    out_row = jnp.where(real, (slot_asg % K) * N + tok, spare)
    src = jnp.concatenate([tok.reshape(n_blocks, 1, M), jnp.zeros((1, 1, M), I32)], axis=0)
    lead = (K * N + M + jnp.arange(M, dtype=I32)).reshape(1, 1, M)
    dst = jnp.concatenate([lead, out_row.reshape(n_blocks, 1, M)], axis=0)
    dst_prime = (K * N + jnp.arange(M, dtype=I32)).reshape(1, 1, M)
    blk_start = jnp.arange(n_blocks, dtype=I32) * M
    valid = blk_start < pad_end[-1]
    block_e = jnp.minimum(jnp.searchsorted(pad_end, blk_start, side='right'), N_EXPERTS - 1).astype(I32)
    n_valid = pad_end[-1] // M
    block_e = block_e[jnp.minimum(jnp.arange(n_blocks, dtype=I32), n_valid - 1)]
    prev_e = jnp.concatenate([jnp.full((1,), -1, I32), block_e[:-1]])
    first = (valid & (block_e != prev_e)).astype(I32)
    return (block_e, first, src, dst, dst_prime)


def _final_kernel(modidx_ref, x_ref, *refs):
    del modidx_ref
    y_refs, (tg_ref, pmod_ref, g_ref, o_ref) = refs[:TOP_K], refs[TOP_K:]
    x = x_ref[...] + pmod_ref[0, 5:6, :] * _combine_experts(y_refs, tg_ref)
    o_ref[...] = x * lax.rsqrt(jnp.mean(x * x, axis=-1, keepdims=True) + EPS) * g_ref[...]


def _final_norm(x, y, pmod, modidx, g_final):
    N, D = x.shape
    tm = TOKEN_TILE
    row = lambda i, mi: (i, 0)
    return pl.pallas_call(
        _final_kernel,
        grid_spec=pltpu.PrefetchScalarGridSpec(
            num_scalar_prefetch=1, grid=(N // tm,),
            in_specs=[pl.BlockSpec((tm, D), row)]
            + [pl.BlockSpec((tm * SUBLANES, LANES), functools.partial(lambda k, i, mi: (k * (N // tm) + i, 0), k))
               for k in range(TOP_K)]
            + [pl.BlockSpec((tm, LANES), row),
               pl.BlockSpec((1, 6, D), lambda i, mi: (mi[i], 0, 0)), pl.BlockSpec((1, D), lambda i, mi: (0, 0))],
            out_specs=pl.BlockSpec((tm, D), row)),
        out_shape=jax.ShapeDtypeStruct((N, D), F32),
        compiler_params=pltpu.CompilerParams(dimension_semantics=("arbitrary",), vmem_limit_bytes=VMEM_LIMIT),
        name="final_norm",
    )(modidx, x, *([y[0]] * TOP_K), y[1], pmod, g_final)


def _rope_tables(seq_len):
    pos = np.arange(seq_len)
    rowp = (pos // GRID_W).astype(np.float32)
    colp = (pos % GRID_W).astype(np.float32)

    def tables(rdim, copies):
        quarter = rdim // 4
        inv_freq = jnp.asarray(ROPE_THETA, F32) ** (-jnp.arange(quarter, dtype=F32) / quarter)
        ar = jnp.asarray(rowp)[:, None] * inv_freq[None, :]
        ac = jnp.asarray(colp)[:, None] * inv_freq[None, :]
        cos = jnp.concatenate([jnp.cos(ar), jnp.cos(ar), jnp.cos(ac), jnp.cos(ac)], axis=1)
        sin = jnp.concatenate([-jnp.sin(ar), jnp.sin(ar), -jnp.sin(ac), jnp.sin(ac)], axis=1)
        return jnp.tile(cos, (1, copies)), jnp.tile(sin, (1, copies))

    c64, s64 = tables(HEAD_DIM, 2)
    c32, s32 = tables(MLA_ROPE, 4)
    lat = jnp.concatenate([c64, s64, c32, s32], axis=1).reshape(seq_len // TOKEN_TILE, TOKEN_TILE, 512)
    ident = jnp.concatenate([jnp.ones((TOKEN_TILE, 128), F32), jnp.zeros((TOKEN_TILE, 128), F32)] * 2, axis=1)
    return jnp.concatenate([ident[None], lat], axis=0)


def _block_ones(n, blk):
    r = np.arange(n) // blk
    return jnp.asarray((r[:, None] == r[None, :]).astype(np.float32), BF16)


def kernel(x_prompt, x_sample, cache_mla_ckv, cache_mla_krope, cache_gqa_k, cache_gqa_v, state_gla, c, c_ctx, w_mod, b_mod, g_attn_norm, g_ffn_norm, w_in, g_q_a, w_uq, g_kv_a, w_ukv, g_q_head, g_k_head, w_gk_fwd, b_gk_fwd, w_gk_bwd, b_gk_bwd, g_gla_out, w_out, w_router, b_router, w_gate_up, b_gate_up, w_down, b_down, g_final):
    B, T, D = x_prompt.shape
    BD, TD, _ = x_sample.shape
    L = w_in.shape[0]
    P = cache_mla_ckv.shape[2]
    tm = TOKEN_TILE
    n_ctx = B * T
    N = n_ctx + BD * TD
    assert T == tm and TD % ATTN_TQ == 0 and n_ctx % TD == 0 and BD + 1 <= 16
    n_ctx_tiles = n_ctx // tm
    nt = N // tm
    tiles_per_lat = TD // tm

    tile_ids = np.arange(nt)
    lat_tile = np.maximum(tile_ids - n_ctx_tiles, 0)
    is_lat = tile_ids >= n_ctx_tiles
    modidx = jnp.asarray(np.where(is_lat, 1 + lat_tile // tiles_per_lat, 0), I32)
    tabidx = jnp.asarray(np.where(is_lat, 1 + lat_tile % tiles_per_lat, 0), I32)
    seq_of_tile = np.where(is_lat, B + lat_tile // tiles_per_lat, tile_ids)
    first_fwd = np.where(is_lat, lat_tile % tiles_per_lat == 0, True)
    last_fwd = np.where(is_lat, lat_tile % tiles_per_lat == tiles_per_lat - 1, True)
    order_fwd = (jnp.asarray(tile_ids, I32), jnp.asarray(seq_of_tile, I32), jnp.asarray(first_fwd, I32))
    rev = tile_ids[::-1]
    order_bwd = (jnp.asarray(rev, I32), jnp.asarray(seq_of_tile[rev], I32), jnp.asarray(last_fwd[rev], I32))
    n_seq = B + BD

    cvecs = jnp.zeros((16, D), F32).at[0].set(c_ctx).at[1:1 + BD].set(c)
    mods = _modulation(cvecs, w_mod, b_mod).reshape(L, 16, 6, D)

    o = np.cumsum([0, Q_LORA, KV_LORA, MLA_ROPE, 512, 128, 128, 128, 128, 256, 16, 16, 256])
    seg = lambda j: w_in[:, :, o[j]:o[j + 1]]
    w_in_r = jnp.concatenate(
        [seg(0), seg(1), seg(3), seg(4), seg(5), seg(6), seg(7), seg(8), seg(11), seg(2), seg(2), seg(2), seg(2),
         seg(9), seg(10), jnp.zeros((L, D, 128 - 2 * GLA_GATE_RANK), F32)], axis=-1).astype(BF16)
    uq = w_uq.reshape(L, Q_LORA, MLA_HEADS, MLA_QK)
    ukv = w_ukv.reshape(L, KV_LORA, MLA_HEADS, MLA_NOPE + MLA_V)
    w_comb = _fold_q_weights(uq[..., :MLA_NOPE].transpose(0, 2, 1, 3), ukv[..., :MLA_NOPE].transpose(0, 2, 1, 3))
    q_pad = jnp.zeros((L, Q_LORA, 256 - KV_LORA - MLA_ROPE), F32)
    w_q2 = (jnp.concatenate([part for h in range(MLA_HEADS)
                             for part in (w_comb[:, h], uq[:, :, h, MLA_NOPE:], q_pad)], axis=-1)
            * (MLA_QK ** -0.5 * LOG2E)).astype(BF16)
    w_uv = ukv[..., MLA_NOPE:].transpose(0, 2, 1, 3)
    eye_h = jnp.eye(MLA_HEADS, dtype=F32)
    wuv_pad = jnp.einsum('lhkv,hg->lhkgv', w_uv, eye_h).reshape(L, MLA_HEADS, KV_LORA, MLA_HEADS * MLA_V).astype(BF16)
    w_gate = jnp.zeros((L, 128, 256), F32).at[:, 0:16, 0:128].set(w_gk_fwd).at[:, 16:32, 128:256].set(w_gk_bwd)
    w_gate = w_gate.astype(BF16)
    b_gate = jnp.concatenate([b_gk_fwd, b_gk_bwd], axis=-1)
    w_out_bf = w_out.astype(BF16)
    wr_pad = jnp.pad(w_router, ((0, 0), (0, 0), (0, LANES - N_EXPERTS)))
    wr_hi = wr_pad.astype(BF16)
    wr_lo = (wr_pad - wr_hi.astype(F32)).astype(BF16)
    br_pad = jnp.pad(b_router, ((0, 0), (0, LANES - N_EXPERTS)), constant_values=NEG_BIG)
    ones512 = _block_ones(512, HEAD_DIM)
    head_expand = jnp.asarray((np.arange(128)[:, None] // GLA_DK == np.arange(256)[None, :] // GLA_DV)
                              .astype(np.float32), BF16)
    tab = _rope_tables(TD)

    x = jnp.concatenate([x_prompt.reshape(n_ctx, D), x_sample.reshape(BD * TD, D)], axis=0)
    y = None
    n_blocks = (N * TOP_K) // MOE_ROWS + N_EXPERTS
    caches, states = [], []
    for l in range(L):
        lw = dict(g_attn=g_attn_norm[l][None], w_in_r=w_in_r[l], g_q_a=g_q_a[l][None], w_q2=w_q2[l],
                  g_kv_a=g_kv_a[l][None], g_qh=jnp.tile(g_q_head[l], GQA_HEADS)[None],
                  g_kh=jnp.tile(g_k_head[l], GQA_KV_HEADS)[None], ones512=ones512, w_gate=w_gate[l],
                  b_gate=b_gate[l][None], head_expand=head_expand, g_gla_out=jnp.tile(g_gla_out[l], GLA_HEADS)[None],
                  ones256=ones512[:256, :256], w_out=w_out_bf[l], g_ffn=g_ffn_norm[l][None],
                  w_router_hi=wr_hi[l], w_router_lo=wr_lo[l], b_router=br_pad[l][None])
        pmod = mods[l - 1] if l > 0 else None
        x, qm, kvc, qg, kdup, vdup, gin, cache = _pre_attn(x, y, pmod, mods[l], tab, modidx, tabidx, lw, n_ctx_tiles)
        caches.append(cache[:n_ctx])

        mla_ctx = jnp.concatenate([cache_mla_ckv[:, l]] + [cache_mla_krope[:, l]] * 4, axis=-1).astype(BF16)
        ck = cache_gqa_k[:, l]
        cv = cache_gqa_v[:, l]
        k_ctx = jnp.concatenate([ck[:, :, 0], ck[:, :, 0], ck[:, :, 1], ck[:, :, 1]], axis=-1).astype(BF16)
        ones_v = jnp.ones(cv.shape[:2] + (HEAD_DIM,), F32)
        v_ctx = jnp.concatenate([cv[:, :, 0], ones_v, cv[:, :, 1], ones_v], axis=-1).astype(BF16)
        o_mla = (_attention('mla', qm, kvc, None, None, wuv_pad[l], n_seq=B, seq_len=T, tile_offset=0),
                 _attention('mla', qm, kvc, None, (mla_ctx,), wuv_pad[l], n_seq=BD, seq_len=TD,
                            tile_offset=n_ctx // TD))
        o_gqa = (_attention('gqa', qg, kdup, vdup, None, None, n_seq=B, seq_len=T, tile_offset=0),
                 _attention('gqa', qg, kdup, vdup, (k_ctx, v_ctx), None, n_seq=BD, seq_len=TD,
                            tile_offset=n_ctx // TD))

        eye_g = jnp.eye(GLA_HEADS, dtype=F32)
        st_lat = jnp.einsum('bshde,hg->bshegd', state_gla[:, l], eye_g).reshape(BD, 2, 256, 128)
        zeros_ctx = jnp.zeros((B, 256, 128), F32)
        o_f, s_f = _gla(gin, jnp.concatenate([zeros_ctx, st_lat[:, 0]], axis=0), None, lw, order_fwd, reverse=False)
        o_gla, s_b = _gla(gin, jnp.concatenate([zeros_ctx, st_lat[:, 1]], axis=0), o_f, lw, order_bwd, reverse=True)
        states.append((s_f[:B], s_b[:B]))

        x1, h2, ti, tg = _post_attn(x, o_mla, o_gqa, o_gla, mods[l], modidx, lw, n_ctx_tiles)
        sched = _route(ti[:, :TOP_K], n_blocks)
        y = (_moe_experts(h2, sched, w_gate_up, b_gate_up, w_down, b_down, l, TOP_K * N + 2 * MOE_ROWS), tg)
        x = x1

    out = _final_norm(x, y, mods[L - 1], modidx, g_final[None])
    y_prompt = out[:n_ctx].reshape(B, T, D)
    y_sample = out[n_ctx:].reshape(BD, TD, D)
    cache_all = jnp.stack([cc.reshape(B, T, 512) for cc in caches], axis=1)
    new_ckv = cache_all[..., 0:128]
    new_k = cache_all[..., 128:256].reshape(B, L, T, GQA_KV_HEADS, HEAD_DIM)
    new_v = cache_all[..., 256:384].reshape(B, L, T, GQA_KV_HEADS, HEAD_DIM)
    new_krope = cache_all[..., 384:384 + MLA_ROPE]

    def unpack_state(st):
        s5 = st.reshape(B, GLA_HEADS, GLA_DV, GLA_HEADS, GLA_DK)
        diag = jnp.stack([s5[:, h, :, h, :] for h in range(GLA_HEADS)], axis=1)
        return diag.transpose(0, 1, 3, 2)

    new_state = jnp.stack([jnp.stack([unpack_state(sf), unpack_state(sb)], axis=1) for sf, sb in states], axis=1)
    return (y_prompt, y_sample, new_ckv, new_krope, new_k, new_v, new_state)
```

```python
import functools

import jax
import jax.numpy as jnp
import numpy as np
from jax import lax
from jax.experimental import pallas as pl
from jax.experimental.pallas import tpu as pltpu

F32 = jnp.float32
BF16 = jnp.bfloat16
I32 = jnp.int32

D_MODEL = 1024
GRID_W = 64
ROPE_THETA = 10000.0
EPS = 1e-6
MLA_HEADS = 4
Q_LORA = 256
KV_LORA = 128
MLA_NOPE = 64
MLA_ROPE = 32
MLA_V = 64
MLA_QK = MLA_NOPE + MLA_ROPE
GQA_HEADS = 8
GQA_KV_HEADS = 2
GQA_GROUP = GQA_HEADS // GQA_KV_HEADS
HEAD_DIM = 64
GLA_HEADS = 4
GLA_DK = 32
GLA_DV = 64
GLA_GATE_RANK = 16
GLA_GATE_NORM = 16.0
N_EXPERTS = 32
TOP_K = 4
D_EXPERT = D_MODEL
SWIGLU_LIMIT = 7.0
SWIGLU_ALPHA = 1.702

LANES = 128
SUBLANES = 8
TOKEN_TILE = 256
GLA_STEP = 16
MOE_ROWS = 512
ATTN_TQ = 512
ATTN_TK = 512
VMEM_LIMIT = 56 * 1024 * 1024
NEG_BIG = -1e30
LOG2E = 1.4426950408889634

C_CQ, C_CKV, C_GQ, C_GK, C_GV = 0, 256, 384, 896, 1024
C_LQ, C_LK, C_LV, C_LG, C_KR, C_GATE = 1152, 1280, 1408, 1664, 1920, 2048
IN_COLS = 2176


def _dot(a, b):
    return jnp.dot(a, b, preferred_element_type=F32)


def _dot_nt(a, b):
    return lax.dot_general(a, b, (((1,), (1,)), ((), ())), preferred_element_type=F32)


def _dot_tn(a, b):
    return lax.dot_general(a, b, (((0,), (0,)), ((), ())), preferred_element_type=F32)


def _split_dot(x, ones_bf):
    hi = x.astype(BF16)
    lo = (x - hi.astype(F32)).astype(BF16)
    return _dot(hi, ones_bf) + _dot(lo, ones_bf)


def _rope(x, cos, sin_signed, quarter):
    width = x.shape[-1]
    lane = lax.broadcasted_iota(I32, x.shape, 1)
    first = (lane % (2 * quarter)) < quarter
    partner = jnp.where(first, pltpu.roll(x, width - quarter, 1), pltpu.roll(x, quarter, 1))
    return x * cos + partner * sin_signed


def _mod_kernel(c_ref, w_ref, b_ref, o_ref):
    c = c_ref[...]
    a = c * jax.nn.sigmoid(c)
    o_ref[0] = jnp.dot(a, w_ref[0], precision=lax.Precision.HIGHEST, preferred_element_type=F32) + b_ref[0]


def _modulation(cvecs, w_mod, b_mod):
    L, D, D6 = w_mod.shape
    tn = 1536
    return pl.pallas_call(
        _mod_kernel,
        grid=(L, D6 // tn),
        in_specs=[pl.BlockSpec((16, D), lambda l, j: (0, 0)),
                  pl.BlockSpec((1, D, tn), lambda l, j: (l, 0, j)),
                  pl.BlockSpec((1, 1, tn), lambda l, j: (l, 0, j))],
        out_specs=pl.BlockSpec((1, 16, tn), lambda l, j: (l, 0, j)),
        out_shape=jax.ShapeDtypeStruct((L, 16, D6), F32),
        compiler_params=pltpu.CompilerParams(dimension_semantics=("arbitrary", "arbitrary"),
                                             vmem_limit_bytes=VMEM_LIMIT),
        name="modulation",
    )(cvecs, w_mod, b_mod.reshape(L, 1, D6))


def _fold_kernel(a_ref, b_ref, o_ref):
    o_ref[0, 0] = lax.dot_general(a_ref[0, 0], b_ref[0, 0], (((1,), (1,)), ((), ())),
                                  precision=lax.Precision.HIGHEST, preferred_element_type=F32)


def _fold_q_weights(wq_nope, wk_nope):
    L, H, A, K = wq_nope.shape
    B = wk_nope.shape[2]
    return pl.pallas_call(
        _fold_kernel,
        grid=(L, H),
        in_specs=[pl.BlockSpec((1, 1, A, K), lambda l, h: (l, h, 0, 0)),
                  pl.BlockSpec((1, 1, B, K), lambda l, h: (l, h, 0, 0))],
        out_specs=pl.BlockSpec((1, 1, A, B), lambda l, h: (l, h, 0, 0)),
        out_shape=jax.ShapeDtypeStruct((L, H, A, B), F32),
        name="fold_q_weights",
    )(wq_nope, wk_nope)


def _from_token_tiles(ref, rows):
    return jnp.concatenate([ref[pl.ds(c, rows, stride=SUBLANES), :] for c in range(D_MODEL // LANES)], axis=1)


def _to_token_tiles(ref, x):
    for c in range(D_MODEL // LANES):
        ref[pl.ds(c, x.shape[0], stride=SUBLANES), :] = x[:, c * LANES:(c + 1) * LANES]


def _combine_experts(y_refs, tg_ref):
    tg = tg_ref[...]
    acc = None
    for k in range(TOP_K):
        term = tg[:, k:k + 1] * _from_token_tiles(y_refs[k], tg.shape[0])
        acc = term if acc is None else acc + term
    return acc


def _pre_attn_kernel(modidx_ref, tabidx_ref, *refs, has_y):
    del modidx_ref, tabidx_ref
    if has_y:
        x_ref, y_ref, tg_ref, pmod_ref = refs[0], refs[1:1 + TOP_K], refs[1 + TOP_K], refs[2 + TOP_K]
        refs = refs[3 + TOP_K:]
    else:
        x_ref = refs[0]
        refs = refs[1:]
    (mod_ref, tab_ref, gattn_ref, win_ref, gqa_ref, wq2_ref, gkva_ref, gqh_ref, gkh_ref, ones_ref,
     wg_ref, bg_ref) = refs[:12]
    outs = refs[12:]
    if has_y:
        xres_ref = outs[0]
        outs = outs[1:]
    qm_ref, kvc_ref, qg_ref, kdup_ref, vdup_ref, gla_ref, cache_ref = outs

    x = x_ref[...]
    if has_y:
        x = x + pmod_ref[0, 5:6, :] * _combine_experts(y_ref, tg_ref)
        xres_ref[...] = x
    mod = mod_ref[0]
    shift1, scale1 = mod[0:1, :], mod[1:2, :]
    xn = x * lax.rsqrt(jnp.mean(x * x, axis=-1, keepdims=True) + EPS) * gattn_ref[...]
    h = xn * (1.0 + scale1) + shift1
    p = _dot(h.astype(BF16), win_ref[...])

    tab = tab_ref[0]
    cos64, sin64 = tab[:, 0:128], tab[:, 128:256]
    cos32, sin32 = tab[:, 256:384], tab[:, 384:512]
    ones = ones_ref[...]

    cq = p[:, C_CQ:C_CQ + Q_LORA]
    cqn = cq * lax.rsqrt(jnp.mean(cq * cq, axis=-1, keepdims=True) + EPS) * gqa_ref[...]
    qm = _dot(cqn.astype(BF16), wq2_ref[...])
    q_rope = _rope(jnp.concatenate([qm[:, h * 256 + 128:(h + 1) * 256] for h in range(MLA_HEADS)], axis=1),
                   jnp.concatenate([cos32] * MLA_HEADS, axis=1), jnp.concatenate([sin32] * MLA_HEADS, axis=1),
                   MLA_ROPE // 4)
    for h in range(MLA_HEADS):
        qm_ref[:, h * 256:h * 256 + 128] = qm[:, h * 256:h * 256 + 128].astype(BF16)
        qm_ref[:, h * 256 + 128:(h + 1) * 256] = q_rope[:, h * 128:(h + 1) * 128].astype(BF16)
    ckv_raw = p[:, C_CKV:C_CKV + KV_LORA]
    ckv = ckv_raw * lax.rsqrt(jnp.mean(ckv_raw * ckv_raw, axis=-1, keepdims=True) + EPS) * gkva_ref[...]
    kr4 = p[:, C_KR:C_KR + 128]
    kr4_rot = _rope(kr4, cos32, sin32, MLA_ROPE // 4)
    kvc_ref[:, 0:128] = ckv.astype(BF16)
    kvc_ref[:, 128:256] = kr4_rot.astype(BF16)

    gq = p[:, C_GQ:C_GQ + 512]
    ms_q = _split_dot(gq * gq, ones) * (1.0 / HEAD_DIM)
    qn = gq * lax.rsqrt(ms_q + EPS) * gqh_ref[...]
    cos_q = jnp.concatenate([cos64] * 4, axis=1)
    sin_q = jnp.concatenate([sin64] * 4, axis=1)
    qg = _rope(qn, cos_q, sin_q, HEAD_DIM // 4) * (HEAD_DIM ** -0.5 * LOG2E)
    qg_ref[...] = qg.astype(BF16)
    gk = p[:, C_GK:C_GK + 128]
    ms_k = _split_dot(gk * gk, ones[0:128, 0:128]) * (1.0 / HEAD_DIM)
    kn = gk * lax.rsqrt(ms_k + EPS) * gkh_ref[...]
    kg = _rope(kn, cos64, sin64, HEAD_DIM // 4)
    gv = p[:, C_GV:C_GV + 128]
    lane = lax.broadcasted_iota(I32, kg.shape, 1)
    low = lane < HEAD_DIM
    kg_sw = pltpu.roll(kg, HEAD_DIM, 1)
    gv_sw = pltpu.roll(gv, HEAD_DIM, 1)
    kdup_ref[:, 0:128] = jnp.where(low, kg, kg_sw).astype(BF16)
    kdup_ref[:, 128:256] = jnp.where(low, kg_sw, kg).astype(BF16)
    vdup_ref[:, 0:128] = jnp.where(low, gv, 1.0).astype(BF16)
    vdup_ref[:, 128:256] = jnp.where(low, gv_sw, 1.0).astype(BF16)

    z = _dot(p[:, C_GATE:C_GATE + 128].astype(BF16), wg_ref[...]) + bg_ref[...]
    logsig = jnp.minimum(z, 0.0) - jnp.log1p(jnp.exp(-jnp.abs(z)))
    gla_ref[:, 0:128] = p[:, C_LQ:C_LQ + 128] * (GLA_DK ** -0.5)
    gla_ref[:, 128:256] = p[:, C_LK:C_LK + 128]
    gla_ref[:, 256:512] = logsig * (1.0 / GLA_GATE_NORM)
    gla_ref[:, 512:768] = p[:, C_LV:C_LV + 256]
    gla_ref[:, 768:1024] = p[:, C_LG:C_LG + 256]

    cache_ref[:, 0:128] = ckv
    cache_ref[:, 128:256] = kn
    cache_ref[:, 256:384] = gv
    cache_ref[:, 384:512] = kr4


def _pre_attn(x, y, pmod, mod, tab, modidx, tabidx, lw, n_ctx_tiles):
    N, D = x.shape
    tm = TOKEN_TILE
    nt = N // tm
    has_y = y is not None
    row = lambda i, mi, ti: (i, 0)
    const2 = lambda i, mi, ti: (0, 0)
    in_specs = [pl.BlockSpec((tm, D), row)]
    args = [x]
    if has_y:
        yk, gates = y
        in_specs += [pl.BlockSpec((tm * SUBLANES, LANES), functools.partial(lambda k, i, mi, ti: (k * nt + i, 0), k))
                     for k in range(TOP_K)]
        in_specs += [pl.BlockSpec((tm, LANES), row), pl.BlockSpec((1, 6, D), lambda i, mi, ti: (mi[i], 0, 0))]
        args += [yk] * TOP_K + [gates, pmod]
    in_specs += [
        pl.BlockSpec((1, 6, D), lambda i, mi, ti: (mi[i], 0, 0)),
        pl.BlockSpec((1, tm, 512), lambda i, mi, ti: (ti[i], 0, 0)),
        pl.BlockSpec((1, D), const2),
        pl.BlockSpec((D, IN_COLS), const2),
        pl.BlockSpec((1, Q_LORA), const2),
        pl.BlockSpec((Q_LORA, 1024), const2),
        pl.BlockSpec((1, KV_LORA), const2),
        pl.BlockSpec((1, 512), const2),
        pl.BlockSpec((1, 128), const2),
        pl.BlockSpec((512, 512), const2),
        pl.BlockSpec((128, 256), const2),
        pl.BlockSpec((1, 256), const2),
    ]
    args += [mod, tab, lw['g_attn'], lw['w_in_r'], lw['g_q_a'], lw['w_q2'], lw['g_kv_a'], lw['g_qh'], lw['g_kh'],
             lw['ones512'], lw['w_gate'], lw['b_gate']]
    out_shape, out_specs = [], []
    if has_y:
        out_shape.append(jax.ShapeDtypeStruct((N, D), F32))
        out_specs.append(pl.BlockSpec((tm, D), row))
    for width, dt in ((1024, BF16), (256, BF16), (512, BF16), (256, BF16), (256, BF16), (1024, F32)):
        out_shape.append(jax.ShapeDtypeStruct((N, width), dt))
        out_specs.append(pl.BlockSpec((tm, width), row))
    out_shape.append(jax.ShapeDtypeStruct(((n_ctx_tiles + 1) * tm, 512), F32))
    out_specs.append(pl.BlockSpec((tm, 512), lambda i, mi, ti: (jnp.minimum(i, n_ctx_tiles), 0)))
    outs = pl.pallas_call(
        functools.partial(_pre_attn_kernel, has_y=has_y),
        grid_spec=pltpu.PrefetchScalarGridSpec(num_scalar_prefetch=2, grid=(nt,), in_specs=in_specs,
                                               out_specs=out_specs),
        out_shape=out_shape,
        compiler_params=pltpu.CompilerParams(dimension_semantics=("arbitrary",), vmem_limit_bytes=VMEM_LIMIT),
        name="pre_attn",
    )(modidx, tabidx, *args)
    if not has_y:
        outs = [x] + list(outs)
    return outs


def _softmax_step(qs, k, m_ref, c):
    s = _dot_nt(qs, k)
    m_old = m_ref[c]
    m_new = jnp.maximum(m_old, jnp.broadcast_to(jnp.max(s, axis=-1, keepdims=True), m_old.shape))
    alpha = jnp.exp2(m_old - m_new)
    p = jnp.exp2(s - jnp.concatenate([m_new] * (k.shape[0] // LANES), axis=1))
    m_ref[c] = m_new
    return alpha, p


def _mla_attn_kernel(*refs, has_ctx, n_chunks, tk):
    if has_ctx:
        q_ref, kc_ref, k_ref, wuv_ref, o_ref, m_ref, l_ref, acc_ref = refs
    else:
        q_ref, k_ref, wuv_ref, o_ref, m_ref, l_ref, acc_ref = refs
        kc_ref = None
    tq = q_ref.shape[0]
    m_ref[...] = jnp.full(m_ref.shape, NEG_BIG, F32)
    l_ref[...] = jnp.zeros(l_ref.shape, F32)
    acc_ref[...] = jnp.zeros(acc_ref.shape, F32)
    q = q_ref[...]
    qs = [jnp.concatenate([q[:, h * 256:(h + 1) * 256] for h in (2 * c, 2 * c + 1)], axis=0) for c in range(2)]

    def process(kblk):
        for c in range(2):
            alpha, p = _softmax_step(qs[c], kblk, m_ref, c)
            l_ref[c] = alpha * l_ref[c] + jnp.broadcast_to(jnp.sum(p, axis=-1, keepdims=True), alpha.shape)
            acc_ref[c] = alpha * acc_ref[c] + _dot(p.astype(BF16), kblk[:, 0:KV_LORA])

    if has_ctx:
        process(kc_ref[0])

    def body(j, carry):
        process(k_ref[pl.ds(pl.multiple_of(j * tk, tk), tk), :])
        return carry

    lax.fori_loop(0, n_chunks, body, 0)
    out = jnp.zeros(o_ref.shape, F32)
    for h in range(MLA_HEADS):
        rows = slice((h % 2) * tq, (h % 2 + 1) * tq)
        o_lat = acc_ref[h // 2, rows, :] / l_ref[h // 2, rows, :]
        out = out + _dot(o_lat.astype(BF16), wuv_ref[h])
    o_ref[...] = out.astype(o_ref.dtype)


def _gqa_attn_kernel(*refs, has_ctx, n_chunks, tk):
    if has_ctx:
        q_ref, kc_ref, vc_ref, k_ref, v_ref, o_ref, m_ref, acc_ref = refs
    else:
        q_ref, k_ref, v_ref, o_ref, m_ref, acc_ref = refs
        kc_ref = vc_ref = None
    tq = q_ref.shape[0]
    m_ref[...] = jnp.full(m_ref.shape, NEG_BIG, F32)
    acc_ref[...] = jnp.zeros(acc_ref.shape, F32)
    q = q_ref[...]
    lane = lax.broadcasted_iota(I32, (tq, LANES), 1)
    low = lane < HEAD_DIM
    qs = []
    for g in range(GQA_KV_HEADS):
        parts = []
        for h in range(g * GQA_GROUP, (g + 1) * GQA_GROUP):
            pair = q[:, (h // 2) * LANES:(h // 2 + 1) * LANES]
            keep = low if h % 2 == 0 else jnp.logical_not(low)
            parts.append(jnp.where(keep, pair, jnp.zeros_like(pair)))
        qs.append(jnp.concatenate(parts, axis=0))

    def process(kblk, vblk):
        for g in range(GQA_KV_HEADS):
            alpha, p = _softmax_step(qs[g], kblk[:, g * LANES:(g + 1) * LANES], m_ref, g)
            acc_ref[g] = alpha * acc_ref[g] + _dot(p.astype(BF16), vblk[:, g * LANES:(g + 1) * LANES])

    if has_ctx:
        process(kc_ref[0], vc_ref[0])

    def body(j, carry):
        sl = pl.ds(pl.multiple_of(j * tk, tk), tk)
        process(k_ref[sl, :], v_ref[sl, :])
        return carry

    lax.fori_loop(0, n_chunks, body, 0)
    for j in range(GQA_HEADS // 2):
        outs = []
        for h in (2 * j, 2 * j + 1):
            a = acc_ref[h // GQA_GROUP, (h % GQA_GROUP) * tq:(h % GQA_GROUP + 1) * tq, :]
            outs.append(a / jnp.where(low, pltpu.roll(a, HEAD_DIM, 1), a))
        o_ref[:, j * LANES:(j + 1) * LANES] = jnp.where(low, outs[0], pltpu.roll(outs[1], HEAD_DIM, 1)).astype(
            o_ref.dtype)


def _attention(kind, q, k, v, ctx, wuv, *, n_seq, seq_len, tile_offset):
    has_ctx = ctx is not None
    tq = min(ATTN_TQ, seq_len)
    tk = min(ATTN_TK, seq_len)
    nq = seq_len // tq
    qw = q.shape[1]
    ow = 256 if kind == 'mla' else 512
    heads = MLA_HEADS if kind == 'mla' else GQA_HEADS
    qmap = lambda b, i: (tile_offset * nq + b * nq + i, 0)
    kmap = lambda b, i: (tile_offset + b, 0)
    cmap = lambda b, i: (b, 0, 0)
    in_specs = [pl.BlockSpec((tq, qw), qmap)]
    args = [q]
    if kind == 'mla':
        if has_ctx:
            in_specs.append(pl.BlockSpec((1,) + ctx[0].shape[1:], cmap))
            args.append(ctx[0])
        in_specs += [pl.BlockSpec((seq_len, 256), kmap), pl.BlockSpec((MLA_HEADS, 128, 256), lambda b, i: (0, 0, 0))]
        args += [k, wuv]
        body = _mla_attn_kernel
    else:
        if has_ctx:
            in_specs += [pl.BlockSpec((1,) + ctx[0].shape[1:], cmap), pl.BlockSpec((1,) + ctx[1].shape[1:], cmap)]
            args += [ctx[0], ctx[1]]
        in_specs += [pl.BlockSpec((seq_len, 256), kmap), pl.BlockSpec((seq_len, 256), kmap)]
        args += [k, v]
        body = _gqa_attn_kernel
    return pl.pallas_call(
        functools.partial(body, has_ctx=has_ctx, n_chunks=seq_len // tk, tk=tk),
        grid=(n_seq, nq),
        in_specs=in_specs,
        out_specs=pl.BlockSpec((tq, ow), lambda b, i: (b * nq + i, 0)),
        out_shape=jax.ShapeDtypeStruct((n_seq * seq_len, ow), BF16),
        scratch_shapes=[pltpu.VMEM((2, heads // 2 * tq, LANES), F32)] * (3 if kind == 'mla' else 2),
        compiler_params=pltpu.CompilerParams(dimension_semantics=("arbitrary", "arbitrary"),
                                             vmem_limit_bytes=VMEM_LIMIT),
        name=kind + ("_attn_latent" if has_ctx else "_attn_context"),
    )(*args)


def _gla_kernel(blk_ref, seq_ref, first_ref, *refs, reverse):
    del blk_ref, seq_ref
    if reverse:
        gin_ref, s0_ref, he_ref, of_ref, gout_ref, ones_ref, o_ref, sout_ref, st_ref = refs
    else:
        gin_ref, s0_ref, he_ref, o_ref, sout_ref, st_ref = refs
    i = pl.program_id(0)

    @pl.when(first_ref[i] == 1)
    def _():
        st_ref[...] = s0_ref[0]

    R, S = TOKEN_TILE, GLA_STEP
    ns = R // S
    q = gin_ref[:, 0:128]
    k = gin_ref[:, 128:256]
    g = gin_ref[:, 384:512] if reverse else gin_ref[:, 256:384]
    v = gin_ref[:, 512:768]
    pos = lax.broadcasted_iota(I32, (R, 128), 0) % S
    b = g
    for s in (1, 2, 4, 8):
        if reverse:
            b = b + jnp.where(pos < S - s, pltpu.roll(b, R - s, 0), 0.0)
        else:
            b = b + jnp.where(pos >= s, pltpu.roll(b, s, 0), 0.0)

    def step_row(a, j):
        w = a.shape[-1]
        a3 = a.reshape(ns, S, w)
        return jnp.broadcast_to(a3[:, j:j + 1, :], (ns, S, w)).reshape(R, w)

    b_edge = step_row(b, 0 if reverse else S - 1)
    qt = (q * jnp.exp(b)).astype(BF16)
    kt = (k * jnp.exp(b_edge - b)).astype(BF16)
    d_edge = jnp.exp(b_edge)
    vb = v.astype(BF16)
    he = he_ref[...]

    o_intra = jnp.zeros((R, 256), F32)
    for j in range(S):
        cond = (pos <= j) if reverse else (pos >= j)
        bj, kj, vj = step_row(b, j), step_row(k, j), step_row(v, j)
        t = jnp.where(cond, q * kj * jnp.exp(jnp.where(cond, b - bj, 0.0)), 0.0)
        o_intra = o_intra + _dot(t.astype(BF16), he) * vj

    own_head = (lax.broadcasted_iota(I32, (256, 128), 0) // GLA_DV) == (lax.broadcasted_iota(I32, (256, 128), 1) // GLA_DK)
    st = st_ref[...]
    parts = [None] * ns
    for t in (range(ns - 1, -1, -1) if reverse else range(ns)):
        rows = slice(t * S, (t + 1) * S)
        parts[t] = _dot_nt(qt[rows], st.astype(BF16)) + o_intra[rows]
        ut = _dot_tn(vb[rows], kt[rows])
        st = d_edge[t * S:t * S + 1, :] * st + jnp.where(own_head, ut, 0.0)
    st_ref[...] = st
    sout_ref[0] = st
    o = jnp.concatenate(parts, axis=0)
    if reverse:
        o = o + of_ref[...]
        ms = _split_dot(o * o, ones_ref[...]) * (1.0 / GLA_DV)
        on = o * lax.rsqrt(ms + EPS) * gout_ref[...]
        lg = gin_ref[:, 768:1024]
        o_ref[...] = (on * (lg * jax.nn.sigmoid(lg))).astype(o_ref.dtype)
    else:
        o_ref[...] = o


def _gla(gin, s0, o_f, lw, order, *, reverse):
    blk, seq, first = order
    N = gin.shape[0]
    tm = TOKEN_TILE
    n_seq = s0.shape[0]
    tile = lambda i, b, s, f: (b[i], 0)
    state = lambda i, b, s, f: (s[i], 0, 0)
    const2 = lambda i, b, s, f: (0, 0)
    in_specs = [pl.BlockSpec((tm, 1024), tile), pl.BlockSpec((1, 256, 128), state), pl.BlockSpec((128, 256), const2)]
    args = [gin, s0, lw['head_expand']]
    if reverse:
        in_specs += [pl.BlockSpec((tm, 256), tile), pl.BlockSpec((1, 256), const2), pl.BlockSpec((256, 256), const2)]
        args += [o_f, lw['g_gla_out'], lw['ones256']]
    return pl.pallas_call(
        functools.partial(_gla_kernel, reverse=reverse),
        grid_spec=pltpu.PrefetchScalarGridSpec(
            num_scalar_prefetch=3, grid=(N // tm,), in_specs=in_specs,
            out_specs=[pl.BlockSpec((tm, 256), tile), pl.BlockSpec((1, 256, 128), state)],
            scratch_shapes=[pltpu.VMEM((256, 128), F32)]),
        out_shape=[jax.ShapeDtypeStruct((N, 256), BF16 if reverse else F32),
                   jax.ShapeDtypeStruct((n_seq, 256, 128), F32)],
        compiler_params=pltpu.CompilerParams(dimension_semantics=("arbitrary",), vmem_limit_bytes=VMEM_LIMIT),
        name="gla_bwd" if reverse else "gla_fwd",
    )(blk, seq, first, *args)


def _post_attn_kernel(modidx_ref, x_ref, omc_ref, oml_ref, ogc_ref, ogl_ref, ol_ref, mod_ref, wout_ref, gffn_ref,
                      wrh_ref, wrl_ref, br_ref, tri_ref, x1_ref, h2_ref, ti_ref, tg_ref, cnt_ref, carry_ref, *,
                      n_ctx_tiles):
    del modidx_ref
    mod = mod_ref[0]
    gate1, shift2, scale2 = mod[2:3, :], mod[3:4, :], mod[4:5, :]
    is_ctx = pl.program_id(0) < n_ctx_tiles
    om = jnp.where(is_ctx, omc_ref[...], oml_ref[...])
    og = jnp.where(is_ctx, ogc_ref[...], ogl_ref[...])
    o = _dot(om, wout_ref[0:256, :]) + _dot(og, wout_ref[256:768, :]) + _dot(ol_ref[...], wout_ref[768:1024, :])
    x1 = x_ref[...] + gate1 * o
    x1_ref[...] = x1
    xn = x1 * lax.rsqrt(jnp.mean(x1 * x1, axis=-1, keepdims=True) + EPS) * gffn_ref[...]
    h2 = xn * (1.0 + scale2) + shift2
    _to_token_tiles(h2_ref, h2)
    hi = h2.astype(BF16)
    lo = (h2 - hi.astype(F32)).astype(BF16)
    wrh = wrh_ref[...]
    logits = _dot(hi, wrh) + _dot(hi, wrl_ref[...]) + _dot(lo, wrh) + br_ref[...]

    lane = lax.broadcasted_iota(I32, logits.shape, 1).astype(F32)
    work = logits
    vals, idxs = [], []
    for _ in range(TOP_K):
        m = jnp.max(work, axis=-1, keepdims=True)
        idx = jnp.min(jnp.where(work == m, lane, float(LANES)), axis=-1, keepdims=True)
        vals.append(m)
        idxs.append(idx)
        work = jnp.where(lane == idx, NEG_BIG * 4.0, work)
    es = [jnp.exp(vv - vals[0]) for vv in vals]
    denom = es[0] + es[1] + es[2] + es[3]
    ti = jnp.zeros(logits.shape, F32)
    tg = jnp.zeros(logits.shape, F32)
    for r in range(TOP_K):
        ti = jnp.where(lane == float(r), idxs[r], ti)
        tg = jnp.where(lane == float(r), es[r] / denom, tg)
    tg_ref[...] = tg

    @pl.when(pl.program_id(0) == 0)
    def _():
        carry_ref[...] = jnp.zeros(carry_ref.shape, F32)

    picks = [lane == idxs[r] for r in range(TOP_K)]
    onehot = sum(jnp.where(pk, 1.0, 0.0) for pk in picks)
    carry = carry_ref[...]
    before = carry + _dot(tri_ref[...], onehot.astype(BF16))
    for r in range(TOP_K):
        rank = jnp.sum(jnp.where(picks[r], before, 0.0), axis=-1, keepdims=True)
        ti = jnp.where(lane == float(TOP_K + r), rank, ti)
    ti_ref[...] = ti.astype(I32)
    carry = carry + jnp.sum(onehot, axis=0, keepdims=True)
    carry_ref[...] = carry
    cnt_ref[...] = jnp.broadcast_to(carry, cnt_ref.shape)


def _post_attn(x, o_mla, o_gqa, o_gla, mod, modidx, lw, n_ctx_tiles):
    N, D = x.shape
    tm = TOKEN_TILE
    row = lambda i, mi: (i, 0)
    ctx_row = lambda i, mi: (jnp.minimum(i, n_ctx_tiles - 1), 0)
    lat_row = lambda i, mi: (jnp.maximum(i - n_ctx_tiles, 0), 0)
    const2 = lambda i, mi: (0, 0)
    return pl.pallas_call(
        functools.partial(_post_attn_kernel, n_ctx_tiles=n_ctx_tiles),
        grid_spec=pltpu.PrefetchScalarGridSpec(
            num_scalar_prefetch=1, grid=(N // tm,),
            in_specs=[pl.BlockSpec((tm, D), row), pl.BlockSpec((tm, 256), ctx_row), pl.BlockSpec((tm, 256), lat_row),
                      pl.BlockSpec((tm, 512), ctx_row), pl.BlockSpec((tm, 512), lat_row),
                      pl.BlockSpec((tm, 256), row), pl.BlockSpec((1, 6, D), lambda i, mi: (mi[i], 0, 0)),
                      pl.BlockSpec((D, D), const2), pl.BlockSpec((1, D), const2),
                      pl.BlockSpec((D, LANES), const2), pl.BlockSpec((D, LANES), const2),
                      pl.BlockSpec((1, LANES), const2), pl.BlockSpec((tm, tm), const2)],
            out_specs=[pl.BlockSpec((tm, D), row), pl.BlockSpec((tm * SUBLANES, LANES), row),
                       pl.BlockSpec((tm, LANES), row), pl.BlockSpec((tm, LANES), row),
                       pl.BlockSpec((SUBLANES, LANES), const2)],
            scratch_shapes=[pltpu.VMEM((1, LANES), F32)]),
        out_shape=[jax.ShapeDtypeStruct((N, D), F32), jax.ShapeDtypeStruct((N * SUBLANES, LANES), F32),
                   jax.ShapeDtypeStruct((N, LANES), I32), jax.ShapeDtypeStruct((N, LANES), F32),
                   jax.ShapeDtypeStruct((SUBLANES, LANES), F32)],
        compiler_params=pltpu.CompilerParams(dimension_semantics=("arbitrary",), vmem_limit_bytes=VMEM_LIMIT),
        name="post_attn",
    )(modidx, x, o_mla[0], o_mla[1], o_gqa[0], o_gqa[1], o_gla, mod, lw['w_out'], lw['g_ffn'], lw['w_router_hi'],
      lw['w_router_lo'], lw['b_router'], lw['tri'])


def _moe_kernel(be_ref, first_ref, valid_ref, src0_ref, src_ref, dst_ref, dstp_ref, h_hbm, wgu_ref, bgu_ref, wd_ref, bd_ref,
                y_hbm, wgu_bf, wd_bf, xbuf, ybuf, xsem, ysem):
    del be_ref
    i = pl.program_id(0)
    slot = i % 2
    other = 1 - slot
    M = MOE_ROWS

    def gather(idx_ref, r, s):
        tok = idx_ref[0, 0, r]
        return pltpu.make_async_copy(h_hbm.at[pl.ds(pl.multiple_of(tok * SUBLANES, SUBLANES), SUBLANES), :],
                                     xbuf.at[s, pl.ds(r * SUBLANES, SUBLANES), :], xsem.at[s])

    def scatter(idx_ref, r, s):
        row = idx_ref[0, 0, r]
        return pltpu.make_async_copy(ybuf.at[s, pl.ds(r * SUBLANES, SUBLANES), :],
                                     y_hbm.at[pl.ds(pl.multiple_of(row * SUBLANES, SUBLANES), SUBLANES), :],
                                     ysem.at[s])

    def all_rows_in(s):
        return pltpu.make_async_copy(h_hbm.at[pl.ds(0, M * SUBLANES), :], xbuf.at[s], xsem.at[s])

    def all_rows_out(s):
        return pltpu.make_async_copy(ybuf.at[s], y_hbm.at[pl.ds(0, M * SUBLANES), :], ysem.at[s])

    @pl.when(i == 0)
    def _():
        ybuf[...] = jnp.zeros(ybuf.shape, F32)

        def prime(r, carry):
            gather(src0_ref, r, 0).start()
            scatter(dstp_ref, r, 0).start()
            return carry
        lax.fori_loop(0, M, prime, 0)

    @pl.when(first_ref[i] == 1)
    def _():
        wgu_bf[...] = wgu_ref[...].astype(BF16)
        wd_bf[...] = wd_ref[...].astype(BF16)

    all_rows_in(slot).wait()

    @pl.when(valid_ref[i] == 1)
    def _():
        for r in range(M):
            scatter(dst_ref, r, other).start(priority=1)
            gather(src_ref, r, other).start()
        gu = _dot(_from_token_tiles(xbuf.at[slot], M).astype(BF16), wgu_bf[...]) + bgu_ref[...]
        gate = jnp.minimum(gu[:, :D_EXPERT], SWIGLU_LIMIT)
        up = jnp.clip(gu[:, D_EXPERT:], -SWIGLU_LIMIT, SWIGLU_LIMIT)
        act = gate * jax.nn.sigmoid(SWIGLU_ALPHA * gate) * (up + 1.0)
        y = _dot(act.astype(BF16), wd_bf[...]) + bd_ref[...]
        all_rows_out(slot).wait()
        _to_token_tiles(ybuf.at[slot], y)

    @pl.when(valid_ref[i] == 0)
    def _():
        def move(r, carry):
            scatter(dst_ref, r, other).start(priority=1)
            gather(src_ref, r, other).start()
            return carry
        lax.fori_loop(0, M, move, 0)
        all_rows_out(slot).wait()

    @pl.when(i == pl.num_programs(0) - 1)
    def _():
        all_rows_out(other).wait()
        all_rows_in(other).wait()


def _moe_experts(h2t, sched, w_gate_up, b_gate_up, w_down, b_down, layer, n_out_rows):
    block_e, first, valid, src, dst, dst_prime = sched
    n_blocks = block_e.shape[0]
    D = D_MODEL
    M = MOE_ROWS
    L, E = w_gate_up.shape[:2]
    wmap = lambda i, be, fi, va: (layer, be[i], 0, 0)
    smem = lambda imap: pl.BlockSpec((1, 1, M), imap, memory_space=pltpu.SMEM)
    return pl.pallas_call(
        _moe_kernel,
        grid_spec=pltpu.PrefetchScalarGridSpec(
            num_scalar_prefetch=3, grid=(n_blocks,),
            in_specs=[smem(lambda i, be, fi, va: (0, 0, 0)), smem(lambda i, be, fi, va: (i + 1, 0, 0)),
                      smem(lambda i, be, fi, va: (i, 0, 0)), smem(lambda i, be, fi, va: (0, 0, 0)),
                      pl.BlockSpec(memory_space=pl.ANY),
                      pl.BlockSpec((None, None, D, 2 * D_EXPERT), wmap),
                      pl.BlockSpec((None, None, 1, 2 * D_EXPERT), wmap),
                      pl.BlockSpec((None, None, D_EXPERT, D), wmap),
                      pl.BlockSpec((None, None, 1, D), wmap)],
            out_specs=pl.BlockSpec(memory_space=pl.ANY),
            scratch_shapes=[pltpu.VMEM((D, 2 * D_EXPERT), BF16), pltpu.VMEM((D_EXPERT, D), BF16),
                            pltpu.VMEM((2, M * SUBLANES, LANES), F32), pltpu.VMEM((2, M * SUBLANES, LANES), F32),
                            pltpu.SemaphoreType.DMA((2,)), pltpu.SemaphoreType.DMA((2,))]),
        out_shape=jax.ShapeDtypeStruct((n_out_rows * SUBLANES, LANES), F32),
        compiler_params=pltpu.CompilerParams(dimension_semantics=("arbitrary",), vmem_limit_bytes=VMEM_LIMIT),
        name="moe_experts",
    )(block_e, first, valid, src, src, dst, dst_prime, h2t, w_gate_up, b_gate_up.reshape(L, E, 1, 2 * D_EXPERT), w_down,
      b_down.reshape(L, E, 1, D))


def _route(top_idx, rank, counts, n_blocks):
    N = top_idx.shape[0]
    padded = (counts + MOE_ROWS - 1) // MOE_ROWS * MOE_ROWS
    pad_end = jnp.cumsum(padded)
    pad_start = pad_end - padded
    dest = pad_start[top_idx] + rank
    M, K = MOE_ROWS, TOP_K
    asg = jnp.arange(N * K, dtype=I32)
    slot_asg = jnp.full((n_blocks * M,), -1, I32).at[dest.reshape(-1)].set(asg)
    real = slot_asg >= 0
    tok = jnp.where(real, slot_asg // K, 0)
    slot_id = jnp.arange(n_blocks * M, dtype=I32)
    spare = K * N + ((slot_id // M) % 2) * M + slot_id % M
    out_row = jnp.where(real, (slot_asg % K) * N + tok, spare)
    src = jnp.concatenate([tok.reshape(n_blocks, 1, M), jnp.zeros((1, 1, M), I32)], axis=0)
    lead = (K * N + M + jnp.arange(M, dtype=I32)).reshape(1, 1, M)
    dst = jnp.concatenate([lead, out_row.reshape(n_blocks, 1, M)], axis=0)
    dst_prime = (K * N + jnp.arange(M, dtype=I32)).reshape(1, 1, M)
    blk_start = jnp.arange(n_blocks, dtype=I32) * M
    valid = blk_start < pad_end[-1]
    block_e = jnp.minimum(jnp.searchsorted(pad_end, blk_start, side='right'), N_EXPERTS - 1).astype(I32)
    n_valid = pad_end[-1] // M
    block_e = block_e[jnp.minimum(jnp.arange(n_blocks, dtype=I32), n_valid - 1)]
    prev_e = jnp.concatenate([jnp.full((1,), -1, I32), block_e[:-1]])
    first = (valid & (block_e != prev_e)).astype(I32)
    return (block_e, first, valid.astype(I32), src, dst, dst_prime)


def _final_kernel(modidx_ref, x_ref, *refs):
    del modidx_ref
    y_refs, (tg_ref, pmod_ref, g_ref, o_ref) = refs[:TOP_K], refs[TOP_K:]
    x = x_ref[...] + pmod_ref[0, 5:6, :] * _combine_experts(y_refs, tg_ref)
    o_ref[...] = x * lax.rsqrt(jnp.mean(x * x, axis=-1, keepdims=True) + EPS) * g_ref[...]


def _final_norm(x, y, pmod, modidx, g_final):
    N, D = x.shape
    tm = TOKEN_TILE
    row = lambda i, mi: (i, 0)
    return pl.pallas_call(
        _final_kernel,
        grid_spec=pltpu.PrefetchScalarGridSpec(
            num_scalar_prefetch=1, grid=(N // tm,),
            in_specs=[pl.BlockSpec((tm, D), row)]
            + [pl.BlockSpec((tm * SUBLANES, LANES), functools.partial(lambda k, i, mi: (k * (N // tm) + i, 0), k))
               for k in range(TOP_K)]
            + [pl.BlockSpec((tm, LANES), row),
               pl.BlockSpec((1, 6, D), lambda i, mi: (mi[i], 0, 0)), pl.BlockSpec((1, D), lambda i, mi: (0, 0))],
            out_specs=pl.BlockSpec((tm, D), row)),
        out_shape=jax.ShapeDtypeStruct((N, D), F32),
        compiler_params=pltpu.CompilerParams(dimension_semantics=("arbitrary",), vmem_limit_bytes=VMEM_LIMIT),
        name="final_norm",
    )(modidx, x, *([y[0]] * TOP_K), y[1], pmod, g_final)


def _rope_tables(seq_len):
    pos = np.arange(seq_len)
    rowp = (pos // GRID_W).astype(np.float32)
    colp = (pos % GRID_W).astype(np.float32)

    def tables(rdim, copies):
        quarter = rdim // 4
        inv_freq = jnp.asarray(ROPE_THETA, F32) ** (-jnp.arange(quarter, dtype=F32) / quarter)
        ar = jnp.asarray(rowp)[:, None] * inv_freq[None, :]
        ac = jnp.asarray(colp)[:, None] * inv_freq[None, :]
        cos = jnp.concatenate([jnp.cos(ar), jnp.cos(ar), jnp.cos(ac), jnp.cos(ac)], axis=1)
        sin = jnp.concatenate([-jnp.sin(ar), jnp.sin(ar), -jnp.sin(ac), jnp.sin(ac)], axis=1)
        return jnp.tile(cos, (1, copies)), jnp.tile(sin, (1, copies))

    c64, s64 = tables(HEAD_DIM, 2)
    c32, s32 = tables(MLA_ROPE, 4)
    lat = jnp.concatenate([c64, s64, c32, s32], axis=1).reshape(seq_len // TOKEN_TILE, TOKEN_TILE, 512)
    ident = jnp.concatenate([jnp.ones((TOKEN_TILE, 128), F32), jnp.zeros((TOKEN_TILE, 128), F32)] * 2, axis=1)
    return jnp.concatenate([ident[None], lat], axis=0)


def _block_ones(n, blk):
    r = np.arange(n) // blk
    return jnp.asarray((r[:, None] == r[None, :]).astype(np.float32), BF16)


def kernel(x_prompt, x_sample, cache_mla_ckv, cache_mla_krope, cache_gqa_k, cache_gqa_v, state_gla, c, c_ctx, w_mod, b_mod, g_attn_norm, g_ffn_norm, w_in, g_q_a, w_uq, g_kv_a, w_ukv, g_q_head, g_k_head, w_gk_fwd, b_gk_fwd, w_gk_bwd, b_gk_bwd, g_gla_out, w_out, w_router, b_router, w_gate_up, b_gate_up, w_down, b_down, g_final):
    B, T, D = x_prompt.shape
    BD, TD, _ = x_sample.shape
    L = w_in.shape[0]
    P = cache_mla_ckv.shape[2]
    tm = TOKEN_TILE
    n_ctx = B * T
    N = n_ctx + BD * TD
    assert T == tm and TD % ATTN_TQ == 0 and n_ctx % TD == 0 and BD + 1 <= 16
    n_ctx_tiles = n_ctx // tm
    nt = N // tm
    tiles_per_lat = TD // tm

    tile_ids = np.arange(nt)
    lat_tile = np.maximum(tile_ids - n_ctx_tiles, 0)
    is_lat = tile_ids >= n_ctx_tiles
    modidx = jnp.asarray(np.where(is_lat, 1 + lat_tile // tiles_per_lat, 0), I32)
    tabidx = jnp.asarray(np.where(is_lat, 1 + lat_tile % tiles_per_lat, 0), I32)
    seq_of_tile = np.where(is_lat, B + lat_tile // tiles_per_lat, tile_ids)
    first_fwd = np.where(is_lat, lat_tile % tiles_per_lat == 0, True)
    last_fwd = np.where(is_lat, lat_tile % tiles_per_lat == tiles_per_lat - 1, True)
    order_fwd = (jnp.asarray(tile_ids, I32), jnp.asarray(seq_of_tile, I32), jnp.asarray(first_fwd, I32))
    rev = tile_ids[::-1]
    order_bwd = (jnp.asarray(rev, I32), jnp.asarray(seq_of_tile[rev], I32), jnp.asarray(last_fwd[rev], I32))
    n_seq = B + BD

    cvecs = jnp.zeros((16, D), F32).at[0].set(c_ctx).at[1:1 + BD].set(c)
    mods = _modulation(cvecs, w_mod, b_mod).reshape(L, 16, 6, D)

    o = np.cumsum([0, Q_LORA, KV_LORA, MLA_ROPE, 512, 128, 128, 128, 128, 256, 16, 16, 256])
    seg = lambda j: w_in[:, :, o[j]:o[j + 1]]
    w_in_r = jnp.concatenate(
        [seg(0), seg(1), seg(3), seg(4), seg(5), seg(6), seg(7), seg(8), seg(11), seg(2), seg(2), seg(2), seg(2),
         seg(9), seg(10), jnp.zeros((L, D, 128 - 2 * GLA_GATE_RANK), F32)], axis=-1).astype(BF16)
    uq = w_uq.reshape(L, Q_LORA, MLA_HEADS, MLA_QK)
    ukv = w_ukv.reshape(L, KV_LORA, MLA_HEADS, MLA_NOPE + MLA_V)
    w_comb = _fold_q_weights(uq[..., :MLA_NOPE].transpose(0, 2, 1, 3), ukv[..., :MLA_NOPE].transpose(0, 2, 1, 3))
    q_pad = jnp.zeros((L, Q_LORA, 256 - KV_LORA - MLA_ROPE), F32)
    w_q2 = (jnp.concatenate([part for h in range(MLA_HEADS)
                             for part in (w_comb[:, h], uq[:, :, h, MLA_NOPE:], q_pad)], axis=-1)
            * (MLA_QK ** -0.5 * LOG2E)).astype(BF16)
    w_uv = ukv[..., MLA_NOPE:].transpose(0, 2, 1, 3)
    eye_h = jnp.eye(MLA_HEADS, dtype=F32)
    wuv_pad = jnp.einsum('lhkv,hg->lhkgv', w_uv, eye_h).reshape(L, MLA_HEADS, KV_LORA, MLA_HEADS * MLA_V).astype(BF16)
    w_gate = jnp.zeros((L, 128, 256), F32).at[:, 0:16, 0:128].set(w_gk_fwd).at[:, 16:32, 128:256].set(w_gk_bwd)
    w_gate = w_gate.astype(BF16)
    b_gate = jnp.concatenate([b_gk_fwd, b_gk_bwd], axis=-1)
    w_out_bf = w_out.astype(BF16)
    wr_pad = jnp.pad(w_router, ((0, 0), (0, 0), (0, LANES - N_EXPERTS)))
    wr_hi = wr_pad.astype(BF16)
    wr_lo = (wr_pad - wr_hi.astype(F32)).astype(BF16)
    br_pad = jnp.pad(b_router, ((0, 0), (0, LANES - N_EXPERTS)), constant_values=NEG_BIG)
    ones512 = _block_ones(512, HEAD_DIM)
    tri = jnp.asarray((np.arange(tm)[None, :] < np.arange(tm)[:, None]).astype(np.float32), BF16)
    head_expand = jnp.asarray((np.arange(128)[:, None] // GLA_DK == np.arange(256)[None, :] // GLA_DV)
                              .astype(np.float32), BF16)
    tab = _rope_tables(TD)

    x = jnp.concatenate([x_prompt.reshape(n_ctx, D), x_sample.reshape(BD * TD, D)], axis=0)
    y = None
    n_blocks = (N * TOP_K) // MOE_ROWS + N_EXPERTS
    caches, states = [], []
    for l in range(L):
        lw = dict(g_attn=g_attn_norm[l][None], w_in_r=w_in_r[l], g_q_a=g_q_a[l][None], w_q2=w_q2[l],
                  g_kv_a=g_kv_a[l][None], g_qh=jnp.tile(g_q_head[l], GQA_HEADS)[None],
                  g_kh=jnp.tile(g_k_head[l], GQA_KV_HEADS)[None], ones512=ones512, w_gate=w_gate[l],
                  b_gate=b_gate[l][None], head_expand=head_expand, g_gla_out=jnp.tile(g_gla_out[l], GLA_HEADS)[None],
                  ones256=ones512[:256, :256], w_out=w_out_bf[l], g_ffn=g_ffn_norm[l][None], tri=tri,
                  w_router_hi=wr_hi[l], w_router_lo=wr_lo[l], b_router=br_pad[l][None])
        pmod = mods[l - 1] if l > 0 else None
        x, qm, kvc, qg, kdup, vdup, gin, cache = _pre_attn(x, y, pmod, mods[l], tab, modidx, tabidx, lw, n_ctx_tiles)
        caches.append(cache[:n_ctx])

        mla_ctx = jnp.concatenate([cache_mla_ckv[:, l]] + [cache_mla_krope[:, l]] * 4, axis=-1).astype(BF16)
        ck = cache_gqa_k[:, l]
        cv = cache_gqa_v[:, l]
        k_ctx = jnp.concatenate([ck[:, :, 0], ck[:, :, 0], ck[:, :, 1], ck[:, :, 1]], axis=-1).astype(BF16)
        ones_v = jnp.ones(cv.shape[:2] + (HEAD_DIM,), F32)
        v_ctx = jnp.concatenate([cv[:, :, 0], ones_v, cv[:, :, 1], ones_v], axis=-1).astype(BF16)
        o_mla = (_attention('mla', qm, kvc, None, None, wuv_pad[l], n_seq=B, seq_len=T, tile_offset=0),
                 _attention('mla', qm, kvc, None, (mla_ctx,), wuv_pad[l], n_seq=BD, seq_len=TD,
                            tile_offset=n_ctx // TD))
        o_gqa = (_attention('gqa', qg, kdup, vdup, None, None, n_seq=B, seq_len=T, tile_offset=0),
                 _attention('gqa', qg, kdup, vdup, (k_ctx, v_ctx), None, n_seq=BD, seq_len=TD,
                            tile_offset=n_ctx // TD))

        eye_g = jnp.eye(GLA_HEADS, dtype=F32)
        st_lat = jnp.einsum('bshde,hg->bshegd', state_gla[:, l], eye_g).reshape(BD, 2, 256, 128)
        zeros_ctx = jnp.zeros((B, 256, 128), F32)
        o_f, s_f = _gla(gin, jnp.concatenate([zeros_ctx, st_lat[:, 0]], axis=0), None, lw, order_fwd, reverse=False)
        o_gla, s_b = _gla(gin, jnp.concatenate([zeros_ctx, st_lat[:, 1]], axis=0), o_f, lw, order_bwd, reverse=True)
        states.append((s_f[:B], s_b[:B]))

        x1, h2, ti, tg, cnt = _post_attn(x, o_mla, o_gqa, o_gla, mods[l], modidx, lw, n_ctx_tiles)
        sched = _route(ti[:, :TOP_K], ti[:, TOP_K:2 * TOP_K], cnt[0, :N_EXPERTS].astype(I32), n_blocks)
        y = (_moe_experts(h2, sched, w_gate_up, b_gate_up, w_down, b_down, l, TOP_K * N + 2 * MOE_ROWS), tg)
        x = x1

    out = _final_norm(x, y, mods[L - 1], modidx, g_final[None])
    y_prompt = out[:n_ctx].reshape(B, T, D)
    y_sample = out[n_ctx:].reshape(BD, TD, D)
    cache_all = jnp.stack([cc.reshape(B, T, 512) for cc in caches], axis=1)
    new_ckv = cache_all[..., 0:128]
    new_k = cache_all[..., 128:256].reshape(B, L, T, GQA_KV_HEADS, HEAD_DIM)
    new_v = cache_all[..., 256:384].reshape(B, L, T, GQA_KV_HEADS, HEAD_DIM)
    new_krope = cache_all[..., 384:384 + MLA_ROPE]

    def unpack_state(st):
        s5 = st.reshape(B, GLA_HEADS, GLA_DV, GLA_HEADS, GLA_DK)
        diag = jnp.stack([s5[:, h, :, h, :] for h in range(GLA_HEADS)], axis=1)
        return diag.transpose(0, 1, 3, 2)

    new_state = jnp.stack([jnp.stack([unpack_state(sf), unpack_state(sb)], axis=1) for sf, sb in states], axis=1)
    return (y_prompt, y_sample, new_ckv, new_krope, new_k, new_v, new_state)
```

```python
import functools

import jax
import jax.numpy as jnp
import numpy as np
from jax import lax
from jax.experimental import pallas as pl
from jax.experimental.pallas import tpu as pltpu

F32 = jnp.float32
BF16 = jnp.bfloat16
I32 = jnp.int32

D_MODEL = 1024
GRID_W = 64
ROPE_THETA = 10000.0
EPS = 1e-6
MLA_HEADS = 4
Q_LORA = 256
KV_LORA = 128
MLA_NOPE = 64
MLA_ROPE = 32
MLA_V = 64
MLA_QK = MLA_NOPE + MLA_ROPE
GQA_HEADS = 8
GQA_KV_HEADS = 2
GQA_GROUP = GQA_HEADS // GQA_KV_HEADS
HEAD_DIM = 64
GLA_HEADS = 4
GLA_DK = 32
GLA_DV = 64
GLA_GATE_RANK = 16
GLA_GATE_NORM = 16.0
N_EXPERTS = 32
TOP_K = 4
D_EXPERT = D_MODEL
SWIGLU_LIMIT = 7.0
SWIGLU_ALPHA = 1.702

LANES = 128
SUBLANES = 8
TOKEN_TILE = 256
GLA_STEP = 16
MOE_ROWS = 512
ATTN_TQ = 512
ATTN_TK = 512
VMEM_LIMIT = 56 * 1024 * 1024
NEG_BIG = -1e30
LOG2E = 1.4426950408889634

C_CQ, C_CKV, C_GQ, C_GK, C_GV = 0, 256, 384, 896, 1024
C_LQ, C_LK, C_LV, C_LG, C_KR, C_GATE = 1152, 1280, 1408, 1664, 1920, 2048
IN_COLS = 2176


def _dot(a, b):
    return jnp.dot(a, b, preferred_element_type=F32)


def _dot_nt(a, b):
    return lax.dot_general(a, b, (((1,), (1,)), ((), ())), preferred_element_type=F32)


def _dot_tn(a, b):
    return lax.dot_general(a, b, (((0,), (0,)), ((), ())), preferred_element_type=F32)


def _split_dot(x, ones_bf):
    hi = x.astype(BF16)
    lo = (x - hi.astype(F32)).astype(BF16)
    return _dot(hi, ones_bf) + _dot(lo, ones_bf)


def _rope(x, cos, sin_signed, quarter):
    width = x.shape[-1]
    lane = lax.broadcasted_iota(I32, x.shape, 1)
    first = (lane % (2 * quarter)) < quarter
    partner = jnp.where(first, pltpu.roll(x, width - quarter, 1), pltpu.roll(x, quarter, 1))
    return x * cos + partner * sin_signed


def _mod_kernel(c_ref, w_ref, b_ref, o_ref):
    c = c_ref[...]
    a = c * jax.nn.sigmoid(c)
    o_ref[0] = jnp.dot(a, w_ref[0], precision=lax.Precision.HIGHEST, preferred_element_type=F32) + b_ref[0]


def _modulation(cvecs, w_mod, b_mod):
    L, D, D6 = w_mod.shape
    tn = 1536
    return pl.pallas_call(
        _mod_kernel,
        grid=(L, D6 // tn),
        in_specs=[pl.BlockSpec((16, D), lambda l, j: (0, 0)),
                  pl.BlockSpec((1, D, tn), lambda l, j: (l, 0, j)),
                  pl.BlockSpec((1, 1, tn), lambda l, j: (l, 0, j))],
        out_specs=pl.BlockSpec((1, 16, tn), lambda l, j: (l, 0, j)),
        out_shape=jax.ShapeDtypeStruct((L, 16, D6), F32),
        compiler_params=pltpu.CompilerParams(dimension_semantics=("arbitrary", "arbitrary"),
                                             vmem_limit_bytes=VMEM_LIMIT),
        name="modulation",
    )(cvecs, w_mod, b_mod.reshape(L, 1, D6))


def _fold_kernel(a_ref, b_ref, o_ref):
    o_ref[0, 0] = lax.dot_general(a_ref[0, 0], b_ref[0, 0], (((1,), (1,)), ((), ())),
                                  precision=lax.Precision.HIGHEST, preferred_element_type=F32)


def _fold_q_weights(wq_nope, wk_nope):
    L, H, A, K = wq_nope.shape
    B = wk_nope.shape[2]
    return pl.pallas_call(
        _fold_kernel,
        grid=(L, H),
        in_specs=[pl.BlockSpec((1, 1, A, K), lambda l, h: (l, h, 0, 0)),
                  pl.BlockSpec((1, 1, B, K), lambda l, h: (l, h, 0, 0))],
        out_specs=pl.BlockSpec((1, 1, A, B), lambda l, h: (l, h, 0, 0)),
        out_shape=jax.ShapeDtypeStruct((L, H, A, B), F32),
        name="fold_q_weights",
    )(wq_nope, wk_nope)


def _from_token_tiles(ref, rows):
    return jnp.concatenate([ref[pl.ds(c, rows, stride=SUBLANES), :] for c in range(D_MODEL // LANES)], axis=1)


def _to_token_tiles(ref, x):
    for c in range(D_MODEL // LANES):
        ref[pl.ds(c, x.shape[0], stride=SUBLANES), :] = x[:, c * LANES:(c + 1) * LANES]


def _combine_experts(y_refs, tg_ref):
    tg = tg_ref[...]
    acc = None
    for k in range(TOP_K):
        term = tg[:, k:k + 1] * _from_token_tiles(y_refs[k], tg.shape[0])
        acc = term if acc is None else acc + term
    return acc


def _pre_attn_kernel(modidx_ref, tabidx_ref, *refs, has_y):
    del modidx_ref, tabidx_ref
    if has_y:
        x_ref, y_ref, tg_ref, pmod_ref = refs[0], refs[1:1 + TOP_K], refs[1 + TOP_K], refs[2 + TOP_K]
        refs = refs[3 + TOP_K:]
    else:
        x_ref = refs[0]
        refs = refs[1:]
    (mod_ref, tab_ref, gattn_ref, win_ref, gqa_ref, wq2_ref, gkva_ref, gqh_ref, gkh_ref, ones_ref,
     wg_ref, bg_ref) = refs[:12]
    outs = refs[12:]
    if has_y:
        xres_ref = outs[0]
        outs = outs[1:]
    qm_ref, kvc_ref, qg_ref, kdup_ref, vdup_ref, gla_ref, cache_ref = outs

    x = x_ref[...]
    if has_y:
        x = x + pmod_ref[0, 5:6, :] * _combine_experts(y_ref, tg_ref)
        xres_ref[...] = x
    mod = mod_ref[0]
    shift1, scale1 = mod[0:1, :], mod[1:2, :]
    xn = x * lax.rsqrt(jnp.mean(x * x, axis=-1, keepdims=True) + EPS) * gattn_ref[...]
    h = xn * (1.0 + scale1) + shift1
    p = _dot(h.astype(BF16), win_ref[...])

    tab = tab_ref[0]
    cos64, sin64 = tab[:, 0:128], tab[:, 128:256]
    cos32, sin32 = tab[:, 256:384], tab[:, 384:512]
    ones = ones_ref[...]

    cq = p[:, C_CQ:C_CQ + Q_LORA]
    cqn = cq * lax.rsqrt(jnp.mean(cq * cq, axis=-1, keepdims=True) + EPS) * gqa_ref[...]
    qm = _dot(cqn.astype(BF16), wq2_ref[...])
    q_rope = _rope(jnp.concatenate([qm[:, h * 256 + 128:(h + 1) * 256] for h in range(MLA_HEADS)], axis=1),
                   jnp.concatenate([cos32] * MLA_HEADS, axis=1), jnp.concatenate([sin32] * MLA_HEADS, axis=1),
                   MLA_ROPE // 4)
    for h in range(MLA_HEADS):
        qm_ref[:, h * 256:h * 256 + 128] = qm[:, h * 256:h * 256 + 128].astype(BF16)
        qm_ref[:, h * 256 + 128:(h + 1) * 256] = q_rope[:, h * 128:(h + 1) * 128].astype(BF16)
    ckv_raw = p[:, C_CKV:C_CKV + KV_LORA]
    ckv = ckv_raw * lax.rsqrt(jnp.mean(ckv_raw * ckv_raw, axis=-1, keepdims=True) + EPS) * gkva_ref[...]
    kr4 = p[:, C_KR:C_KR + 128]
    kr4_rot = _rope(kr4, cos32, sin32, MLA_ROPE // 4)
    kvc_ref[:, 0:128] = ckv.astype(BF16)
    kvc_ref[:, 128:256] = kr4_rot.astype(BF16)

    gq = p[:, C_GQ:C_GQ + 512]
    ms_q = _split_dot(gq * gq, ones) * (1.0 / HEAD_DIM)
    qn = gq * lax.rsqrt(ms_q + EPS) * gqh_ref[...]
    cos_q = jnp.concatenate([cos64] * 4, axis=1)
    sin_q = jnp.concatenate([sin64] * 4, axis=1)
    qg = _rope(qn, cos_q, sin_q, HEAD_DIM // 4) * (HEAD_DIM ** -0.5 * LOG2E)
    qg_ref[...] = qg.astype(BF16)
    gk = p[:, C_GK:C_GK + 128]
    ms_k = _split_dot(gk * gk, ones[0:128, 0:128]) * (1.0 / HEAD_DIM)
    kn = gk * lax.rsqrt(ms_k + EPS) * gkh_ref[...]
    kg = _rope(kn, cos64, sin64, HEAD_DIM // 4)
    gv = p[:, C_GV:C_GV + 128]
    lane = lax.broadcasted_iota(I32, kg.shape, 1)
    low = lane < HEAD_DIM
    kg_sw = pltpu.roll(kg, HEAD_DIM, 1)
    gv_sw = pltpu.roll(gv, HEAD_DIM, 1)
    kdup_ref[:, 0:128] = jnp.where(low, kg, kg_sw).astype(BF16)
    kdup_ref[:, 128:256] = jnp.where(low, kg_sw, kg).astype(BF16)
    vdup_ref[:, 0:128] = jnp.where(low, gv, 1.0).astype(BF16)
    vdup_ref[:, 128:256] = jnp.where(low, gv_sw, 1.0).astype(BF16)

    z = _dot(p[:, C_GATE:C_GATE + 128].astype(BF16), wg_ref[...]) + bg_ref[...]
    logsig = jnp.minimum(z, 0.0) - jnp.log1p(jnp.exp(-jnp.abs(z)))
    gla_ref[:, 0:128] = p[:, C_LQ:C_LQ + 128] * (GLA_DK ** -0.5)
    gla_ref[:, 128:256] = p[:, C_LK:C_LK + 128]
    gla_ref[:, 256:512] = logsig * (1.0 / GLA_GATE_NORM)
    gla_ref[:, 512:768] = p[:, C_LV:C_LV + 256]
    gla_ref[:, 768:1024] = p[:, C_LG:C_LG + 256]

    cache_ref[:, 0:128] = ckv
    cache_ref[:, 128:256] = kn
    cache_ref[:, 256:384] = gv
    cache_ref[:, 384:512] = kr4


def _pre_attn(x, y, pmod, mod, tab, modidx, tabidx, lw, n_ctx_tiles):
    N, D = x.shape
    tm = TOKEN_TILE
    nt = N // tm
    has_y = y is not None
    row = lambda i, mi, ti: (i, 0)
    const2 = lambda i, mi, ti: (0, 0)
    in_specs = [pl.BlockSpec((tm, D), row)]
    args = [x]
    if has_y:
        yk, gates = y
        in_specs += [pl.BlockSpec((tm * SUBLANES, LANES), functools.partial(lambda k, i, mi, ti: (k * nt + i, 0), k))
                     for k in range(TOP_K)]
        in_specs += [pl.BlockSpec((tm, LANES), row), pl.BlockSpec((1, 6, D), lambda i, mi, ti: (mi[i], 0, 0))]
        args += [yk] * TOP_K + [gates, pmod]
    in_specs += [
        pl.BlockSpec((1, 6, D), lambda i, mi, ti: (mi[i], 0, 0)),
        pl.BlockSpec((1, tm, 512), lambda i, mi, ti: (ti[i], 0, 0)),
        pl.BlockSpec((1, D), const2),
        pl.BlockSpec((D, IN_COLS), const2),
        pl.BlockSpec((1, Q_LORA), const2),
        pl.BlockSpec((Q_LORA, 1024), const2),
        pl.BlockSpec((1, KV_LORA), const2),
        pl.BlockSpec((1, 512), const2),
        pl.BlockSpec((1, 128), const2),
        pl.BlockSpec((512, 512), const2),
        pl.BlockSpec((128, 256), const2),
        pl.BlockSpec((1, 256), const2),
    ]
    args += [mod, tab, lw['g_attn'], lw['w_in_r'], lw['g_q_a'], lw['w_q2'], lw['g_kv_a'], lw['g_qh'], lw['g_kh'],
             lw['ones512'], lw['w_gate'], lw['b_gate']]
    out_shape, out_specs = [], []
    if has_y:
        out_shape.append(jax.ShapeDtypeStruct((N, D), F32))
        out_specs.append(pl.BlockSpec((tm, D), row))
    for width, dt in ((1024, BF16), (256, BF16), (512, BF16), (256, BF16), (256, BF16), (1024, F32)):
        out_shape.append(jax.ShapeDtypeStruct((N, width), dt))
        out_specs.append(pl.BlockSpec((tm, width), row))
    out_shape.append(jax.ShapeDtypeStruct(((n_ctx_tiles + 1) * tm, 512), F32))
    out_specs.append(pl.BlockSpec((tm, 512), lambda i, mi, ti: (jnp.minimum(i, n_ctx_tiles), 0)))
    outs = pl.pallas_call(
        functools.partial(_pre_attn_kernel, has_y=has_y),
        grid_spec=pltpu.PrefetchScalarGridSpec(num_scalar_prefetch=2, grid=(nt,), in_specs=in_specs,
                                               out_specs=out_specs),
        out_shape=out_shape,
        compiler_params=pltpu.CompilerParams(dimension_semantics=("arbitrary",), vmem_limit_bytes=VMEM_LIMIT),
        name="pre_attn",
    )(modidx, tabidx, *args)
    if not has_y:
        outs = [x] + list(outs)
    return outs


def _softmax_step(qs, k, m_ref, c):
    s = _dot_nt(qs, k)
    m_old = m_ref[c]
    m_new = jnp.maximum(m_old, jnp.broadcast_to(jnp.max(s, axis=-1, keepdims=True), m_old.shape))
    alpha = jnp.exp2(m_old - m_new)
    p = jnp.exp2(s - jnp.concatenate([m_new] * (k.shape[0] // LANES), axis=1))
    m_ref[c] = m_new
    return alpha, p


def _mla_attn_kernel(*refs, has_ctx, n_chunks, tk):
    if has_ctx:
        q_ref, kc_ref, k_ref, wuv_ref, o_ref, m_ref, l_ref, acc_ref = refs
    else:
        q_ref, k_ref, wuv_ref, o_ref, m_ref, l_ref, acc_ref = refs
        kc_ref = None
    tq = q_ref.shape[0]
    m_ref[...] = jnp.full(m_ref.shape, NEG_BIG, F32)
    l_ref[...] = jnp.zeros(l_ref.shape, F32)
    acc_ref[...] = jnp.zeros(acc_ref.shape, F32)
    q = q_ref[...]
    qs = [jnp.concatenate([q[:, h * 256:(h + 1) * 256] for h in (2 * c, 2 * c + 1)], axis=0) for c in range(2)]

    def process(kblk):
        for c in range(2):
            alpha, p = _softmax_step(qs[c], kblk, m_ref, c)
            l_ref[c] = alpha * l_ref[c] + jnp.broadcast_to(jnp.sum(p, axis=-1, keepdims=True), alpha.shape)
            acc_ref[c] = alpha * acc_ref[c] + _dot(p.astype(BF16), kblk[:, 0:KV_LORA])

    if has_ctx:
        process(kc_ref[0])

    def body(j, carry):
        process(k_ref[pl.ds(pl.multiple_of(j * tk, tk), tk), :])
        return carry

    lax.fori_loop(0, n_chunks, body, 0)
    out = jnp.zeros(o_ref.shape, F32)
    for h in range(MLA_HEADS):
        rows = slice((h % 2) * tq, (h % 2 + 1) * tq)
        o_lat = acc_ref[h // 2, rows, :] / l_ref[h // 2, rows, :]
        out = out + _dot(o_lat.astype(BF16), wuv_ref[h])
    o_ref[...] = out.astype(o_ref.dtype)


def _gqa_attn_kernel(*refs, has_ctx, n_chunks, tk):
    if has_ctx:
        q_ref, kc_ref, vc_ref, k_ref, v_ref, o_ref, m_ref, acc_ref = refs
    else:
        q_ref, k_ref, v_ref, o_ref, m_ref, acc_ref = refs
        kc_ref = vc_ref = None
    tq = q_ref.shape[0]
    m_ref[...] = jnp.full(m_ref.shape, NEG_BIG, F32)
    acc_ref[...] = jnp.zeros(acc_ref.shape, F32)
    q = q_ref[...]
    lane = lax.broadcasted_iota(I32, (tq, LANES), 1)
    low = lane < HEAD_DIM
    qs = []
    for g in range(GQA_KV_HEADS):
        parts = []
        for h in range(g * GQA_GROUP, (g + 1) * GQA_GROUP):
            pair = q[:, (h // 2) * LANES:(h // 2 + 1) * LANES]
            keep = low if h % 2 == 0 else jnp.logical_not(low)
            parts.append(jnp.where(keep, pair, jnp.zeros_like(pair)))
        qs.append(jnp.concatenate(parts, axis=0))

    def process(kblk, vblk):
        for g in range(GQA_KV_HEADS):
            alpha, p = _softmax_step(qs[g], kblk[:, g * LANES:(g + 1) * LANES], m_ref, g)
            acc_ref[g] = alpha * acc_ref[g] + _dot(p.astype(BF16), vblk[:, g * LANES:(g + 1) * LANES])

    if has_ctx:
        process(kc_ref[0], vc_ref[0])

    def body(j, carry):
        sl = pl.ds(pl.multiple_of(j * tk, tk), tk)
        process(k_ref[sl, :], v_ref[sl, :])
        return carry

    lax.fori_loop(0, n_chunks, body, 0)
    for j in range(GQA_HEADS // 2):
        outs = []
        for h in (2 * j, 2 * j + 1):
            a = acc_ref[h // GQA_GROUP, (h % GQA_GROUP) * tq:(h % GQA_GROUP + 1) * tq, :]
            outs.append(a / jnp.where(low, pltpu.roll(a, HEAD_DIM, 1), a))
        o_ref[:, j * LANES:(j + 1) * LANES] = jnp.where(low, outs[0], pltpu.roll(outs[1], HEAD_DIM, 1)).astype(
            o_ref.dtype)


def _attention(kind, q, k, v, ctx, wuv, *, n_seq, seq_len, tile_offset):
    has_ctx = ctx is not None
    tq = min(ATTN_TQ, seq_len)
    tk = min(ATTN_TK, seq_len)
    nq = seq_len // tq
    qw = q.shape[1]
    ow = 256 if kind == 'mla' else 512
    heads = MLA_HEADS if kind == 'mla' else GQA_HEADS
    qmap = lambda b, i: (tile_offset * nq + b * nq + i, 0)
    kmap = lambda b, i: (tile_offset + b, 0)
    cmap = lambda b, i: (b, 0, 0)
    in_specs = [pl.BlockSpec((tq, qw), qmap)]
    args = [q]
    if kind == 'mla':
        if has_ctx:
            in_specs.append(pl.BlockSpec((1,) + ctx[0].shape[1:], cmap))
            args.append(ctx[0])
        in_specs += [pl.BlockSpec((seq_len, 256), kmap), pl.BlockSpec((MLA_HEADS, 128, 256), lambda b, i: (0, 0, 0))]
        args += [k, wuv]
        body = _mla_attn_kernel
    else:
        if has_ctx:
            in_specs += [pl.BlockSpec((1,) + ctx[0].shape[1:], cmap), pl.BlockSpec((1,) + ctx[1].shape[1:], cmap)]
            args += [ctx[0], ctx[1]]
        in_specs += [pl.BlockSpec((seq_len, 256), kmap), pl.BlockSpec((seq_len, 256), kmap)]
        args += [k, v]
        body = _gqa_attn_kernel
    return pl.pallas_call(
        functools.partial(body, has_ctx=has_ctx, n_chunks=seq_len // tk, tk=tk),
        grid=(n_seq, nq),
        in_specs=in_specs,
        out_specs=pl.BlockSpec((tq, ow), lambda b, i: (b * nq + i, 0)),
        out_shape=jax.ShapeDtypeStruct((n_seq * seq_len, ow), BF16),
        scratch_shapes=[pltpu.VMEM((2, heads // 2 * tq, LANES), F32)] * (3 if kind == 'mla' else 2),
        compiler_params=pltpu.CompilerParams(dimension_semantics=("arbitrary", "arbitrary"),
                                             vmem_limit_bytes=VMEM_LIMIT),
        name=kind + ("_attn_latent" if has_ctx else "_attn_context"),
    )(*args)


def _gla_kernel(blk_ref, seq_ref, first_ref, *refs, reverse):
    del blk_ref, seq_ref
    if reverse:
        gin_ref, s0_ref, he_ref, of_ref, gout_ref, ones_ref, o_ref, sout_ref, st_ref = refs
    else:
        gin_ref, s0_ref, he_ref, o_ref, sout_ref, st_ref = refs
    i = pl.program_id(0)

    @pl.when(first_ref[i] == 1)
    def _():
        st_ref[...] = s0_ref[0]

    R, S = TOKEN_TILE, GLA_STEP
    ns = R // S
    q = gin_ref[:, 0:128]
    k = gin_ref[:, 128:256]
    g = gin_ref[:, 384:512] if reverse else gin_ref[:, 256:384]
    v = gin_ref[:, 512:768]
    pos = lax.broadcasted_iota(I32, (R, 128), 0) % S
    b = g
    for s in (1, 2, 4, 8):
        if reverse:
            b = b + jnp.where(pos < S - s, pltpu.roll(b, R - s, 0), 0.0)
        else:
            b = b + jnp.where(pos >= s, pltpu.roll(b, s, 0), 0.0)

    def step_row(a, j):
        w = a.shape[-1]
        a3 = a.reshape(ns, S, w)
        return jnp.broadcast_to(a3[:, j:j + 1, :], (ns, S, w)).reshape(R, w)

    b_edge = step_row(b, 0 if reverse else S - 1)
    qt = (q * jnp.exp(b)).astype(BF16)
    kt = (k * jnp.exp(b_edge - b)).astype(BF16)
    d_edge = jnp.exp(b_edge)
    vb = v.astype(BF16)
    he = he_ref[...]

    o_intra = jnp.zeros((R, 256), F32)
    for j in range(S):
        cond = (pos <= j) if reverse else (pos >= j)
        bj, kj, vj = step_row(b, j), step_row(k, j), step_row(v, j)
        t = jnp.where(cond, q * kj * jnp.exp(jnp.where(cond, b - bj, 0.0)), 0.0)
        o_intra = o_intra + _dot(t.astype(BF16), he) * vj

    own_head = (lax.broadcasted_iota(I32, (256, 128), 0) // GLA_DV) == (lax.broadcasted_iota(I32, (256, 128), 1) // GLA_DK)
    st = st_ref[...]
    parts = [None] * ns
    for t in (range(ns - 1, -1, -1) if reverse else range(ns)):
        rows = slice(t * S, (t + 1) * S)
        parts[t] = _dot_nt(qt[rows], st.astype(BF16)) + o_intra[rows]
        ut = _dot_tn(vb[rows], kt[rows])
        st = d_edge[t * S:t * S + 1, :] * st + jnp.where(own_head, ut, 0.0)
    st_ref[...] = st
    sout_ref[0] = st
    o = jnp.concatenate(parts, axis=0)
    if reverse:
        o = o + of_ref[...]
        ms = _split_dot(o * o, ones_ref[...]) * (1.0 / GLA_DV)
        on = o * lax.rsqrt(ms + EPS) * gout_ref[...]
        lg = gin_ref[:, 768:1024]
        o_ref[...] = (on * (lg * jax.nn.sigmoid(lg))).astype(o_ref.dtype)
    else:
        o_ref[...] = o


def _gla(gin, s0, o_f, lw, order, *, reverse):
    blk, seq, first = order
    N = gin.shape[0]
    tm = TOKEN_TILE
    n_seq = s0.shape[0]
    tile = lambda i, b, s, f: (b[i], 0)
    state = lambda i, b, s, f: (s[i], 0, 0)
    const2 = lambda i, b, s, f: (0, 0)
    in_specs = [pl.BlockSpec((tm, 1024), tile), pl.BlockSpec((1, 256, 128), state), pl.BlockSpec((128, 256), const2)]
    args = [gin, s0, lw['head_expand']]
    if reverse:
        in_specs += [pl.BlockSpec((tm, 256), tile), pl.BlockSpec((1, 256), const2), pl.BlockSpec((256, 256), const2)]
        args += [o_f, lw['g_gla_out'], lw['ones256']]
    return pl.pallas_call(
        functools.partial(_gla_kernel, reverse=reverse),
        grid_spec=pltpu.PrefetchScalarGridSpec(
            num_scalar_prefetch=3, grid=(N // tm,), in_specs=in_specs,
            out_specs=[pl.BlockSpec((tm, 256), tile), pl.BlockSpec((1, 256, 128), state)],
            scratch_shapes=[pltpu.VMEM((256, 128), F32)]),
        out_shape=[jax.ShapeDtypeStruct((N, 256), BF16 if reverse else F32),
                   jax.ShapeDtypeStruct((n_seq, 256, 128), F32)],
        compiler_params=pltpu.CompilerParams(dimension_semantics=("arbitrary",), vmem_limit_bytes=VMEM_LIMIT),
        name="gla_bwd" if reverse else "gla_fwd",
    )(blk, seq, first, *args)


def _post_attn_kernel(modidx_ref, x_ref, omc_ref, oml_ref, ogc_ref, ogl_ref, ol_ref, mod_ref, wout_ref, gffn_ref,
                      wrh_ref, wrl_ref, br_ref, tri_ref, x1_ref, h2_ref, ti_ref, tg_ref, cnt_ref, carry_ref, *,
                      n_ctx_tiles):
    del modidx_ref
    mod = mod_ref[0]
    gate1, shift2, scale2 = mod[2:3, :], mod[3:4, :], mod[4:5, :]
    is_ctx = pl.program_id(0) < n_ctx_tiles
    om = jnp.where(is_ctx, omc_ref[...], oml_ref[...])
    og = jnp.where(is_ctx, ogc_ref[...], ogl_ref[...])
    o = _dot(om, wout_ref[0:256, :]) + _dot(og, wout_ref[256:768, :]) + _dot(ol_ref[...], wout_ref[768:1024, :])
    x1 = x_ref[...] + gate1 * o
    x1_ref[...] = x1
    xn = x1 * lax.rsqrt(jnp.mean(x1 * x1, axis=-1, keepdims=True) + EPS) * gffn_ref[...]
    h2 = xn * (1.0 + scale2) + shift2
    _to_token_tiles(h2_ref, h2)
    hi = h2.astype(BF16)
    lo = (h2 - hi.astype(F32)).astype(BF16)
    wrh = wrh_ref[...]
    logits = _dot(hi, wrh) + _dot(hi, wrl_ref[...]) + _dot(lo, wrh) + br_ref[...]

    lane = lax.broadcasted_iota(I32, logits.shape, 1).astype(F32)
    work = logits
    vals, idxs = [], []
    for _ in range(TOP_K):
        m = jnp.max(work, axis=-1, keepdims=True)
        idx = jnp.min(jnp.where(work == m, lane, float(LANES)), axis=-1, keepdims=True)
        vals.append(m)
        idxs.append(idx)
        work = jnp.where(lane == idx, NEG_BIG * 4.0, work)
    es = [jnp.exp(vv - vals[0]) for vv in vals]
    denom = es[0] + es[1] + es[2] + es[3]
    ti = jnp.zeros(logits.shape, F32)
    tg = jnp.zeros(logits.shape, F32)
    for r in range(TOP_K):
        ti = jnp.where(lane == float(r), idxs[r], ti)
        tg = jnp.where(lane == float(r), es[r] / denom, tg)
    tg_ref[...] = tg

    @pl.when(pl.program_id(0) == 0)
    def _():
        carry_ref[...] = jnp.zeros(carry_ref.shape, F32)

    picks = [lane == idxs[r] for r in range(TOP_K)]
    onehot = sum(jnp.where(pk, 1.0, 0.0) for pk in picks)
    carry = carry_ref[...]
    before = carry + _dot(tri_ref[...], onehot.astype(BF16))
    for r in range(TOP_K):
        rank = jnp.sum(jnp.where(picks[r], before, 0.0), axis=-1, keepdims=True)
        ti = jnp.where(lane == float(TOP_K + r), rank, ti)
    ti_ref[...] = ti.astype(I32)
    carry = carry + jnp.sum(onehot, axis=0, keepdims=True)
    carry_ref[...] = carry
    cnt_ref[...] = jnp.broadcast_to(carry, cnt_ref.shape)


def _post_attn(x, o_mla, o_gqa, o_gla, mod, modidx, lw, n_ctx_tiles):
    N, D = x.shape
    tm = TOKEN_TILE
    row = lambda i, mi: (i, 0)
    ctx_row = lambda i, mi: (jnp.minimum(i, n_ctx_tiles - 1), 0)
    lat_row = lambda i, mi: (jnp.maximum(i - n_ctx_tiles, 0), 0)
    const2 = lambda i, mi: (0, 0)
    return pl.pallas_call(
        functools.partial(_post_attn_kernel, n_ctx_tiles=n_ctx_tiles),
        grid_spec=pltpu.PrefetchScalarGridSpec(
            num_scalar_prefetch=1, grid=(N // tm,),
            in_specs=[pl.BlockSpec((tm, D), row), pl.BlockSpec((tm, 256), ctx_row), pl.BlockSpec((tm, 256), lat_row),
                      pl.BlockSpec((tm, 512), ctx_row), pl.BlockSpec((tm, 512), lat_row),
                      pl.BlockSpec((tm, 256), row), pl.BlockSpec((1, 6, D), lambda i, mi: (mi[i], 0, 0)),
                      pl.BlockSpec((D, D), const2), pl.BlockSpec((1, D), const2),
                      pl.BlockSpec((D, LANES), const2), pl.BlockSpec((D, LANES), const2),
                      pl.BlockSpec((1, LANES), const2), pl.BlockSpec((tm, tm), const2)],
            out_specs=[pl.BlockSpec((tm, D), row), pl.BlockSpec((tm * SUBLANES, LANES), row),
                       pl.BlockSpec((tm, LANES), row), pl.BlockSpec((tm, LANES), row),
                       pl.BlockSpec((SUBLANES, LANES), const2)],
            scratch_shapes=[pltpu.VMEM((1, LANES), F32)]),
        out_shape=[jax.ShapeDtypeStruct((N, D), F32), jax.ShapeDtypeStruct((N * SUBLANES, LANES), F32),
                   jax.ShapeDtypeStruct((N, LANES), I32), jax.ShapeDtypeStruct((N, LANES), F32),
                   jax.ShapeDtypeStruct((SUBLANES, LANES), F32)],
        compiler_params=pltpu.CompilerParams(dimension_semantics=("arbitrary",), vmem_limit_bytes=VMEM_LIMIT),
        name="post_attn",
    )(modidx, x, o_mla[0], o_mla[1], o_gqa[0], o_gqa[1], o_gla, mod, lw['w_out'], lw['g_ffn'], lw['w_router_hi'],
      lw['w_router_lo'], lw['b_router'], lw['tri'])


def _moe_kernel(be_ref, first_ref, valid_ref, src0_ref, src_ref, dst_ref, dstp_ref, h_hbm, wgu_ref, bgu_ref, wd_ref, bd_ref,
                y_hbm, wgu_bf, wd_bf, xbuf0, xbuf1, ybuf0, ybuf1, xsem0, xsem1, ysem0, ysem1):
    del be_ref
    i = pl.program_id(0)
    M = MOE_ROWS

    def gather(idx_ref, r, xb, sem):
        tok = idx_ref[0, 0, r]
        return pltpu.make_async_copy(h_hbm.at[pl.ds(pl.multiple_of(tok * SUBLANES, SUBLANES), SUBLANES), :],
                                     xb.at[pl.ds(r * SUBLANES, SUBLANES), :], sem)

    def scatter(idx_ref, r, yb, sem):
        row = idx_ref[0, 0, r]
        return pltpu.make_async_copy(yb.at[pl.ds(r * SUBLANES, SUBLANES), :],
                                     y_hbm.at[pl.ds(pl.multiple_of(row * SUBLANES, SUBLANES), SUBLANES), :], sem)

    def all_rows_in(xb, sem):
        return pltpu.make_async_copy(h_hbm.at[pl.ds(0, M * SUBLANES), :], xb, sem)

    def all_rows_out(yb, sem):
        return pltpu.make_async_copy(yb, y_hbm.at[pl.ds(0, M * SUBLANES), :], sem)

    @pl.when(i == 0)
    def _():
        ybuf0[...] = jnp.zeros(ybuf0.shape, F32)
        ybuf1[...] = jnp.zeros(ybuf1.shape, F32)

        def prime(r, carry):
            gather(src0_ref, r, xbuf0, xsem0).start()
            scatter(dstp_ref, r, ybuf0, ysem0).start()
            return carry
        lax.fori_loop(0, M, prime, 0)

    @pl.when(first_ref[i] == 1)
    def _():
        wgu_bf[...] = wgu_ref[...].astype(BF16)
        wd_bf[...] = wd_ref[...].astype(BF16)

    def step(xa, xsa, xo, xso, ya, ysa, yo, yso):
        all_rows_in(xa, xsa).wait()

        @pl.when(valid_ref[i] == 1)
        def _():
            for r in range(M):
                scatter(dst_ref, r, yo, yso).start(priority=1)
                gather(src_ref, r, xo, xso).start()
            gu = _dot(_from_token_tiles(xa, M).astype(BF16), wgu_bf[...]) + bgu_ref[...]
            gate = jnp.minimum(gu[:, :D_EXPERT], SWIGLU_LIMIT)
            up = jnp.clip(gu[:, D_EXPERT:], -SWIGLU_LIMIT, SWIGLU_LIMIT)
            act = gate * jax.nn.sigmoid(SWIGLU_ALPHA * gate) * (up + 1.0)
            y = _dot(act.astype(BF16), wd_bf[...]) + bd_ref[...]
            all_rows_out(ya, ysa).wait()
            _to_token_tiles(ya, y)

        @pl.when(valid_ref[i] == 0)
        def _():
            def move(r, carry):
                scatter(dst_ref, r, yo, yso).start(priority=1)
                gather(src_ref, r, xo, xso).start()
                return carry
            lax.fori_loop(0, M, move, 0)
            all_rows_out(ya, ysa).wait()

    @pl.when(i % 2 == 0)
    def _():
        step(xbuf0, xsem0, xbuf1, xsem1, ybuf0, ysem0, ybuf1, ysem1)

    @pl.when(i % 2 == 1)
    def _():
        step(xbuf1, xsem1, xbuf0, xsem0, ybuf1, ysem1, ybuf0, ysem0)

    @pl.when(i == pl.num_programs(0) - 1)
    def _():
        all_rows_out(ybuf0, ysem0).wait()
        all_rows_in(xbuf0, xsem0).wait()


def _moe_experts(h2t, sched, w_gate_up, b_gate_up, w_down, b_down, layer, n_out_rows):
    block_e, first, valid, src, dst, dst_prime = sched
    n_blocks = block_e.shape[0]
    D = D_MODEL
    M = MOE_ROWS
    L, E = w_gate_up.shape[:2]
    wmap = lambda i, be, fi, va: (layer, be[i], 0, 0)
    smem = lambda imap: pl.BlockSpec((1, 1, M), imap, memory_space=pltpu.SMEM)
    return pl.pallas_call(
        _moe_kernel,
        grid_spec=pltpu.PrefetchScalarGridSpec(
            num_scalar_prefetch=3, grid=(n_blocks,),
            in_specs=[smem(lambda i, be, fi, va: (0, 0, 0)), smem(lambda i, be, fi, va: (i + 1, 0, 0)),
                      smem(lambda i, be, fi, va: (i, 0, 0)), smem(lambda i, be, fi, va: (0, 0, 0)),
                      pl.BlockSpec(memory_space=pl.ANY),
                      pl.BlockSpec((None, None, D, 2 * D_EXPERT), wmap),
                      pl.BlockSpec((None, None, 1, 2 * D_EXPERT), wmap),
                      pl.BlockSpec((None, None, D_EXPERT, D), wmap),
                      pl.BlockSpec((None, None, 1, D), wmap)],
            out_specs=pl.BlockSpec(memory_space=pl.ANY),
            scratch_shapes=[pltpu.VMEM((D, 2 * D_EXPERT), BF16), pltpu.VMEM((D_EXPERT, D), BF16),
                            ] + [pltpu.VMEM((M * SUBLANES, LANES), F32)] * 4 + [pltpu.SemaphoreType.DMA(())] * 4),
        out_shape=jax.ShapeDtypeStruct((n_out_rows * SUBLANES, LANES), F32),
        compiler_params=pltpu.CompilerParams(dimension_semantics=("arbitrary",), vmem_limit_bytes=VMEM_LIMIT),
        name="moe_experts",
    )(block_e, first, valid, src, src, dst, dst_prime, h2t, w_gate_up, b_gate_up.reshape(L, E, 1, 2 * D_EXPERT), w_down,
      b_down.reshape(L, E, 1, D))


def _route(top_idx, rank, counts, n_blocks):
    N = top_idx.shape[0]
    padded = (counts + MOE_ROWS - 1) // MOE_ROWS * MOE_ROWS
    pad_end = jnp.cumsum(padded)
    pad_start = pad_end - padded
    dest = pad_start[top_idx] + rank
    M, K = MOE_ROWS, TOP_K
    asg = jnp.arange(N * K, dtype=I32)
    slot_asg = jnp.full((n_blocks * M,), -1, I32).at[dest.reshape(-1)].set(asg)
    real = slot_asg >= 0
    tok = jnp.where(real, slot_asg // K, 0)
    slot_id = jnp.arange(n_blocks * M, dtype=I32)
    spare = K * N + ((slot_id // M) % 2) * M + slot_id % M
    out_row = jnp.where(real, (slot_asg % K) * N + tok, spare)
    src = jnp.concatenate([tok.reshape(n_blocks, 1, M), jnp.zeros((1, 1, M), I32)], axis=0)
    lead = (K * N + M + jnp.arange(M, dtype=I32)).reshape(1, 1, M)
    dst = jnp.concatenate([lead, out_row.reshape(n_blocks, 1, M)], axis=0)
    dst_prime = (K * N + jnp.arange(M, dtype=I32)).reshape(1, 1, M)
    blk_start = jnp.arange(n_blocks, dtype=I32) * M
    valid = blk_start < pad_end[-1]
    block_e = jnp.minimum((blk_start[:, None] >= pad_end[None, :]).astype(I32).sum(axis=1), N_EXPERTS - 1)
    n_valid = pad_end[-1] // M
    block_e = block_e[jnp.minimum(jnp.arange(n_blocks, dtype=I32), n_valid - 1)]
    prev_e = jnp.concatenate([jnp.full((1,), -1, I32), block_e[:-1]])
    first = (valid & (block_e != prev_e)).astype(I32)
    return (block_e, first, valid.astype(I32), src, dst, dst_prime)


def _final_kernel(modidx_ref, x_ref, *refs):
    del modidx_ref
    y_refs, (tg_ref, pmod_ref, g_ref, o_ref) = refs[:TOP_K], refs[TOP_K:]
    x = x_ref[...] + pmod_ref[0, 5:6, :] * _combine_experts(y_refs, tg_ref)
    o_ref[...] = x * lax.rsqrt(jnp.mean(x * x, axis=-1, keepdims=True) + EPS) * g_ref[...]


def _final_norm(x, y, pmod, modidx, g_final):
    N, D = x.shape
    tm = TOKEN_TILE
    row = lambda i, mi: (i, 0)
    return pl.pallas_call(
        _final_kernel,
        grid_spec=pltpu.PrefetchScalarGridSpec(
            num_scalar_prefetch=1, grid=(N // tm,),
            in_specs=[pl.BlockSpec((tm, D), row)]
            + [pl.BlockSpec((tm * SUBLANES, LANES), functools.partial(lambda k, i, mi: (k * (N // tm) + i, 0), k))
               for k in range(TOP_K)]
            + [pl.BlockSpec((tm, LANES), row),
               pl.BlockSpec((1, 6, D), lambda i, mi: (mi[i], 0, 0)), pl.BlockSpec((1, D), lambda i, mi: (0, 0))],
            out_specs=pl.BlockSpec((tm, D), row)),
        out_shape=jax.ShapeDtypeStruct((N, D), F32),
        compiler_params=pltpu.CompilerParams(dimension_semantics=("arbitrary",), vmem_limit_bytes=VMEM_LIMIT),
        name="final_norm",
    )(modidx, x, *([y[0]] * TOP_K), y[1], pmod, g_final)


def _rope_tables(seq_len):
    pos = np.arange(seq_len)
    rowp = (pos // GRID_W).astype(np.float32)
    colp = (pos % GRID_W).astype(np.float32)

    def tables(rdim, copies):
        quarter = rdim // 4
        inv_freq = jnp.asarray(ROPE_THETA, F32) ** (-jnp.arange(quarter, dtype=F32) / quarter)
        ar = jnp.asarray(rowp)[:, None] * inv_freq[None, :]
        ac = jnp.asarray(colp)[:, None] * inv_freq[None, :]
        cos = jnp.concatenate([jnp.cos(ar), jnp.cos(ar), jnp.cos(ac), jnp.cos(ac)], axis=1)
        sin = jnp.concatenate([-jnp.sin(ar), jnp.sin(ar), -jnp.sin(ac), jnp.sin(ac)], axis=1)
        return jnp.tile(cos, (1, copies)), jnp.tile(sin, (1, copies))

    c64, s64 = tables(HEAD_DIM, 2)
    c32, s32 = tables(MLA_ROPE, 4)
    lat = jnp.concatenate([c64, s64, c32, s32], axis=1).reshape(seq_len // TOKEN_TILE, TOKEN_TILE, 512)
    ident = jnp.concatenate([jnp.ones((TOKEN_TILE, 128), F32), jnp.zeros((TOKEN_TILE, 128), F32)] * 2, axis=1)
    return jnp.concatenate([ident[None], lat], axis=0)


def _block_ones(n, blk):
    r = np.arange(n) // blk
    return jnp.asarray((r[:, None] == r[None, :]).astype(np.float32), BF16)


def kernel(x_prompt, x_sample, cache_mla_ckv, cache_mla_krope, cache_gqa_k, cache_gqa_v, state_gla, c, c_ctx, w_mod, b_mod, g_attn_norm, g_ffn_norm, w_in, g_q_a, w_uq, g_kv_a, w_ukv, g_q_head, g_k_head, w_gk_fwd, b_gk_fwd, w_gk_bwd, b_gk_bwd, g_gla_out, w_out, w_router, b_router, w_gate_up, b_gate_up, w_down, b_down, g_final):
    B, T, D = x_prompt.shape
    BD, TD, _ = x_sample.shape
    L = w_in.shape[0]
    P = cache_mla_ckv.shape[2]
    tm = TOKEN_TILE
    n_ctx = B * T
    N = n_ctx + BD * TD
    assert T == tm and TD % ATTN_TQ == 0 and n_ctx % TD == 0 and BD + 1 <= 16
    n_ctx_tiles = n_ctx // tm
    nt = N // tm
    tiles_per_lat = TD // tm

    tile_ids = np.arange(nt)
    lat_tile = np.maximum(tile_ids - n_ctx_tiles, 0)
    is_lat = tile_ids >= n_ctx_tiles
    modidx = jnp.asarray(np.where(is_lat, 1 + lat_tile // tiles_per_lat, 0), I32)
    tabidx = jnp.asarray(np.where(is_lat, 1 + lat_tile % tiles_per_lat, 0), I32)
    seq_of_tile = np.where(is_lat, B + lat_tile // tiles_per_lat, tile_ids)
    first_fwd = np.where(is_lat, lat_tile % tiles_per_lat == 0, True)
    last_fwd = np.where(is_lat, lat_tile % tiles_per_lat == tiles_per_lat - 1, True)
    order_fwd = (jnp.asarray(tile_ids, I32), jnp.asarray(seq_of_tile, I32), jnp.asarray(first_fwd, I32))
    rev = tile_ids[::-1]
    order_bwd = (jnp.asarray(rev, I32), jnp.asarray(seq_of_tile[rev], I32), jnp.asarray(last_fwd[rev], I32))
    n_seq = B + BD

    cvecs = jnp.zeros((16, D), F32).at[0].set(c_ctx).at[1:1 + BD].set(c)
    mods = _modulation(cvecs, w_mod, b_mod).reshape(L, 16, 6, D)

    o = np.cumsum([0, Q_LORA, KV_LORA, MLA_ROPE, 512, 128, 128, 128, 128, 256, 16, 16, 256])
    seg = lambda j: w_in[:, :, o[j]:o[j + 1]]
    w_in_r = jnp.concatenate(
        [seg(0), seg(1), seg(3), seg(4), seg(5), seg(6), seg(7), seg(8), seg(11), seg(2), seg(2), seg(2), seg(2),
         seg(9), seg(10), jnp.zeros((L, D, 128 - 2 * GLA_GATE_RANK), F32)], axis=-1).astype(BF16)
    uq = w_uq.reshape(L, Q_LORA, MLA_HEADS, MLA_QK)
    ukv = w_ukv.reshape(L, KV_LORA, MLA_HEADS, MLA_NOPE + MLA_V)
    w_comb = _fold_q_weights(uq[..., :MLA_NOPE].transpose(0, 2, 1, 3), ukv[..., :MLA_NOPE].transpose(0, 2, 1, 3))
    q_pad = jnp.zeros((L, Q_LORA, 256 - KV_LORA - MLA_ROPE), F32)
    w_q2 = (jnp.concatenate([part for h in range(MLA_HEADS)
                             for part in (w_comb[:, h], uq[:, :, h, MLA_NOPE:], q_pad)], axis=-1)
            * (MLA_QK ** -0.5 * LOG2E)).astype(BF16)
    w_uv = ukv[..., MLA_NOPE:].transpose(0, 2, 1, 3)
    eye_h = jnp.eye(MLA_HEADS, dtype=F32)
    wuv_pad = jnp.einsum('lhkv,hg->lhkgv', w_uv, eye_h).reshape(L, MLA_HEADS, KV_LORA, MLA_HEADS * MLA_V).astype(BF16)
    w_gate = jnp.zeros((L, 128, 256), F32).at[:, 0:16, 0:128].set(w_gk_fwd).at[:, 16:32, 128:256].set(w_gk_bwd)
    w_gate = w_gate.astype(BF16)
    b_gate = jnp.concatenate([b_gk_fwd, b_gk_bwd], axis=-1)
    w_out_bf = w_out.astype(BF16)
    wr_pad = jnp.pad(w_router, ((0, 0), (0, 0), (0, LANES - N_EXPERTS)))
    wr_hi = wr_pad.astype(BF16)
    wr_lo = (wr_pad - wr_hi.astype(F32)).astype(BF16)
    br_pad = jnp.pad(b_router, ((0, 0), (0, LANES - N_EXPERTS)), constant_values=NEG_BIG)
    ones512 = _block_ones(512, HEAD_DIM)
    tri = jnp.asarray((np.arange(tm)[None, :] < np.arange(tm)[:, None]).astype(np.float32), BF16)
    head_expand = jnp.asarray((np.arange(128)[:, None] // GLA_DK == np.arange(256)[None, :] // GLA_DV)
                              .astype(np.float32), BF16)
    tab = _rope_tables(TD)

    x = jnp.concatenate([x_prompt.reshape(n_ctx, D), x_sample.reshape(BD * TD, D)], axis=0)
    y = None
    n_blocks = (N * TOP_K) // MOE_ROWS + N_EXPERTS
    assert (N * TOP_K) % MOE_ROWS == 0 and n_blocks % 2 == 0
    caches, states = [], []
    for l in range(L):
        lw = dict(g_attn=g_attn_norm[l][None], w_in_r=w_in_r[l], g_q_a=g_q_a[l][None], w_q2=w_q2[l],
                  g_kv_a=g_kv_a[l][None], g_qh=jnp.tile(g_q_head[l], GQA_HEADS)[None],
                  g_kh=jnp.tile(g_k_head[l], GQA_KV_HEADS)[None], ones512=ones512, w_gate=w_gate[l],
                  b_gate=b_gate[l][None], head_expand=head_expand, g_gla_out=jnp.tile(g_gla_out[l], GLA_HEADS)[None],
                  ones256=ones512[:256, :256], w_out=w_out_bf[l], g_ffn=g_ffn_norm[l][None], tri=tri,
                  w_router_hi=wr_hi[l], w_router_lo=wr_lo[l], b_router=br_pad[l][None])
        pmod = mods[l - 1] if l > 0 else None
        x, qm, kvc, qg, kdup, vdup, gin, cache = _pre_attn(x, y, pmod, mods[l], tab, modidx, tabidx, lw, n_ctx_tiles)
        caches.append(cache[:n_ctx])

        mla_ctx = jnp.concatenate([cache_mla_ckv[:, l]] + [cache_mla_krope[:, l]] * 4, axis=-1).astype(BF16)
        ck = cache_gqa_k[:, l]
        cv = cache_gqa_v[:, l]
        k_ctx = jnp.concatenate([ck[:, :, 0], ck[:, :, 0], ck[:, :, 1], ck[:, :, 1]], axis=-1).astype(BF16)
        ones_v = jnp.ones(cv.shape[:2] + (HEAD_DIM,), F32)
        v_ctx = jnp.concatenate([cv[:, :, 0], ones_v, cv[:, :, 1], ones_v], axis=-1).astype(BF16)
        o_mla = (_attention('mla', qm, kvc, None, None, wuv_pad[l], n_seq=B, seq_len=T, tile_offset=0),
                 _attention('mla', qm, kvc, None, (mla_ctx,), wuv_pad[l], n_seq=BD, seq_len=TD,
                            tile_offset=n_ctx // TD))
        o_gqa = (_attention('gqa', qg, kdup, vdup, None, None, n_seq=B, seq_len=T, tile_offset=0),
                 _attention('gqa', qg, kdup, vdup, (k_ctx, v_ctx), None, n_seq=BD, seq_len=TD,
                            tile_offset=n_ctx // TD))

        eye_g = jnp.eye(GLA_HEADS, dtype=F32)
        st_lat = jnp.einsum('bshde,hg->bshegd', state_gla[:, l], eye_g).reshape(BD, 2, 256, 128)
        zeros_ctx = jnp.zeros((B, 256, 128), F32)
        o_f, s_f = _gla(gin, jnp.concatenate([zeros_ctx, st_lat[:, 0]], axis=0), None, lw, order_fwd, reverse=False)
        o_gla, s_b = _gla(gin, jnp.concatenate([zeros_ctx, st_lat[:, 1]], axis=0), o_f, lw, order_bwd, reverse=True)
        states.append((s_f[:B], s_b[:B]))

        x1, h2, ti, tg, cnt = _post_attn(x, o_mla, o_gqa, o_gla, mods[l], modidx, lw, n_ctx_tiles)
        sched = _route(ti[:, :TOP_K], ti[:, TOP_K:2 * TOP_K], cnt[0, :N_EXPERTS].astype(I32), n_blocks)
        y = (_moe_experts(h2, sched, w_gate_up, b_gate_up, w_down, b_down, l, TOP_K * N + 2 * MOE_ROWS), tg)
        x = x1

    out = _final_norm(x, y, mods[L - 1], modidx, g_final[None])
    y_prompt = out[:n_ctx].reshape(B, T, D)
    y_sample = out[n_ctx:].reshape(BD, TD, D)
    cache_all = jnp.stack([cc.reshape(B, T, 512) for cc in caches], axis=1)
    new_ckv = cache_all[..., 0:128]
    new_k = cache_all[..., 128:256].reshape(B, L, T, GQA_KV_HEADS, HEAD_DIM)
    new_v = cache_all[..., 256:384].reshape(B, L, T, GQA_KV_HEADS, HEAD_DIM)
    new_krope = cache_all[..., 384:384 + MLA_ROPE]

    def unpack_state(st):
        s5 = st.reshape(B, GLA_HEADS, GLA_DV, GLA_HEADS, GLA_DK)
        diag = jnp.stack([s5[:, h, :, h, :] for h in range(GLA_HEADS)], axis=1)
        return diag.transpose(0, 1, 3, 2)

    new_state = jnp.stack([jnp.stack([unpack_state(sf), unpack_state(sb)], axis=1) for sf, sb in states], axis=1)
    return (y_prompt, y_sample, new_ckv, new_krope, new_k, new_v, new_state)
```

```python
import functools

import jax
import jax.numpy as jnp
import numpy as np
from jax import lax
from jax.experimental import pallas as pl
from jax.experimental.pallas import tpu as pltpu

F32 = jnp.float32
BF16 = jnp.bfloat16
I32 = jnp.int32

D_MODEL = 1024
GRID_W = 64
ROPE_THETA = 10000.0
EPS = 1e-6
MLA_HEADS = 4
Q_LORA = 256
KV_LORA = 128
MLA_NOPE = 64
MLA_ROPE = 32
MLA_V = 64
MLA_QK = MLA_NOPE + MLA_ROPE
GQA_HEADS = 8
GQA_KV_HEADS = 2
GQA_GROUP = GQA_HEADS // GQA_KV_HEADS
HEAD_DIM = 64
GLA_HEADS = 4
GLA_DK = 32
GLA_DV = 64
GLA_GATE_RANK = 16
GLA_GATE_NORM = 16.0
N_EXPERTS = 32
TOP_K = 4
D_EXPERT = D_MODEL
SWIGLU_LIMIT = 7.0
SWIGLU_ALPHA = 1.702

LANES = 128
SUBLANES = 8
TOKEN_TILE = 256
GLA_STEP = 16
MOE_ROWS = 512
ATTN_TQ = 512
ATTN_TK = 512
VMEM_LIMIT = 56 * 1024 * 1024
NEG_BIG = -1e30
LOG2E = 1.4426950408889634

C_CQ, C_CKV, C_GQ, C_GK, C_GV = 0, 256, 384, 896, 1024
C_LQ, C_LK, C_LV, C_LG, C_KR, C_GATE = 1152, 1280, 1408, 1664, 1920, 2048
IN_COLS = 2176


def _dot(a, b):
    return jnp.dot(a, b, preferred_element_type=F32)


def _dot_nt(a, b):
    return lax.dot_general(a, b, (((1,), (1,)), ((), ())), preferred_element_type=F32)


def _dot_tn(a, b):
    return lax.dot_general(a, b, (((0,), (0,)), ((), ())), preferred_element_type=F32)


def _split_dot(x, ones_bf):
    hi = x.astype(BF16)
    lo = (x - hi.astype(F32)).astype(BF16)
    return _dot(hi, ones_bf) + _dot(lo, ones_bf)


def _rope(x, cos, sin_signed, quarter):
    width = x.shape[-1]
    lane = lax.broadcasted_iota(I32, x.shape, 1)
    first = (lane % (2 * quarter)) < quarter
    partner = jnp.where(first, pltpu.roll(x, width - quarter, 1), pltpu.roll(x, quarter, 1))
    return x * cos + partner * sin_signed


def _mod_kernel(c_ref, w_ref, b_ref, o_ref):
    c = c_ref[...]
    a = c * jax.nn.sigmoid(c)
    o_ref[0] = jnp.dot(a, w_ref[0], precision=lax.Precision.HIGHEST, preferred_element_type=F32) + b_ref[0]


def _modulation(cvecs, w_mod, b_mod):
    L, D, D6 = w_mod.shape
    tn = 1536
    return pl.pallas_call(
        _mod_kernel,
        grid=(L, D6 // tn),
        in_specs=[pl.BlockSpec((16, D), lambda l, j: (0, 0)),
                  pl.BlockSpec((1, D, tn), lambda l, j: (l, 0, j)),
                  pl.BlockSpec((1, 1, tn), lambda l, j: (l, 0, j))],
        out_specs=pl.BlockSpec((1, 16, tn), lambda l, j: (l, 0, j)),
        out_shape=jax.ShapeDtypeStruct((L, 16, D6), F32),
        compiler_params=pltpu.CompilerParams(dimension_semantics=("arbitrary", "arbitrary"),
                                             vmem_limit_bytes=VMEM_LIMIT),
        name="modulation",
    )(cvecs, w_mod, b_mod.reshape(L, 1, D6))


def _fold_kernel(a_ref, b_ref, o_ref):
    o_ref[0, 0] = lax.dot_general(a_ref[0, 0], b_ref[0, 0], (((1,), (1,)), ((), ())),
                                  precision=lax.Precision.HIGHEST, preferred_element_type=F32)


def _fold_q_weights(wq_nope, wk_nope):
    L, H, A, K = wq_nope.shape
    B = wk_nope.shape[2]
    return pl.pallas_call(
        _fold_kernel,
        grid=(L, H),
        in_specs=[pl.BlockSpec((1, 1, A, K), lambda l, h: (l, h, 0, 0)),
                  pl.BlockSpec((1, 1, B, K), lambda l, h: (l, h, 0, 0))],
        out_specs=pl.BlockSpec((1, 1, A, B), lambda l, h: (l, h, 0, 0)),
        out_shape=jax.ShapeDtypeStruct((L, H, A, B), F32),
        name="fold_q_weights",
    )(wq_nope, wk_nope)


def _from_token_tiles(ref, rows):
    return jnp.concatenate([ref[pl.ds(c, rows, stride=SUBLANES), :] for c in range(D_MODEL // LANES)], axis=1)


def _to_token_tiles(ref, x):
    for c in range(D_MODEL // LANES):
        ref[pl.ds(c, x.shape[0], stride=SUBLANES), :] = x[:, c * LANES:(c + 1) * LANES]


def _combine_experts(y_refs, tg_ref):
    tg = tg_ref[...]
    acc = None
    for k in range(TOP_K):
        term = tg[:, k:k + 1] * _from_token_tiles(y_refs[k], tg.shape[0])
        acc = term if acc is None else acc + term
    return acc


def _pre_attn_kernel(modidx_ref, tabidx_ref, *refs, has_y):
    del modidx_ref, tabidx_ref
    if has_y:
        x_ref, y_ref, tg_ref, pmod_ref = refs[0], refs[1:1 + TOP_K], refs[1 + TOP_K], refs[2 + TOP_K]
        refs = refs[3 + TOP_K:]
    else:
        x_ref = refs[0]
        refs = refs[1:]
    (mod_ref, tab_ref, gattn_ref, win_ref, gqa_ref, wq2_ref, gkva_ref, gqh_ref, gkh_ref, ones_ref,
     wg_ref, bg_ref) = refs[:12]
    outs = refs[12:]
    if has_y:
        xres_ref = outs[0]
        outs = outs[1:]
    qm_ref, kvc_ref, qg_ref, kdup_ref, vdup_ref, gla_ref, cache_ref = outs

    x = x_ref[...]
    if has_y:
        x = x + pmod_ref[0, 5:6, :] * _combine_experts(y_ref, tg_ref)
        xres_ref[...] = x
    mod = mod_ref[0]
    shift1, scale1 = mod[0:1, :], mod[1:2, :]
    xn = x * lax.rsqrt(jnp.mean(x * x, axis=-1, keepdims=True) + EPS) * gattn_ref[...]
    h = xn * (1.0 + scale1) + shift1
    p = _dot(h.astype(BF16), win_ref[...])

    tab = tab_ref[0]
    cos64, sin64 = tab[:, 0:128], tab[:, 128:256]
    cos32, sin32 = tab[:, 256:384], tab[:, 384:512]
    ones = ones_ref[...]

    cq = p[:, C_CQ:C_CQ + Q_LORA]
    cqn = cq * lax.rsqrt(jnp.mean(cq * cq, axis=-1, keepdims=True) + EPS) * gqa_ref[...]
    qm = _dot(cqn.astype(BF16), wq2_ref[...])
    q_rope = _rope(jnp.concatenate([qm[:, h * 256 + 128:(h + 1) * 256] for h in range(MLA_HEADS)], axis=1),
                   jnp.concatenate([cos32] * MLA_HEADS, axis=1), jnp.concatenate([sin32] * MLA_HEADS, axis=1),
                   MLA_ROPE // 4)
    for h in range(MLA_HEADS):
        qm_ref[:, h * 256:h * 256 + 128] = qm[:, h * 256:h * 256 + 128].astype(BF16)
        qm_ref[:, h * 256 + 128:(h + 1) * 256] = q_rope[:, h * 128:(h + 1) * 128].astype(BF16)
    ckv_raw = p[:, C_CKV:C_CKV + KV_LORA]
    ckv = ckv_raw * lax.rsqrt(jnp.mean(ckv_raw * ckv_raw, axis=-1, keepdims=True) + EPS) * gkva_ref[...]
    kr4 = p[:, C_KR:C_KR + 128]
    kr4_rot = _rope(kr4, cos32, sin32, MLA_ROPE // 4)
    kvc_ref[:, 0:128] = ckv.astype(BF16)
    kvc_ref[:, 128:256] = kr4_rot.astype(BF16)

    gq = p[:, C_GQ:C_GQ + 512]
    ms_q = _split_dot(gq * gq, ones) * (1.0 / HEAD_DIM)
    qn = gq * lax.rsqrt(ms_q + EPS) * gqh_ref[...]
    cos_q = jnp.concatenate([cos64] * 4, axis=1)
    sin_q = jnp.concatenate([sin64] * 4, axis=1)
    qg = _rope(qn, cos_q, sin_q, HEAD_DIM // 4) * (HEAD_DIM ** -0.5 * LOG2E)
    qg_ref[...] = qg.astype(BF16)
    gk = p[:, C_GK:C_GK + 128]
    ms_k = _split_dot(gk * gk, ones[0:128, 0:128]) * (1.0 / HEAD_DIM)
    kn = gk * lax.rsqrt(ms_k + EPS) * gkh_ref[...]
    kg = _rope(kn, cos64, sin64, HEAD_DIM // 4)
    gv = p[:, C_GV:C_GV + 128]
    lane = lax.broadcasted_iota(I32, kg.shape, 1)
    low = lane < HEAD_DIM
    kg_sw = pltpu.roll(kg, HEAD_DIM, 1)
    gv_sw = pltpu.roll(gv, HEAD_DIM, 1)
    kdup_ref[:, 0:128] = jnp.where(low, kg, kg_sw).astype(BF16)
    kdup_ref[:, 128:256] = jnp.where(low, kg_sw, kg).astype(BF16)
    vdup_ref[:, 0:128] = jnp.where(low, gv, 1.0).astype(BF16)
    vdup_ref[:, 128:256] = jnp.where(low, gv_sw, 1.0).astype(BF16)

    z = _dot(p[:, C_GATE:C_GATE + 128].astype(BF16), wg_ref[...]) + bg_ref[...]
    logsig = jnp.minimum(z, 0.0) - jnp.log1p(jnp.exp(-jnp.abs(z)))
    gla_ref[:, 0:128] = p[:, C_LQ:C_LQ + 128] * (GLA_DK ** -0.5)
    gla_ref[:, 128:256] = p[:, C_LK:C_LK + 128]
    gla_ref[:, 256:512] = logsig * (1.0 / GLA_GATE_NORM)
    gla_ref[:, 512:768] = p[:, C_LV:C_LV + 256]
    gla_ref[:, 768:1024] = p[:, C_LG:C_LG + 256]

    cache_ref[:, 0:128] = ckv
    cache_ref[:, 128:256] = kn
    cache_ref[:, 256:384] = gv
    cache_ref[:, 384:512] = kr4


def _pre_attn(x, y, pmod, mod, tab, modidx, tabidx, lw, n_ctx_tiles):
    N, D = x.shape
    tm = TOKEN_TILE
    nt = N // tm
    has_y = y is not None
    row = lambda i, mi, ti: (i, 0)
    const2 = lambda i, mi, ti: (0, 0)
    in_specs = [pl.BlockSpec((tm, D), row)]
    args = [x]
    if has_y:
        yk, gates = y
        in_specs += [pl.BlockSpec((tm * SUBLANES, LANES), functools.partial(lambda k, i, mi, ti: (k * nt + i, 0), k))
                     for k in range(TOP_K)]
        in_specs += [pl.BlockSpec((tm, LANES), row), pl.BlockSpec((1, 6, D), lambda i, mi, ti: (mi[i], 0, 0))]
        args += [yk] * TOP_K + [gates, pmod]
    in_specs += [
        pl.BlockSpec((1, 6, D), lambda i, mi, ti: (mi[i], 0, 0)),
        pl.BlockSpec((1, tm, 512), lambda i, mi, ti: (ti[i], 0, 0)),
        pl.BlockSpec((1, D), const2),
        pl.BlockSpec((D, IN_COLS), const2),
        pl.BlockSpec((1, Q_LORA), const2),
        pl.BlockSpec((Q_LORA, 1024), const2),
        pl.BlockSpec((1, KV_LORA), const2),
        pl.BlockSpec((1, 512), const2),
        pl.BlockSpec((1, 128), const2),
        pl.BlockSpec((512, 512), const2),
        pl.BlockSpec((128, 256), const2),
        pl.BlockSpec((1, 256), const2),
    ]
    args += [mod, tab, lw['g_attn'], lw['w_in_r'], lw['g_q_a'], lw['w_q2'], lw['g_kv_a'], lw['g_qh'], lw['g_kh'],
             lw['ones512'], lw['w_gate'], lw['b_gate']]
    out_shape, out_specs = [], []
    if has_y:
        out_shape.append(jax.ShapeDtypeStruct((N, D), F32))
        out_specs.append(pl.BlockSpec((tm, D), row))
    for width, dt in ((1024, BF16), (256, BF16), (512, BF16), (256, BF16), (256, BF16), (1024, F32)):
        out_shape.append(jax.ShapeDtypeStruct((N, width), dt))
        out_specs.append(pl.BlockSpec((tm, width), row))
    out_shape.append(jax.ShapeDtypeStruct(((n_ctx_tiles + 1) * tm, 512), F32))
    out_specs.append(pl.BlockSpec((tm, 512), lambda i, mi, ti: (jnp.minimum(i, n_ctx_tiles), 0)))
    outs = pl.pallas_call(
        functools.partial(_pre_attn_kernel, has_y=has_y),
        grid_spec=pltpu.PrefetchScalarGridSpec(num_scalar_prefetch=2, grid=(nt,), in_specs=in_specs,
                                               out_specs=out_specs),
        out_shape=out_shape,
        compiler_params=pltpu.CompilerParams(dimension_semantics=("arbitrary",), vmem_limit_bytes=VMEM_LIMIT),
        name="pre_attn",
    )(modidx, tabidx, *args)
    if not has_y:
        outs = [x] + list(outs)
    return outs


def _softmax_step(qs, k, m_ref, c):
    s = _dot_nt(qs, k)
    m_old = m_ref[c]
    m_new = jnp.maximum(m_old, jnp.broadcast_to(jnp.max(s, axis=-1, keepdims=True), m_old.shape))
    alpha = jnp.exp2(m_old - m_new)
    p = jnp.exp2(s - jnp.concatenate([m_new] * (k.shape[0] // LANES), axis=1))
    m_ref[c] = m_new
    return alpha, p


def _mla_attn_kernel(*refs, has_ctx, n_chunks, tk):
    if has_ctx:
        q_ref, kc_ref, k_ref, wuv_ref, o_ref, m_ref, l_ref, acc_ref = refs
    else:
        q_ref, k_ref, wuv_ref, o_ref, m_ref, l_ref, acc_ref = refs
        kc_ref = None
    tq = q_ref.shape[0]
    m_ref[...] = jnp.full(m_ref.shape, NEG_BIG, F32)
    l_ref[...] = jnp.zeros(l_ref.shape, F32)
    acc_ref[...] = jnp.zeros(acc_ref.shape, F32)
    q = q_ref[...]
    qs = [jnp.concatenate([q[:, h * 256:(h + 1) * 256] for h in (2 * c, 2 * c + 1)], axis=0) for c in range(2)]

    def process(kblk):
        for c in range(2):
            alpha, p = _softmax_step(qs[c], kblk, m_ref, c)
            l_ref[c] = alpha * l_ref[c] + jnp.broadcast_to(jnp.sum(p, axis=-1, keepdims=True), alpha.shape)
            acc_ref[c] = alpha * acc_ref[c] + _dot(p.astype(BF16), kblk[:, 0:KV_LORA])

    if has_ctx:
        process(kc_ref[0])

    def body(j, carry):
        process(k_ref[pl.ds(pl.multiple_of(j * tk, tk), tk), :])
        return carry

    lax.fori_loop(0, n_chunks, body, 0)
    out = jnp.zeros(o_ref.shape, F32)
    for h in range(MLA_HEADS):
        rows = slice((h % 2) * tq, (h % 2 + 1) * tq)
        o_lat = acc_ref[h // 2, rows, :] / l_ref[h // 2, rows, :]
        out = out + _dot(o_lat.astype(BF16), wuv_ref[h])
    o_ref[...] = out.astype(o_ref.dtype)


def _gqa_attn_kernel(*refs, has_ctx, n_chunks, tk):
    if has_ctx:
        q_ref, kc_ref, vc_ref, k_ref, v_ref, o_ref, m_ref, acc_ref = refs
    else:
        q_ref, k_ref, v_ref, o_ref, m_ref, acc_ref = refs
        kc_ref = vc_ref = None
    tq = q_ref.shape[0]
    m_ref[...] = jnp.full(m_ref.shape, NEG_BIG, F32)
    acc_ref[...] = jnp.zeros(acc_ref.shape, F32)
    q = q_ref[...]
    lane = lax.broadcasted_iota(I32, (tq, LANES), 1)
    low = lane < HEAD_DIM
    qs = []
    for g in range(GQA_KV_HEADS):
        parts = []
        for h in range(g * GQA_GROUP, (g + 1) * GQA_GROUP):
            pair = q[:, (h // 2) * LANES:(h // 2 + 1) * LANES]
            keep = low if h % 2 == 0 else jnp.logical_not(low)
            parts.append(jnp.where(keep, pair, jnp.zeros_like(pair)))
        qs.append(jnp.concatenate(parts, axis=0))

    def process(kblk, vblk):
        for g in range(GQA_KV_HEADS):
            alpha, p = _softmax_step(qs[g], kblk[:, g * LANES:(g + 1) * LANES], m_ref, g)
            acc_ref[g] = alpha * acc_ref[g] + _dot(p.astype(BF16), vblk[:, g * LANES:(g + 1) * LANES])

    if has_ctx:
        process(kc_ref[0], vc_ref[0])

    def body(j, carry):
        sl = pl.ds(pl.multiple_of(j * tk, tk), tk)
        process(k_ref[sl, :], v_ref[sl, :])
        return carry

    lax.fori_loop(0, n_chunks, body, 0)
    for j in range(GQA_HEADS // 2):
        outs = []
        for h in (2 * j, 2 * j + 1):
            a = acc_ref[h // GQA_GROUP, (h % GQA_GROUP) * tq:(h % GQA_GROUP + 1) * tq, :]
            outs.append(a / jnp.where(low, pltpu.roll(a, HEAD_DIM, 1), a))
        o_ref[:, j * LANES:(j + 1) * LANES] = jnp.where(low, outs[0], pltpu.roll(outs[1], HEAD_DIM, 1)).astype(
            o_ref.dtype)


def _attention(kind, q, k, v, ctx, wuv, *, n_seq, seq_len, tile_offset):
    has_ctx = ctx is not None
    tq = min(ATTN_TQ, seq_len)
    tk = min(ATTN_TK, seq_len)
    nq = seq_len // tq
    qw = q.shape[1]
    ow = 256 if kind == 'mla' else 512
    heads = MLA_HEADS if kind == 'mla' else GQA_HEADS
    qmap = lambda b, i: (tile_offset * nq + b * nq + i, 0)
    kmap = lambda b, i: (tile_offset + b, 0)
    cmap = lambda b, i: (b, 0, 0)
    in_specs = [pl.BlockSpec((tq, qw), qmap)]
    args = [q]
    if kind == 'mla':
        if has_ctx:
            in_specs.append(pl.BlockSpec((1,) + ctx[0].shape[1:], cmap))
            args.append(ctx[0])
        in_specs += [pl.BlockSpec((seq_len, 256), kmap), pl.BlockSpec((MLA_HEADS, 128, 256), lambda b, i: (0, 0, 0))]
        args += [k, wuv]
        body = _mla_attn_kernel
    else:
        if has_ctx:
            in_specs += [pl.BlockSpec((1,) + ctx[0].shape[1:], cmap), pl.BlockSpec((1,) + ctx[1].shape[1:], cmap)]
            args += [ctx[0], ctx[1]]
        in_specs += [pl.BlockSpec((seq_len, 256), kmap), pl.BlockSpec((seq_len, 256), kmap)]
        args += [k, v]
        body = _gqa_attn_kernel
    return pl.pallas_call(
        functools.partial(body, has_ctx=has_ctx, n_chunks=seq_len // tk, tk=tk),
        grid=(n_seq, nq),
        in_specs=in_specs,
        out_specs=pl.BlockSpec((tq, ow), lambda b, i: (b * nq + i, 0)),
        out_shape=jax.ShapeDtypeStruct((n_seq * seq_len, ow), BF16),
        scratch_shapes=[pltpu.VMEM((2, heads // 2 * tq, LANES), F32)] * (3 if kind == 'mla' else 2),
        compiler_params=pltpu.CompilerParams(dimension_semantics=("arbitrary", "arbitrary"),
                                             vmem_limit_bytes=VMEM_LIMIT),
        name=kind + ("_attn_latent" if has_ctx else "_attn_context"),
    )(*args)


def _gla_kernel(blk_ref, seq_ref, first_ref, *refs, reverse):
    del blk_ref, seq_ref
    if reverse:
        gin_ref, s0_ref, he_ref, of_ref, gout_ref, ones_ref, o_ref, sout_ref, st_ref = refs
    else:
        gin_ref, s0_ref, he_ref, o_ref, sout_ref, st_ref = refs
    i = pl.program_id(0)

    @pl.when(first_ref[i] == 1)
    def _():
        st_ref[...] = s0_ref[0]

    R, S = TOKEN_TILE, GLA_STEP
    ns = R // S
    q = gin_ref[:, 0:128]
    k = gin_ref[:, 128:256]
    g = gin_ref[:, 384:512] if reverse else gin_ref[:, 256:384]
    v = gin_ref[:, 512:768]
    pos = lax.broadcasted_iota(I32, (R, 128), 0) % S
    b = g
    for s in (1, 2, 4, 8):
        if reverse:
            b = b + jnp.where(pos < S - s, pltpu.roll(b, R - s, 0), 0.0)
        else:
            b = b + jnp.where(pos >= s, pltpu.roll(b, s, 0), 0.0)

    def step_row(a, j):
        w = a.shape[-1]
        a3 = a.reshape(ns, S, w)
        return jnp.broadcast_to(a3[:, j:j + 1, :], (ns, S, w)).reshape(R, w)

    b_edge = step_row(b, 0 if reverse else S - 1)
    qt = (q * jnp.exp(b)).astype(BF16)
    kt = (k * jnp.exp(b_edge - b)).astype(BF16)
    d_edge = jnp.exp(b_edge)
    vb = v.astype(BF16)
    he = he_ref[...]

    o_intra = jnp.zeros((R, 256), F32)
    for j in range(S):
        cond = (pos <= j) if reverse else (pos >= j)
        bj, kj, vj = step_row(b, j), step_row(k, j), step_row(v, j)
        t = jnp.where(cond, q * kj * jnp.exp(jnp.where(cond, b - bj, 0.0)), 0.0)
        o_intra = o_intra + _dot(t.astype(BF16), he) * vj

    own_head = (lax.broadcasted_iota(I32, (256, 128), 0) // GLA_DV) == (lax.broadcasted_iota(I32, (256, 128), 1) // GLA_DK)
    st = st_ref[...]
    parts = [None] * ns
    for t in (range(ns - 1, -1, -1) if reverse else range(ns)):
        rows = slice(t * S, (t + 1) * S)
        parts[t] = _dot_nt(qt[rows], st.astype(BF16)) + o_intra[rows]
        ut = _dot_tn(vb[rows], kt[rows])
        st = d_edge[t * S:t * S + 1, :] * st + jnp.where(own_head, ut, 0.0)
    st_ref[...] = st
    sout_ref[0] = st
    o = jnp.concatenate(parts, axis=0)
    if reverse:
        o = o + of_ref[...]
        ms = _split_dot(o * o, ones_ref[...]) * (1.0 / GLA_DV)
        on = o * lax.rsqrt(ms + EPS) * gout_ref[...]
        lg = gin_ref[:, 768:1024]
        o_ref[...] = (on * (lg * jax.nn.sigmoid(lg))).astype(o_ref.dtype)
    else:
        o_ref[...] = o


def _gla(gin, s0, o_f, lw, order, *, reverse):
    blk, seq, first = order
    N = gin.shape[0]
    tm = TOKEN_TILE
    n_seq = s0.shape[0]
    tile = lambda i, b, s, f: (b[i], 0)
    state = lambda i, b, s, f: (s[i], 0, 0)
    const2 = lambda i, b, s, f: (0, 0)
    in_specs = [pl.BlockSpec((tm, 1024), tile), pl.BlockSpec((1, 256, 128), state), pl.BlockSpec((128, 256), const2)]
    args = [gin, s0, lw['head_expand']]
    if reverse:
        in_specs += [pl.BlockSpec((tm, 256), tile), pl.BlockSpec((1, 256), const2), pl.BlockSpec((256, 256), const2)]
        args += [o_f, lw['g_gla_out'], lw['ones256']]
    return pl.pallas_call(
        functools.partial(_gla_kernel, reverse=reverse),
        grid_spec=pltpu.PrefetchScalarGridSpec(
            num_scalar_prefetch=3, grid=(N // tm,), in_specs=in_specs,
            out_specs=[pl.BlockSpec((tm, 256), tile), pl.BlockSpec((1, 256, 128), state)],
            scratch_shapes=[pltpu.VMEM((256, 128), F32)]),
        out_shape=[jax.ShapeDtypeStruct((N, 256), BF16 if reverse else F32),
                   jax.ShapeDtypeStruct((n_seq, 256, 128), F32)],
        compiler_params=pltpu.CompilerParams(dimension_semantics=("arbitrary",), vmem_limit_bytes=VMEM_LIMIT),
        name="gla_bwd" if reverse else "gla_fwd",
    )(blk, seq, first, *args)


def _post_attn_kernel(modidx_ref, x_ref, omc_ref, oml_ref, ogc_ref, ogl_ref, ol_ref, mod_ref, wout_ref, gffn_ref,
                      wrh_ref, wrl_ref, br_ref, tri_ref, x1_ref, h2_ref, ti_ref, tg_ref, cnt_ref, carry_ref, *,
                      n_ctx_tiles):
    del modidx_ref
    mod = mod_ref[0]
    gate1, shift2, scale2 = mod[2:3, :], mod[3:4, :], mod[4:5, :]
    is_ctx = pl.program_id(0) < n_ctx_tiles
    om = jnp.where(is_ctx, omc_ref[...], oml_ref[...])
    og = jnp.where(is_ctx, ogc_ref[...], ogl_ref[...])
    o = _dot(om, wout_ref[0:256, :]) + _dot(og, wout_ref[256:768, :]) + _dot(ol_ref[...], wout_ref[768:1024, :])
    x1 = x_ref[...] + gate1 * o
    x1_ref[...] = x1
    xn = x1 * lax.rsqrt(jnp.mean(x1 * x1, axis=-1, keepdims=True) + EPS) * gffn_ref[...]
    h2 = xn * (1.0 + scale2) + shift2
    _to_token_tiles(h2_ref, h2)
    hi = h2.astype(BF16)
    lo = (h2 - hi.astype(F32)).astype(BF16)
    wrh = wrh_ref[...]
    logits = _dot(hi, wrh) + _dot(hi, wrl_ref[...]) + _dot(lo, wrh) + br_ref[...]

    lane = lax.broadcasted_iota(I32, logits.shape, 1).astype(F32)
    work = logits
    vals, idxs = [], []
    for _ in range(TOP_K):
        m = jnp.max(work, axis=-1, keepdims=True)
        idx = jnp.min(jnp.where(work == m, lane, float(LANES)), axis=-1, keepdims=True)
        vals.append(m)
        idxs.append(idx)
        work = jnp.where(lane == idx, NEG_BIG * 4.0, work)
    es = [jnp.exp(vv - vals[0]) for vv in vals]
    denom = es[0] + es[1] + es[2] + es[3]
    ti = jnp.zeros(logits.shape, F32)
    tg = jnp.zeros(logits.shape, F32)
    for r in range(TOP_K):
        ti = jnp.where(lane == float(r), idxs[r], ti)
        tg = jnp.where(lane == float(r), es[r] / denom, tg)
    tg_ref[...] = tg

    @pl.when(pl.program_id(0) == 0)
    def _():
        carry_ref[...] = jnp.zeros(carry_ref.shape, F32)

    picks = [lane == idxs[r] for r in range(TOP_K)]
    onehot = sum(jnp.where(pk, 1.0, 0.0) for pk in picks)
    carry = carry_ref[...]
    before = carry + _dot(tri_ref[...], onehot.astype(BF16))
    for r in range(TOP_K):
        rank = jnp.sum(jnp.where(picks[r], before, 0.0), axis=-1, keepdims=True)
        ti = jnp.where(lane == float(TOP_K + r), rank, ti)
    ti_ref[...] = ti.astype(I32)
    carry = carry + jnp.sum(onehot, axis=0, keepdims=True)
    carry_ref[...] = carry
    cnt_ref[...] = jnp.broadcast_to(carry, cnt_ref.shape)


def _post_attn(x, o_mla, o_gqa, o_gla, mod, modidx, lw, n_ctx_tiles):
    N, D = x.shape
    tm = TOKEN_TILE
    row = lambda i, mi: (i, 0)
    ctx_row = lambda i, mi: (jnp.minimum(i, n_ctx_tiles - 1), 0)
    lat_row = lambda i, mi: (jnp.maximum(i - n_ctx_tiles, 0), 0)
    const2 = lambda i, mi: (0, 0)
    return pl.pallas_call(
        functools.partial(_post_attn_kernel, n_ctx_tiles=n_ctx_tiles),
        grid_spec=pltpu.PrefetchScalarGridSpec(
            num_scalar_prefetch=1, grid=(N // tm,),
            in_specs=[pl.BlockSpec((tm, D), row), pl.BlockSpec((tm, 256), ctx_row), pl.BlockSpec((tm, 256), lat_row),
                      pl.BlockSpec((tm, 512), ctx_row), pl.BlockSpec((tm, 512), lat_row),
                      pl.BlockSpec((tm, 256), row), pl.BlockSpec((1, 6, D), lambda i, mi: (mi[i], 0, 0)),
                      pl.BlockSpec((D, D), const2), pl.BlockSpec((1, D), const2),
                      pl.BlockSpec((D, LANES), const2), pl.BlockSpec((D, LANES), const2),
                      pl.BlockSpec((1, LANES), const2), pl.BlockSpec((tm, tm), const2)],
            out_specs=[pl.BlockSpec((tm, D), row), pl.BlockSpec((tm * SUBLANES, LANES), row),
                       pl.BlockSpec((tm, LANES), row), pl.BlockSpec((tm, LANES), row),
                       pl.BlockSpec((SUBLANES, LANES), const2)],
            scratch_shapes=[pltpu.VMEM((1, LANES), F32)]),
        out_shape=[jax.ShapeDtypeStruct((N, D), F32), jax.ShapeDtypeStruct((N * SUBLANES, LANES), F32),
                   jax.ShapeDtypeStruct((N, LANES), I32), jax.ShapeDtypeStruct((N, LANES), F32),
                   jax.ShapeDtypeStruct((SUBLANES, LANES), F32)],
        compiler_params=pltpu.CompilerParams(dimension_semantics=("arbitrary",), vmem_limit_bytes=VMEM_LIMIT),
        name="post_attn",
    )(modidx, x, o_mla[0], o_mla[1], o_gqa[0], o_gqa[1], o_gla, mod, lw['w_out'], lw['g_ffn'], lw['w_router_hi'],
      lw['w_router_lo'], lw['b_router'], lw['tri'])


def _moe_kernel(be_ref, first_ref, valid_ref, src0_ref, src_ref, dst_ref, dstp_ref, h_hbm, wgu_ref, bgu_ref, wd_ref, bd_ref,
                y_hbm, wgu_bf, wd_bf, xbuf0, xbuf1, ybuf0, ybuf1, xsem0, xsem1, ysem0, ysem1):
    del be_ref
    i = pl.program_id(0)
    M = MOE_ROWS

    def gather(idx_ref, r, xb, sem):
        tok = idx_ref[0, 0, r]
        return pltpu.make_async_copy(h_hbm.at[pl.ds(pl.multiple_of(tok * SUBLANES, SUBLANES), SUBLANES), :],
                                     xb.at[pl.ds(r * SUBLANES, SUBLANES), :], sem)

    def scatter(idx_ref, r, yb, sem):
        row = idx_ref[0, 0, r]
        return pltpu.make_async_copy(yb.at[pl.ds(r * SUBLANES, SUBLANES), :],
                                     y_hbm.at[pl.ds(pl.multiple_of(row * SUBLANES, SUBLANES), SUBLANES), :], sem)

    def all_rows_in(xb, sem):
        return pltpu.make_async_copy(h_hbm.at[pl.ds(0, M * SUBLANES), :], xb, sem)

    def all_rows_out(yb, sem):
        return pltpu.make_async_copy(yb, y_hbm.at[pl.ds(0, M * SUBLANES), :], sem)

    @pl.when(i == 0)
    def _():
        ybuf0[...] = jnp.zeros(ybuf0.shape, F32)
        ybuf1[...] = jnp.zeros(ybuf1.shape, F32)

        def prime(r, carry):
            gather(src0_ref, r, xbuf0, xsem0).start()
            scatter(dstp_ref, r, ybuf0, ysem0).start()
            return carry
        lax.fori_loop(0, M, prime, 0)

    @pl.when(first_ref[i] == 1)
    def _():
        wgu_bf[...] = wgu_ref[...].astype(BF16)
        wd_bf[...] = wd_ref[...].astype(BF16)

    def step(xa, xsa, xo, xso, ya, ysa, yo, yso):
        all_rows_in(xa, xsa).wait()

        @pl.when(valid_ref[i] == 1)
        def _():
            for r in range(M):
                scatter(dst_ref, r, yo, yso).start(priority=1)
                gather(src_ref, r, xo, xso).start()
            gu = _dot(_from_token_tiles(xa, M).astype(BF16), wgu_bf[...]) + bgu_ref[...]
            gate = jnp.minimum(gu[:, :D_EXPERT], SWIGLU_LIMIT)
            up = jnp.clip(gu[:, D_EXPERT:], -SWIGLU_LIMIT, SWIGLU_LIMIT)
            act = gate * jax.nn.sigmoid(SWIGLU_ALPHA * gate) * (up + 1.0)
            y = _dot(act.astype(BF16), wd_bf[...]) + bd_ref[...]
            all_rows_out(ya, ysa).wait()
            _to_token_tiles(ya, y)

        @pl.when(valid_ref[i] == 0)
        def _():
            def move(r, carry):
                scatter(dst_ref, r, yo, yso).start(priority=1)
                gather(src_ref, r, xo, xso).start()
                return carry
            lax.fori_loop(0, M, move, 0)
            all_rows_out(ya, ysa).wait()

    @pl.when(i % 2 == 0)
    def _():
        step(xbuf0, xsem0, xbuf1, xsem1, ybuf0, ysem0, ybuf1, ysem1)

    @pl.when(i % 2 == 1)
    def _():
        step(xbuf1, xsem1, xbuf0, xsem0, ybuf1, ysem1, ybuf0, ysem0)

    @pl.when(i == pl.num_programs(0) - 1)
    def _():
        all_rows_out(ybuf0, ysem0).wait()
        all_rows_in(xbuf0, xsem0).wait()


def _moe_experts(h2t, sched, w_gate_up, b_gate_up, w_down, b_down, layer, n_out_rows):
    block_e, first, valid, src, dst, dst_prime = sched
    n_blocks = block_e.shape[0]
    D = D_MODEL
    M = MOE_ROWS
    L, E = w_gate_up.shape[:2]
    wmap = lambda i, be, fi, va: (layer, be[i], 0, 0)
    smem = lambda imap: pl.BlockSpec((1, 1, M), imap, memory_space=pltpu.SMEM)
    return pl.pallas_call(
        _moe_kernel,
        grid_spec=pltpu.PrefetchScalarGridSpec(
            num_scalar_prefetch=3, grid=(n_blocks,),
            in_specs=[smem(lambda i, be, fi, va: (0, 0, 0)), smem(lambda i, be, fi, va: (i + 1, 0, 0)),
                      smem(lambda i, be, fi, va: (i, 0, 0)), smem(lambda i, be, fi, va: (0, 0, 0)),
                      pl.BlockSpec(memory_space=pl.ANY),
                      pl.BlockSpec((None, None, D, 2 * D_EXPERT), wmap),
                      pl.BlockSpec((None, None, 1, 2 * D_EXPERT), wmap),
                      pl.BlockSpec((None, None, D_EXPERT, D), wmap),
                      pl.BlockSpec((None, None, 1, D), wmap)],
            out_specs=pl.BlockSpec(memory_space=pl.ANY),
            scratch_shapes=[pltpu.VMEM((D, 2 * D_EXPERT), BF16), pltpu.VMEM((D_EXPERT, D), BF16),
                            ] + [pltpu.VMEM((M * SUBLANES, LANES), F32)] * 4 + [pltpu.SemaphoreType.DMA(())] * 4),
        out_shape=jax.ShapeDtypeStruct((n_out_rows * SUBLANES, LANES), F32),
        compiler_params=pltpu.CompilerParams(dimension_semantics=("arbitrary",), vmem_limit_bytes=VMEM_LIMIT),
        name="moe_experts",
    )(block_e, first, valid, src, src, dst, dst_prime, h2t, w_gate_up, b_gate_up.reshape(L, E, 1, 2 * D_EXPERT), w_down,
      b_down.reshape(L, E, 1, D))


def _route(top_idx, rank, counts, n_blocks):
    N = top_idx.shape[0]
    padded = (counts + MOE_ROWS - 1) // MOE_ROWS * MOE_ROWS
    pad_end = jnp.cumsum(padded)
    pad_start = pad_end - padded
    dest = pad_start[top_idx] + rank
    M, K = MOE_ROWS, TOP_K
    blk_start = jnp.arange(n_blocks, dtype=I32) * M
    valid = blk_start < pad_end[-1]
    block_e = jnp.minimum((blk_start[:, None] >= pad_end[None, :]).astype(I32).sum(axis=1), N_EXPERTS - 1)
    order = jnp.argsort(dest.reshape(-1)).astype(I32)
    first_pos = jnp.cumsum(counts) - counts
    slot_id = jnp.arange(n_blocks * M, dtype=I32)
    e_slot = jnp.repeat(block_e, M)
    offset = slot_id - pad_start[e_slot]
    real = jnp.repeat(valid, M) & (offset < counts[e_slot])
    slot_asg = jnp.where(real, order[jnp.clip(first_pos[e_slot] + offset, 0, N * K - 1)], -1)
    tok = jnp.where(real, slot_asg // K, 0)
    spare = K * N + ((slot_id // M) % 2) * M + slot_id % M
    out_row = jnp.where(real, (slot_asg % K) * N + tok, spare)
    src = jnp.concatenate([tok.reshape(n_blocks, 1, M), jnp.zeros((1, 1, M), I32)], axis=0)
    lead = (K * N + M + jnp.arange(M, dtype=I32)).reshape(1, 1, M)
    dst = jnp.concatenate([lead, out_row.reshape(n_blocks, 1, M)], axis=0)
    dst_prime = (K * N + jnp.arange(M, dtype=I32)).reshape(1, 1, M)
    n_valid = pad_end[-1] // M
    block_e = block_e[jnp.minimum(jnp.arange(n_blocks, dtype=I32), n_valid - 1)]
    prev_e = jnp.concatenate([jnp.full((1,), -1, I32), block_e[:-1]])
    first = (valid & (block_e != prev_e)).astype(I32)
    return (block_e, first, valid.astype(I32), src, dst, dst_prime)


def _final_kernel(modidx_ref, x_ref, *refs):
    del modidx_ref
    y_refs, (tg_ref, pmod_ref, g_ref, o_ref) = refs[:TOP_K], refs[TOP_K:]
    x = x_ref[...] + pmod_ref[0, 5:6, :] * _combine_experts(y_refs, tg_ref)
    o_ref[...] = x * lax.rsqrt(jnp.mean(x * x, axis=-1, keepdims=True) + EPS) * g_ref[...]


def _final_norm(x, y, pmod, modidx, g_final):
    N, D = x.shape
    tm = TOKEN_TILE
    row = lambda i, mi: (i, 0)
    return pl.pallas_call(
        _final_kernel,
        grid_spec=pltpu.PrefetchScalarGridSpec(
            num_scalar_prefetch=1, grid=(N // tm,),
            in_specs=[pl.BlockSpec((tm, D), row)]
            + [pl.BlockSpec((tm * SUBLANES, LANES), functools.partial(lambda k, i, mi: (k * (N // tm) + i, 0), k))
               for k in range(TOP_K)]
            + [pl.BlockSpec((tm, LANES), row),
               pl.BlockSpec((1, 6, D), lambda i, mi: (mi[i], 0, 0)), pl.BlockSpec((1, D), lambda i, mi: (0, 0))],
            out_specs=pl.BlockSpec((tm, D), row)),
        out_shape=jax.ShapeDtypeStruct((N, D), F32),
        compiler_params=pltpu.CompilerParams(dimension_semantics=("arbitrary",), vmem_limit_bytes=VMEM_LIMIT),
        name="final_norm",
    )(modidx, x, *([y[0]] * TOP_K), y[1], pmod, g_final)


def _rope_tables(seq_len):
    pos = np.arange(seq_len)
    rowp = (pos // GRID_W).astype(np.float32)
    colp = (pos % GRID_W).astype(np.float32)

    def tables(rdim, copies):
        quarter = rdim // 4
        inv_freq = jnp.asarray(ROPE_THETA, F32) ** (-jnp.arange(quarter, dtype=F32) / quarter)
        ar = jnp.asarray(rowp)[:, None] * inv_freq[None, :]
        ac = jnp.asarray(colp)[:, None] * inv_freq[None, :]
        cos = jnp.concatenate([jnp.cos(ar), jnp.cos(ar), jnp.cos(ac), jnp.cos(ac)], axis=1)
        sin = jnp.concatenate([-jnp.sin(ar), jnp.sin(ar), -jnp.sin(ac), jnp.sin(ac)], axis=1)
        return jnp.tile(cos, (1, copies)), jnp.tile(sin, (1, copies))

    c64, s64 = tables(HEAD_DIM, 2)
    c32, s32 = tables(MLA_ROPE, 4)
    lat = jnp.concatenate([c64, s64, c32, s32], axis=1).reshape(seq_len // TOKEN_TILE, TOKEN_TILE, 512)
    ident = jnp.concatenate([jnp.ones((TOKEN_TILE, 128), F32), jnp.zeros((TOKEN_TILE, 128), F32)] * 2, axis=1)
    return jnp.concatenate([ident[None], lat], axis=0)


def _block_ones(n, blk):
    r = np.arange(n) // blk
    return jnp.asarray((r[:, None] == r[None, :]).astype(np.float32), BF16)


def kernel(x_prompt, x_sample, cache_mla_ckv, cache_mla_krope, cache_gqa_k, cache_gqa_v, state_gla, c, c_ctx, w_mod, b_mod, g_attn_norm, g_ffn_norm, w_in, g_q_a, w_uq, g_kv_a, w_ukv, g_q_head, g_k_head, w_gk_fwd, b_gk_fwd, w_gk_bwd, b_gk_bwd, g_gla_out, w_out, w_router, b_router, w_gate_up, b_gate_up, w_down, b_down, g_final):
    B, T, D = x_prompt.shape
    BD, TD, _ = x_sample.shape
    L = w_in.shape[0]
    P = cache_mla_ckv.shape[2]
    tm = TOKEN_TILE
    n_ctx = B * T
    N = n_ctx + BD * TD
    assert T == tm and TD % ATTN_TQ == 0 and n_ctx % TD == 0 and BD + 1 <= 16
    n_ctx_tiles = n_ctx // tm
    nt = N // tm
    tiles_per_lat = TD // tm

    tile_ids = np.arange(nt)
    lat_tile = np.maximum(tile_ids - n_ctx_tiles, 0)
    is_lat = tile_ids >= n_ctx_tiles
    modidx = jnp.asarray(np.where(is_lat, 1 + lat_tile // tiles_per_lat, 0), I32)
    tabidx = jnp.asarray(np.where(is_lat, 1 + lat_tile % tiles_per_lat, 0), I32)
    seq_of_tile = np.where(is_lat, B + lat_tile // tiles_per_lat, tile_ids)
    first_fwd = np.where(is_lat, lat_tile % tiles_per_lat == 0, True)
    last_fwd = np.where(is_lat, lat_tile % tiles_per_lat == tiles_per_lat - 1, True)
    order_fwd = (jnp.asarray(tile_ids, I32), jnp.asarray(seq_of_tile, I32), jnp.asarray(first_fwd, I32))
    rev = tile_ids[::-1]
    order_bwd = (jnp.asarray(rev, I32), jnp.asarray(seq_of_tile[rev], I32), jnp.asarray(last_fwd[rev], I32))
    n_seq = B + BD

    cvecs = jnp.zeros((16, D), F32).at[0].set(c_ctx).at[1:1 + BD].set(c)
    mods = _modulation(cvecs, w_mod, b_mod).reshape(L, 16, 6, D)

    o = np.cumsum([0, Q_LORA, KV_LORA, MLA_ROPE, 512, 128, 128, 128, 128, 256, 16, 16, 256])
    seg = lambda j: w_in[:, :, o[j]:o[j + 1]]
    w_in_r = jnp.concatenate(
        [seg(0), seg(1), seg(3), seg(4), seg(5), seg(6), seg(7), seg(8), seg(11), seg(2), seg(2), seg(2), seg(2),
         seg(9), seg(10), jnp.zeros((L, D, 128 - 2 * GLA_GATE_RANK), F32)], axis=-1).astype(BF16)
    uq = w_uq.reshape(L, Q_LORA, MLA_HEADS, MLA_QK)
    ukv = w_ukv.reshape(L, KV_LORA, MLA_HEADS, MLA_NOPE + MLA_V)
    w_comb = _fold_q_weights(uq[..., :MLA_NOPE].transpose(0, 2, 1, 3), ukv[..., :MLA_NOPE].transpose(0, 2, 1, 3))
    q_pad = jnp.zeros((L, Q_LORA, 256 - KV_LORA - MLA_ROPE), F32)
    w_q2 = (jnp.concatenate([part for h in range(MLA_HEADS)
                             for part in (w_comb[:, h], uq[:, :, h, MLA_NOPE:], q_pad)], axis=-1)
            * (MLA_QK ** -0.5 * LOG2E)).astype(BF16)
    w_uv = ukv[..., MLA_NOPE:].transpose(0, 2, 1, 3)
    eye_h = jnp.eye(MLA_HEADS, dtype=F32)
    wuv_pad = jnp.einsum('lhkv,hg->lhkgv', w_uv, eye_h).reshape(L, MLA_HEADS, KV_LORA, MLA_HEADS * MLA_V).astype(BF16)
    w_gate = jnp.zeros((L, 128, 256), F32).at[:, 0:16, 0:128].set(w_gk_fwd).at[:, 16:32, 128:256].set(w_gk_bwd)
    w_gate = w_gate.astype(BF16)
    b_gate = jnp.concatenate([b_gk_fwd, b_gk_bwd], axis=-1)
    w_out_bf = w_out.astype(BF16)
    wr_pad = jnp.pad(w_router, ((0, 0), (0, 0), (0, LANES - N_EXPERTS)))
    wr_hi = wr_pad.astype(BF16)
    wr_lo = (wr_pad - wr_hi.astype(F32)).astype(BF16)
    br_pad = jnp.pad(b_router, ((0, 0), (0, LANES - N_EXPERTS)), constant_values=NEG_BIG)
    ones512 = _block_ones(512, HEAD_DIM)
    tri = jnp.asarray((np.arange(tm)[None, :] < np.arange(tm)[:, None]).astype(np.float32), BF16)
    head_expand = jnp.asarray((np.arange(128)[:, None] // GLA_DK == np.arange(256)[None, :] // GLA_DV)
                              .astype(np.float32), BF16)
    tab = _rope_tables(TD)

    x = jnp.concatenate([x_prompt.reshape(n_ctx, D), x_sample.reshape(BD * TD, D)], axis=0)
    y = None
    n_blocks = (N * TOP_K) // MOE_ROWS + N_EXPERTS
    assert (N * TOP_K) % MOE_ROWS == 0 and n_blocks % 2 == 0
    caches, states = [], []
    for l in range(L):
        lw = dict(g_attn=g_attn_norm[l][None], w_in_r=w_in_r[l], g_q_a=g_q_a[l][None], w_q2=w_q2[l],
                  g_kv_a=g_kv_a[l][None], g_qh=jnp.tile(g_q_head[l], GQA_HEADS)[None],
                  g_kh=jnp.tile(g_k_head[l], GQA_KV_HEADS)[None], ones512=ones512, w_gate=w_gate[l],
                  b_gate=b_gate[l][None], head_expand=head_expand, g_gla_out=jnp.tile(g_gla_out[l], GLA_HEADS)[None],
                  ones256=ones512[:256, :256], w_out=w_out_bf[l], g_ffn=g_ffn_norm[l][None], tri=tri,
                  w_router_hi=wr_hi[l], w_router_lo=wr_lo[l], b_router=br_pad[l][None])
        pmod = mods[l - 1] if l > 0 else None
        x, qm, kvc, qg, kdup, vdup, gin, cache = _pre_attn(x, y, pmod, mods[l], tab, modidx, tabidx, lw, n_ctx_tiles)
        caches.append(cache[:n_ctx])

        mla_ctx = jnp.concatenate([cache_mla_ckv[:, l]] + [cache_mla_krope[:, l]] * 4, axis=-1).astype(BF16)
        ck = cache_gqa_k[:, l]
        cv = cache_gqa_v[:, l]
        k_ctx = jnp.concatenate([ck[:, :, 0], ck[:, :, 0], ck[:, :, 1], ck[:, :, 1]], axis=-1).astype(BF16)
        ones_v = jnp.ones(cv.shape[:2] + (HEAD_DIM,), F32)
        v_ctx = jnp.concatenate([cv[:, :, 0], ones_v, cv[:, :, 1], ones_v], axis=-1).astype(BF16)
        o_mla = (_attention('mla', qm, kvc, None, None, wuv_pad[l], n_seq=B, seq_len=T, tile_offset=0),
                 _attention('mla', qm, kvc, None, (mla_ctx,), wuv_pad[l], n_seq=BD, seq_len=TD,
                            tile_offset=n_ctx // TD))
        o_gqa = (_attention('gqa', qg, kdup, vdup, None, None, n_seq=B, seq_len=T, tile_offset=0),
                 _attention('gqa', qg, kdup, vdup, (k_ctx, v_ctx), None, n_seq=BD, seq_len=TD,
                            tile_offset=n_ctx // TD))

        eye_g = jnp.eye(GLA_HEADS, dtype=F32)
        st_lat = jnp.einsum('bshde,hg->bshegd', state_gla[:, l], eye_g).reshape(BD, 2, 256, 128)
        zeros_ctx = jnp.zeros((B, 256, 128), F32)
        o_f, s_f = _gla(gin, jnp.concatenate([zeros_ctx, st_lat[:, 0]], axis=0), None, lw, order_fwd, reverse=False)
        o_gla, s_b = _gla(gin, jnp.concatenate([zeros_ctx, st_lat[:, 1]], axis=0), o_f, lw, order_bwd, reverse=True)
        states.append((s_f[:B], s_b[:B]))

        x1, h2, ti, tg, cnt = _post_attn(x, o_mla, o_gqa, o_gla, mods[l], modidx, lw, n_ctx_tiles)
        sched = _route(ti[:, :TOP_K], ti[:, TOP_K:2 * TOP_K], cnt[0, :N_EXPERTS].astype(I32), n_blocks)
        y = (_moe_experts(h2, sched, w_gate_up, b_gate_up, w_down, b_down, l, TOP_K * N + 2 * MOE_ROWS), tg)
        x = x1

    out = _final_norm(x, y, mods[L - 1], modidx, g_final[None])
    y_prompt = out[:n_ctx].reshape(B, T, D)
    y_sample = out[n_ctx:].reshape(BD, TD, D)
    cache_all = jnp.stack([cc.reshape(B, T, 512) for cc in caches], axis=1)
    new_ckv = cache_all[..., 0:128]
    new_k = cache_all[..., 128:256].reshape(B, L, T, GQA_KV_HEADS, HEAD_DIM)
    new_v = cache_all[..., 256:384].reshape(B, L, T, GQA_KV_HEADS, HEAD_DIM)
    new_krope = cache_all[..., 384:384 + MLA_ROPE]

    def unpack_state(st):
        s5 = st.reshape(B, GLA_HEADS, GLA_DV, GLA_HEADS, GLA_DK)
        diag = jnp.stack([s5[:, h, :, h, :] for h in range(GLA_HEADS)], axis=1)
        return diag.transpose(0, 1, 3, 2)

    new_state = jnp.stack([jnp.stack([unpack_state(sf), unpack_state(sb)], axis=1) for sf, sb in states], axis=1)
    return (y_prompt, y_sample, new_ckv, new_krope, new_k, new_v, new_state)
```

```python
import functools

import jax
import jax.numpy as jnp
import numpy as np
from jax import lax
from jax.experimental import pallas as pl
from jax.experimental.pallas import tpu as pltpu

F32 = jnp.float32
BF16 = jnp.bfloat16
I32 = jnp.int32

D_MODEL = 1024
GRID_W = 64
ROPE_THETA = 10000.0
EPS = 1e-6
MLA_HEADS = 4
Q_LORA = 256
KV_LORA = 128
MLA_NOPE = 64
MLA_ROPE = 32
MLA_V = 64
MLA_QK = MLA_NOPE + MLA_ROPE
GQA_HEADS = 8
GQA_KV_HEADS = 2
GQA_GROUP = GQA_HEADS // GQA_KV_HEADS
HEAD_DIM = 64
GLA_HEADS = 4
GLA_DK = 32
GLA_DV = 64
GLA_GATE_RANK = 16
GLA_GATE_NORM = 16.0
N_EXPERTS = 32
TOP_K = 4
D_EXPERT = D_MODEL
SWIGLU_LIMIT = 7.0
SWIGLU_ALPHA = 1.702

LANES = 128
SUBLANES = 8
TOKEN_TILE = 256
GLA_STEP = 16
MOE_ROWS = 512
ATTN_TQ = 512
ATTN_TK = 512
VMEM_LIMIT = 56 * 1024 * 1024
NEG_BIG = -1e30
LOG2E = 1.4426950408889634

C_CQ, C_CKV, C_GQ, C_GK, C_GV = 0, 256, 384, 896, 1024
C_LQ, C_LK, C_LV, C_LG, C_KR, C_GATE = 1152, 1280, 1408, 1664, 1920, 2048
IN_COLS = 2176


def _dot(a, b):
    return jnp.dot(a, b, preferred_element_type=F32)


def _dot_nt(a, b):
    return lax.dot_general(a, b, (((1,), (1,)), ((), ())), preferred_element_type=F32)


def _dot_tn(a, b):
    return lax.dot_general(a, b, (((0,), (0,)), ((), ())), preferred_element_type=F32)


def _split_dot(x, ones_bf):
    hi = x.astype(BF16)
    lo = (x - hi.astype(F32)).astype(BF16)
    return _dot(hi, ones_bf) + _dot(lo, ones_bf)


def _rope(x, cos, sin_signed, quarter):
    width = x.shape[-1]
    lane = lax.broadcasted_iota(I32, x.shape, 1)
    first = (lane % (2 * quarter)) < quarter
    partner = jnp.where(first, pltpu.roll(x, width - quarter, 1), pltpu.roll(x, quarter, 1))
    return x * cos + partner * sin_signed


def _mod_kernel(c_ref, w_ref, b_ref, o_ref):
    c = c_ref[...]
    a = c * jax.nn.sigmoid(c)
    o_ref[0] = jnp.dot(a, w_ref[0], precision=lax.Precision.HIGHEST, preferred_element_type=F32) + b_ref[0]


def _modulation(cvecs, w_mod, b_mod):
    L, D, D6 = w_mod.shape
    tn = 1536
    return pl.pallas_call(
        _mod_kernel,
        grid=(L, D6 // tn),
        in_specs=[pl.BlockSpec((16, D), lambda l, j: (0, 0)),
                  pl.BlockSpec((1, D, tn), lambda l, j: (l, 0, j)),
                  pl.BlockSpec((1, 1, tn), lambda l, j: (l, 0, j))],
        out_specs=pl.BlockSpec((1, 16, tn), lambda l, j: (l, 0, j)),
        out_shape=jax.ShapeDtypeStruct((L, 16, D6), F32),
        compiler_params=pltpu.CompilerParams(dimension_semantics=("arbitrary", "arbitrary"),
                                             vmem_limit_bytes=VMEM_LIMIT),
        name="modulation",
    )(cvecs, w_mod, b_mod.reshape(L, 1, D6))


def _fold_kernel(a_ref, b_ref, o_ref):
    o_ref[0, 0] = lax.dot_general(a_ref[0, 0], b_ref[0, 0], (((1,), (1,)), ((), ())),
                                  precision=lax.Precision.HIGHEST, preferred_element_type=F32)


def _fold_q_weights(wq_nope, wk_nope):
    L, H, A, K = wq_nope.shape
    B = wk_nope.shape[2]
    return pl.pallas_call(
        _fold_kernel,
        grid=(L, H),
        in_specs=[pl.BlockSpec((1, 1, A, K), lambda l, h: (l, h, 0, 0)),
                  pl.BlockSpec((1, 1, B, K), lambda l, h: (l, h, 0, 0))],
        out_specs=pl.BlockSpec((1, 1, A, B), lambda l, h: (l, h, 0, 0)),
        out_shape=jax.ShapeDtypeStruct((L, H, A, B), F32),
        name="fold_q_weights",
    )(wq_nope, wk_nope)


def _from_token_tiles(ref, rows):
    return jnp.concatenate([ref[pl.ds(c, rows, stride=SUBLANES), :] for c in range(D_MODEL // LANES)], axis=1)


def _to_token_tiles(ref, x):
    for c in range(D_MODEL // LANES):
        ref[pl.ds(c, x.shape[0], stride=SUBLANES), :] = x[:, c * LANES:(c + 1) * LANES]


def _combine_experts(y_refs, tg_ref):
    tg = tg_ref[...]
    acc = None
    for k in range(TOP_K):
        term = tg[:, k:k + 1] * _from_token_tiles(y_refs[k], tg.shape[0])
        acc = term if acc is None else acc + term
    return acc


def _pre_attn_kernel(modidx_ref, tabidx_ref, *refs, has_y):
    del modidx_ref, tabidx_ref
    if has_y:
        x_ref, y_ref, tg_ref, pmod_ref = refs[0], refs[1:1 + TOP_K], refs[1 + TOP_K], refs[2 + TOP_K]
        refs = refs[3 + TOP_K:]
    else:
        x_ref = refs[0]
        refs = refs[1:]
    (mod_ref, tab_ref, gattn_ref, win_ref, gqa_ref, wq2_ref, gkva_ref, gqh_ref, gkh_ref, ones_ref,
     wg_ref, bg_ref) = refs[:12]
    outs = refs[12:]
    if has_y:
        xres_ref = outs[0]
        outs = outs[1:]
    qm_ref, kvc_ref, qg_ref, kdup_ref, vdup_ref, gla_ref, cache_ref = outs

    x = x_ref[...]
    if has_y:
        x = x + pmod_ref[0, 5:6, :] * _combine_experts(y_ref, tg_ref)
        xres_ref[...] = x
    mod = mod_ref[0]
    shift1, scale1 = mod[0:1, :], mod[1:2, :]
    xn = x * lax.rsqrt(jnp.mean(x * x, axis=-1, keepdims=True) + EPS) * gattn_ref[...]
    h = xn * (1.0 + scale1) + shift1
    p = _dot(h.astype(BF16), win_ref[...])

    tab = tab_ref[0]
    cos64, sin64 = tab[:, 0:128], tab[:, 128:256]
    cos32, sin32 = tab[:, 256:384], tab[:, 384:512]
    ones = ones_ref[...]

    cq = p[:, C_CQ:C_CQ + Q_LORA]
    cqn = cq * lax.rsqrt(jnp.mean(cq * cq, axis=-1, keepdims=True) + EPS) * gqa_ref[...]
    qm = _dot(cqn.astype(BF16), wq2_ref[...])
    q_rope = _rope(jnp.concatenate([qm[:, h * 256 + 128:(h + 1) * 256] for h in range(MLA_HEADS)], axis=1),
                   jnp.concatenate([cos32] * MLA_HEADS, axis=1), jnp.concatenate([sin32] * MLA_HEADS, axis=1),
                   MLA_ROPE // 4)
    for h in range(MLA_HEADS):
        qm_ref[:, h * 256:h * 256 + 128] = qm[:, h * 256:h * 256 + 128].astype(BF16)
        qm_ref[:, h * 256 + 128:(h + 1) * 256] = q_rope[:, h * 128:(h + 1) * 128].astype(BF16)
    ckv_raw = p[:, C_CKV:C_CKV + KV_LORA]
    ckv = ckv_raw * lax.rsqrt(jnp.mean(ckv_raw * ckv_raw, axis=-1, keepdims=True) + EPS) * gkva_ref[...]
    kr4 = p[:, C_KR:C_KR + 128]
    kr4_rot = _rope(kr4, cos32, sin32, MLA_ROPE // 4)
    kvc_ref[:, 0:128] = ckv.astype(BF16)
    kvc_ref[:, 128:256] = kr4_rot.astype(BF16)

    gq = p[:, C_GQ:C_GQ + 512]
    ms_q = _split_dot(gq * gq, ones) * (1.0 / HEAD_DIM)
    qn = gq * lax.rsqrt(ms_q + EPS) * gqh_ref[...]
    cos_q = jnp.concatenate([cos64] * 4, axis=1)
    sin_q = jnp.concatenate([sin64] * 4, axis=1)
    qg = _rope(qn, cos_q, sin_q, HEAD_DIM // 4) * (HEAD_DIM ** -0.5 * LOG2E)
    qg_ref[...] = qg.astype(BF16)
    gk = p[:, C_GK:C_GK + 128]
    ms_k = _split_dot(gk * gk, ones[0:128, 0:128]) * (1.0 / HEAD_DIM)
    kn = gk * lax.rsqrt(ms_k + EPS) * gkh_ref[...]
    kg = _rope(kn, cos64, sin64, HEAD_DIM // 4)
    gv = p[:, C_GV:C_GV + 128]
    lane = lax.broadcasted_iota(I32, kg.shape, 1)
    low = lane < HEAD_DIM
    kg_sw = pltpu.roll(kg, HEAD_DIM, 1)
    gv_sw = pltpu.roll(gv, HEAD_DIM, 1)
    kdup_ref[:, 0:128] = jnp.where(low, kg, kg_sw).astype(BF16)
    kdup_ref[:, 128:256] = jnp.where(low, kg_sw, kg).astype(BF16)
    vdup_ref[:, 0:128] = jnp.where(low, gv, 1.0).astype(BF16)
    vdup_ref[:, 128:256] = jnp.where(low, gv_sw, 1.0).astype(BF16)

    z = _dot(p[:, C_GATE:C_GATE + 128].astype(BF16), wg_ref[...]) + bg_ref[...]
    logsig = jnp.minimum(z, 0.0) - jnp.log1p(jnp.exp(-jnp.abs(z)))
    gla_ref[:, 0:128] = p[:, C_LQ:C_LQ + 128] * (GLA_DK ** -0.5)
    gla_ref[:, 128:256] = p[:, C_LK:C_LK + 128]
    gla_ref[:, 256:512] = logsig * (1.0 / GLA_GATE_NORM)
    gla_ref[:, 512:768] = p[:, C_LV:C_LV + 256]
    gla_ref[:, 768:1024] = p[:, C_LG:C_LG + 256]

    cache_ref[:, 0:128] = ckv
    cache_ref[:, 128:256] = kn
    cache_ref[:, 256:384] = gv
    cache_ref[:, 384:512] = kr4


def _pre_attn(x, y, pmod, mod, tab, modidx, tabidx, lw, n_ctx_tiles):
    N, D = x.shape
    tm = TOKEN_TILE
    nt = N // tm
    has_y = y is not None
    row = lambda i, mi, ti: (i, 0)
    const2 = lambda i, mi, ti: (0, 0)
    in_specs = [pl.BlockSpec((tm, D), row)]
    args = [x]
    if has_y:
        yk, gates = y
        in_specs += [pl.BlockSpec((tm * SUBLANES, LANES), functools.partial(lambda k, i, mi, ti: (k * nt + i, 0), k))
                     for k in range(TOP_K)]
        in_specs += [pl.BlockSpec((tm, LANES), row), pl.BlockSpec((1, 6, D), lambda i, mi, ti: (mi[i], 0, 0))]
        args += [yk] * TOP_K + [gates, pmod]
    in_specs += [
        pl.BlockSpec((1, 6, D), lambda i, mi, ti: (mi[i], 0, 0)),
        pl.BlockSpec((1, tm, 512), lambda i, mi, ti: (ti[i], 0, 0)),
        pl.BlockSpec((1, D), const2),
        pl.BlockSpec((D, IN_COLS), const2),
        pl.BlockSpec((1, Q_LORA), const2),
        pl.BlockSpec((Q_LORA, 1024), const2),
        pl.BlockSpec((1, KV_LORA), const2),
        pl.BlockSpec((1, 512), const2),
        pl.BlockSpec((1, 128), const2),
        pl.BlockSpec((512, 512), const2),
        pl.BlockSpec((128, 256), const2),
        pl.BlockSpec((1, 256), const2),
    ]
    args += [mod, tab, lw['g_attn'], lw['w_in_r'], lw['g_q_a'], lw['w_q2'], lw['g_kv_a'], lw['g_qh'], lw['g_kh'],
             lw['ones512'], lw['w_gate'], lw['b_gate']]
    out_shape, out_specs = [], []
    if has_y:
        out_shape.append(jax.ShapeDtypeStruct((N, D), F32))
        out_specs.append(pl.BlockSpec((tm, D), row))
    for width, dt in ((1024, BF16), (256, BF16), (512, BF16), (256, BF16), (256, BF16), (1024, F32)):
        out_shape.append(jax.ShapeDtypeStruct((N, width), dt))
        out_specs.append(pl.BlockSpec((tm, width), row))
    out_shape.append(jax.ShapeDtypeStruct(((n_ctx_tiles + 1) * tm, 512), F32))
    out_specs.append(pl.BlockSpec((tm, 512), lambda i, mi, ti: (jnp.minimum(i, n_ctx_tiles), 0)))
    outs = pl.pallas_call(
        functools.partial(_pre_attn_kernel, has_y=has_y),
        grid_spec=pltpu.PrefetchScalarGridSpec(num_scalar_prefetch=2, grid=(nt,), in_specs=in_specs,
                                               out_specs=out_specs),
        out_shape=out_shape,
        compiler_params=pltpu.CompilerParams(dimension_semantics=("arbitrary",), vmem_limit_bytes=VMEM_LIMIT),
        name="pre_attn",
    )(modidx, tabidx, *args)
    if not has_y:
        outs = [x] + list(outs)
    return outs


def _softmax_step(qs, k, m_ref, c):
    s = _dot_nt(qs, k)
    m_old = m_ref[c]
    m_new = jnp.maximum(m_old, jnp.broadcast_to(jnp.max(s, axis=-1, keepdims=True), m_old.shape))
    alpha = jnp.exp2(m_old - m_new)
    p = jnp.exp2(s - jnp.concatenate([m_new] * (k.shape[0] // LANES), axis=1))
    m_ref[c] = m_new
    return alpha, p


def _mla_attn_kernel(*refs, has_ctx, n_chunks, tk):
    if has_ctx:
        q_ref, kc_ref, k_ref, wuv_ref, o_ref, m_ref, l_ref, acc_ref = refs
    else:
        q_ref, k_ref, wuv_ref, o_ref, m_ref, l_ref, acc_ref = refs
        kc_ref = None
    tq = q_ref.shape[0]
    m_ref[...] = jnp.full(m_ref.shape, NEG_BIG, F32)
    l_ref[...] = jnp.zeros(l_ref.shape, F32)
    acc_ref[...] = jnp.zeros(acc_ref.shape, F32)
    q = q_ref[...]
    qs = [jnp.concatenate([q[:, h * 256:(h + 1) * 256] for h in (2 * c, 2 * c + 1)], axis=0) for c in range(2)]

    def process(kblk):
        for c in range(2):
            alpha, p = _softmax_step(qs[c], kblk, m_ref, c)
            l_ref[c] = alpha * l_ref[c] + jnp.broadcast_to(jnp.sum(p, axis=-1, keepdims=True), alpha.shape)
            acc_ref[c] = alpha * acc_ref[c] + _dot(p.astype(BF16), kblk[:, 0:KV_LORA])

    if has_ctx:
        process(kc_ref[0])

    def body(j, carry):
        process(k_ref[pl.ds(pl.multiple_of(j * tk, tk), tk), :])
        return carry

    lax.fori_loop(0, n_chunks, body, 0)
    out = jnp.zeros(o_ref.shape, F32)
    for h in range(MLA_HEADS):
        rows = slice((h % 2) * tq, (h % 2 + 1) * tq)
        o_lat = acc_ref[h // 2, rows, :] / l_ref[h // 2, rows, :]
        out = out + _dot(o_lat.astype(BF16), wuv_ref[h])
    o_ref[...] = out.astype(o_ref.dtype)


def _gqa_attn_kernel(*refs, has_ctx, n_chunks, tk):
    if has_ctx:
        q_ref, kc_ref, vc_ref, k_ref, v_ref, o_ref, m_ref, acc_ref = refs
    else:
        q_ref, k_ref, v_ref, o_ref, m_ref, acc_ref = refs
        kc_ref = vc_ref = None
    tq = q_ref.shape[0]
    m_ref[...] = jnp.full(m_ref.shape, NEG_BIG, F32)
    acc_ref[...] = jnp.zeros(acc_ref.shape, F32)
    q = q_ref[...]
    lane = lax.broadcasted_iota(I32, (tq, LANES), 1)
    low = lane < HEAD_DIM
    qs = []
    for g in range(GQA_KV_HEADS):
        parts = []
        for h in range(g * GQA_GROUP, (g + 1) * GQA_GROUP):
            pair = q[:, (h // 2) * LANES:(h // 2 + 1) * LANES]
            keep = low if h % 2 == 0 else jnp.logical_not(low)
            parts.append(jnp.where(keep, pair, jnp.zeros_like(pair)))
        qs.append(jnp.concatenate(parts, axis=0))

    def process(kblk, vblk):
        for g in range(GQA_KV_HEADS):
            alpha, p = _softmax_step(qs[g], kblk[:, g * LANES:(g + 1) * LANES], m_ref, g)
            acc_ref[g] = alpha * acc_ref[g] + _dot(p.astype(BF16), vblk[:, g * LANES:(g + 1) * LANES])

    if has_ctx:
        process(kc_ref[0], vc_ref[0])

    def body(j, carry):
        sl = pl.ds(pl.multiple_of(j * tk, tk), tk)
        process(k_ref[sl, :], v_ref[sl, :])
        return carry

    lax.fori_loop(0, n_chunks, body, 0)
    for j in range(GQA_HEADS // 2):
        outs = []
        for h in (2 * j, 2 * j + 1):
            a = acc_ref[h // GQA_GROUP, (h % GQA_GROUP) * tq:(h % GQA_GROUP + 1) * tq, :]
            outs.append(a / jnp.where(low, pltpu.roll(a, HEAD_DIM, 1), a))
        o_ref[:, j * LANES:(j + 1) * LANES] = jnp.where(low, outs[0], pltpu.roll(outs[1], HEAD_DIM, 1)).astype(
            o_ref.dtype)


def _attention(kind, q, k, v, ctx, wuv, *, n_seq, seq_len, tile_offset):
    has_ctx = ctx is not None
    tq = min(ATTN_TQ, seq_len)
    tk = min(ATTN_TK, seq_len)
    nq = seq_len // tq
    qw = q.shape[1]
    ow = 256 if kind == 'mla' else 512
    heads = MLA_HEADS if kind == 'mla' else GQA_HEADS
    qmap = lambda b, i: (tile_offset * nq + b * nq + i, 0)
    kmap = lambda b, i: (tile_offset + b, 0)
    cmap = lambda b, i: (b, 0, 0)
    in_specs = [pl.BlockSpec((tq, qw), qmap)]
    args = [q]
    if kind == 'mla':
        if has_ctx:
            in_specs.append(pl.BlockSpec((1,) + ctx[0].shape[1:], cmap))
            args.append(ctx[0])
        in_specs += [pl.BlockSpec((seq_len, 256), kmap), pl.BlockSpec((MLA_HEADS, 128, 256), lambda b, i: (0, 0, 0))]
        args += [k, wuv]
        body = _mla_attn_kernel
    else:
        if has_ctx:
            in_specs += [pl.BlockSpec((1,) + ctx[0].shape[1:], cmap), pl.BlockSpec((1,) + ctx[1].shape[1:], cmap)]
            args += [ctx[0], ctx[1]]
        in_specs += [pl.BlockSpec((seq_len, 256), kmap), pl.BlockSpec((seq_len, 256), kmap)]
        args += [k, v]
        body = _gqa_attn_kernel
    return pl.pallas_call(
        functools.partial(body, has_ctx=has_ctx, n_chunks=seq_len // tk, tk=tk),
        grid=(n_seq, nq),
        in_specs=in_specs,
        out_specs=pl.BlockSpec((tq, ow), lambda b, i: (b * nq + i, 0)),
        out_shape=jax.ShapeDtypeStruct((n_seq * seq_len, ow), BF16),
        scratch_shapes=[pltpu.VMEM((2, heads // 2 * tq, LANES), F32)] * (3 if kind == 'mla' else 2),
        compiler_params=pltpu.CompilerParams(dimension_semantics=("arbitrary", "arbitrary"),
                                             vmem_limit_bytes=VMEM_LIMIT),
        name=kind + ("_attn_latent" if has_ctx else "_attn_context"),
    )(*args)


def _gla_kernel(blk_ref, seq_ref, first_ref, *refs, reverse):
    del blk_ref, seq_ref
    if reverse:
        gin_ref, s0_ref, he_ref, of_ref, gout_ref, ones_ref, o_ref, sout_ref, st_ref = refs
    else:
        gin_ref, s0_ref, he_ref, o_ref, sout_ref, st_ref = refs
    i = pl.program_id(0)

    @pl.when(first_ref[i] == 1)
    def _():
        st_ref[...] = s0_ref[0]

    R, S = TOKEN_TILE, GLA_STEP
    ns = R // S
    q = gin_ref[:, 0:128]
    k = gin_ref[:, 128:256]
    g = gin_ref[:, 384:512] if reverse else gin_ref[:, 256:384]
    v = gin_ref[:, 512:768]
    pos = lax.broadcasted_iota(I32, (R, 128), 0) % S
    b = g
    for s in (1, 2, 4, 8):
        if reverse:
            b = b + jnp.where(pos < S - s, pltpu.roll(b, R - s, 0), 0.0)
        else:
            b = b + jnp.where(pos >= s, pltpu.roll(b, s, 0), 0.0)

    def step_row(a, j):
        w = a.shape[-1]
        a3 = a.reshape(ns, S, w)
        return jnp.broadcast_to(a3[:, j:j + 1, :], (ns, S, w)).reshape(R, w)

    b_edge = step_row(b, 0 if reverse else S - 1)
    qt = (q * jnp.exp(b)).astype(BF16)
    kt = (k * jnp.exp(b_edge - b)).astype(BF16)
    d_edge = jnp.exp(b_edge)
    vb = v.astype(BF16)
    he = he_ref[...]

    o_intra = jnp.zeros((R, 256), F32)
    for j in range(S):
        cond = (pos <= j) if reverse else (pos >= j)
        bj, kj, vj = step_row(b, j), step_row(k, j), step_row(v, j)
        t = q * kj * jnp.exp(jnp.where(cond, b - bj, NEG_BIG))
        o_intra = o_intra + _dot(t.astype(BF16), he) * vj

    own_head = (lax.broadcasted_iota(I32, (256, 128), 0) // GLA_DV) == (lax.broadcasted_iota(I32, (256, 128), 1) // GLA_DK)
    st = st_ref[...]
    parts = [None] * ns
    for t in (range(ns - 1, -1, -1) if reverse else range(ns)):
        rows = slice(t * S, (t + 1) * S)
        parts[t] = _dot_nt(qt[rows], st.astype(BF16)) + o_intra[rows]
        ut = _dot_tn(vb[rows], kt[rows])
        st = d_edge[t * S:t * S + 1, :] * st + jnp.where(own_head, ut, 0.0)
    st_ref[...] = st
    sout_ref[0] = st
    o = jnp.concatenate(parts, axis=0)
    if reverse:
        o = o + of_ref[...]
        ms = _split_dot(o * o, ones_ref[...]) * (1.0 / GLA_DV)
        on = o * lax.rsqrt(ms + EPS) * gout_ref[...]
        lg = gin_ref[:, 768:1024]
        o_ref[...] = (on * (lg * jax.nn.sigmoid(lg))).astype(o_ref.dtype)
    else:
        o_ref[...] = o


def _gla(gin, s0, o_f, lw, order, *, reverse):
    blk, seq, first = order
    N = gin.shape[0]
    tm = TOKEN_TILE
    n_seq = s0.shape[0]
    tile = lambda i, b, s, f: (b[i], 0)
    state = lambda i, b, s, f: (s[i], 0, 0)
    const2 = lambda i, b, s, f: (0, 0)
    in_specs = [pl.BlockSpec((tm, 1024), tile), pl.BlockSpec((1, 256, 128), state), pl.BlockSpec((128, 256), const2)]
    args = [gin, s0, lw['head_expand']]
    if reverse:
        in_specs += [pl.BlockSpec((tm, 256), tile), pl.BlockSpec((1, 256), const2), pl.BlockSpec((256, 256), const2)]
        args += [o_f, lw['g_gla_out'], lw['ones256']]
    return pl.pallas_call(
        functools.partial(_gla_kernel, reverse=reverse),
        grid_spec=pltpu.PrefetchScalarGridSpec(
            num_scalar_prefetch=3, grid=(N // tm,), in_specs=in_specs,
            out_specs=[pl.BlockSpec((tm, 256), tile), pl.BlockSpec((1, 256, 128), state)],
            scratch_shapes=[pltpu.VMEM((256, 128), F32)]),
        out_shape=[jax.ShapeDtypeStruct((N, 256), BF16 if reverse else F32),
                   jax.ShapeDtypeStruct((n_seq, 256, 128), F32)],
        compiler_params=pltpu.CompilerParams(dimension_semantics=("arbitrary",), vmem_limit_bytes=VMEM_LIMIT),
        name="gla_bwd" if reverse else "gla_fwd",
    )(blk, seq, first, *args)


def _post_attn_kernel(modidx_ref, x_ref, omc_ref, oml_ref, ogc_ref, ogl_ref, ol_ref, mod_ref, wout_ref, gffn_ref,
                      wrh_ref, wrl_ref, br_ref, tri_ref, x1_ref, h2_ref, ti_ref, tg_ref, cnt_ref, carry_ref, *,
                      n_ctx_tiles):
    del modidx_ref
    mod = mod_ref[0]
    gate1, shift2, scale2 = mod[2:3, :], mod[3:4, :], mod[4:5, :]
    is_ctx = pl.program_id(0) < n_ctx_tiles
    om = jnp.where(is_ctx, omc_ref[...], oml_ref[...])
    og = jnp.where(is_ctx, ogc_ref[...], ogl_ref[...])
    o = _dot(om, wout_ref[0:256, :]) + _dot(og, wout_ref[256:768, :]) + _dot(ol_ref[...], wout_ref[768:1024, :])
    x1 = x_ref[...] + gate1 * o
    x1_ref[...] = x1
    xn = x1 * lax.rsqrt(jnp.mean(x1 * x1, axis=-1, keepdims=True) + EPS) * gffn_ref[...]
    h2 = xn * (1.0 + scale2) + shift2
    _to_token_tiles(h2_ref, h2)
    hi = h2.astype(BF16)
    lo = (h2 - hi.astype(F32)).astype(BF16)
    wrh = wrh_ref[...]
    logits = _dot(hi, wrh) + _dot(hi, wrl_ref[...]) + _dot(lo, wrh) + br_ref[...]

    lane = lax.broadcasted_iota(I32, logits.shape, 1).astype(F32)
    work = logits
    vals, idxs = [], []
    for _ in range(TOP_K):
        m = jnp.max(work, axis=-1, keepdims=True)
        idx = jnp.min(jnp.where(work == m, lane, float(LANES)), axis=-1, keepdims=True)
        vals.append(m)
        idxs.append(idx)
        work = jnp.where(lane == idx, NEG_BIG * 4.0, work)
    es = [jnp.exp(vv - vals[0]) for vv in vals]
    denom = es[0] + es[1] + es[2] + es[3]
    ti = jnp.zeros(logits.shape, F32)
    tg = jnp.zeros(logits.shape, F32)
    for r in range(TOP_K):
        ti = jnp.where(lane == float(r), idxs[r], ti)
        tg = jnp.where(lane == float(r), es[r] / denom, tg)
    tg_ref[...] = tg

    @pl.when(pl.program_id(0) == 0)
    def _():
        carry_ref[...] = jnp.zeros(carry_ref.shape, F32)

    picks = [lane == idxs[r] for r in range(TOP_K)]
    onehot = sum(jnp.where(pk, 1.0, 0.0) for pk in picks)
    carry = carry_ref[...]
    before = carry + _dot(tri_ref[...], onehot.astype(BF16))
    for r in range(TOP_K):
        rank = jnp.sum(jnp.where(picks[r], before, 0.0), axis=-1, keepdims=True)
        ti = jnp.where(lane == float(TOP_K + r), rank, ti)
    ti_ref[...] = ti.astype(I32)
    carry = carry + jnp.sum(onehot, axis=0, keepdims=True)
    carry_ref[...] = carry
    cnt_ref[...] = jnp.broadcast_to(carry, cnt_ref.shape)


def _post_attn(x, o_mla, o_gqa, o_gla, mod, modidx, lw, n_ctx_tiles):
    N, D = x.shape
    tm = TOKEN_TILE
    row = lambda i, mi: (i, 0)
    ctx_row = lambda i, mi: (jnp.minimum(i, n_ctx_tiles - 1), 0)
    lat_row = lambda i, mi: (jnp.maximum(i - n_ctx_tiles, 0), 0)
    const2 = lambda i, mi: (0, 0)
    return pl.pallas_call(
        functools.partial(_post_attn_kernel, n_ctx_tiles=n_ctx_tiles),
        grid_spec=pltpu.PrefetchScalarGridSpec(
            num_scalar_prefetch=1, grid=(N // tm,),
            in_specs=[pl.BlockSpec((tm, D), row), pl.BlockSpec((tm, 256), ctx_row), pl.BlockSpec((tm, 256), lat_row),
                      pl.BlockSpec((tm, 512), ctx_row), pl.BlockSpec((tm, 512), lat_row),
                      pl.BlockSpec((tm, 256), row), pl.BlockSpec((1, 6, D), lambda i, mi: (mi[i], 0, 0)),
                      pl.BlockSpec((D, D), const2), pl.BlockSpec((1, D), const2),
                      pl.BlockSpec((D, LANES), const2), pl.BlockSpec((D, LANES), const2),
                      pl.BlockSpec((1, LANES), const2), pl.BlockSpec((tm, tm), const2)],
            out_specs=[pl.BlockSpec((tm, D), row), pl.BlockSpec((tm * SUBLANES, LANES), row),
                       pl.BlockSpec((tm, LANES), row), pl.BlockSpec((tm, LANES), row),
                       pl.BlockSpec((SUBLANES, LANES), const2)],
            scratch_shapes=[pltpu.VMEM((1, LANES), F32)]),
        out_shape=[jax.ShapeDtypeStruct((N, D), F32), jax.ShapeDtypeStruct((N * SUBLANES, LANES), F32),
                   jax.ShapeDtypeStruct((N, LANES), I32), jax.ShapeDtypeStruct((N, LANES), F32),
                   jax.ShapeDtypeStruct((SUBLANES, LANES), F32)],
        compiler_params=pltpu.CompilerParams(dimension_semantics=("arbitrary",), vmem_limit_bytes=VMEM_LIMIT),
        name="post_attn",
    )(modidx, x, o_mla[0], o_mla[1], o_gqa[0], o_gqa[1], o_gla, mod, lw['w_out'], lw['g_ffn'], lw['w_router_hi'],
      lw['w_router_lo'], lw['b_router'], lw['tri'])


def _moe_kernel(be_ref, first_ref, valid_ref, src0_ref, src_ref, dst_ref, dstp_ref, h_hbm, wgu_ref, bgu_ref, wd_ref, bd_ref,
                y_hbm, wgu_bf, wd_bf, xbuf0, xbuf1, ybuf0, ybuf1, xsem0, xsem1, ysem0, ysem1):
    del be_ref
    i = pl.program_id(0)
    M = MOE_ROWS

    def gather(idx_ref, r, xb, sem):
        tok = idx_ref[0, 0, r]
        return pltpu.make_async_copy(h_hbm.at[pl.ds(pl.multiple_of(tok * SUBLANES, SUBLANES), SUBLANES), :],
                                     xb.at[pl.ds(r * SUBLANES, SUBLANES), :], sem)

    def scatter(idx_ref, r, yb, sem):
        row = idx_ref[0, 0, r]
        return pltpu.make_async_copy(yb.at[pl.ds(r * SUBLANES, SUBLANES), :],
                                     y_hbm.at[pl.ds(pl.multiple_of(row * SUBLANES, SUBLANES), SUBLANES), :], sem)

    def all_rows_in(xb, sem):
        return pltpu.make_async_copy(h_hbm.at[pl.ds(0, M * SUBLANES), :], xb, sem)

    def all_rows_out(yb, sem):
        return pltpu.make_async_copy(yb, y_hbm.at[pl.ds(0, M * SUBLANES), :], sem)

    @pl.when(i == 0)
    def _():
        ybuf0[...] = jnp.zeros(ybuf0.shape, F32)
        ybuf1[...] = jnp.zeros(ybuf1.shape, F32)

        def prime(r, carry):
            gather(src0_ref, r, xbuf0, xsem0).start()
            scatter(dstp_ref, r, ybuf0, ysem0).start()
            return carry
        lax.fori_loop(0, M, prime, 0)

    @pl.when(first_ref[i] == 1)
    def _():
        wgu_bf[...] = wgu_ref[...].astype(BF16)
        wd_bf[...] = wd_ref[...].astype(BF16)

    def step(xa, xsa, xo, xso, ya, ysa, yo, yso):
        here = valid_ref[i] == 1
        prev = valid_ref[jnp.maximum(i - 1, 0)] == 1

        @pl.when(prev)
        def _():
            all_rows_in(xa, xsa).wait()

        @pl.when(here)
        def _():
            for r in range(M):
                scatter(dst_ref, r, yo, yso).start(priority=1)
                gather(src_ref, r, xo, xso).start()
            gu = _dot(_from_token_tiles(xa, M).astype(BF16), wgu_bf[...]) + bgu_ref[...]
            gate = jnp.minimum(gu[:, :D_EXPERT], SWIGLU_LIMIT)
            up = jnp.clip(gu[:, D_EXPERT:], -SWIGLU_LIMIT, SWIGLU_LIMIT)
            act = gate * jax.nn.sigmoid(SWIGLU_ALPHA * gate) * (up + 1.0)
            y = _dot(act.astype(BF16), wd_bf[...]) + bd_ref[...]
            all_rows_out(ya, ysa).wait()
            _to_token_tiles(ya, y)

        @pl.when(jnp.logical_and(prev, jnp.logical_not(here)))
        def _():
            def move(r, carry):
                scatter(dst_ref, r, yo, yso).start(priority=1)
                return carry
            lax.fori_loop(0, M, move, 0)

    @pl.when(i % 2 == 0)
    def _():
        step(xbuf0, xsem0, xbuf1, xsem1, ybuf0, ysem0, ybuf1, ysem1)

    @pl.when(i % 2 == 1)
    def _():
        step(xbuf1, xsem1, xbuf0, xsem0, ybuf1, ysem1, ybuf0, ysem0)

    @pl.when(i == pl.num_programs(0) - 1)
    def _():
        all_rows_out(ybuf0, ysem0).wait()
        all_rows_out(ybuf1, ysem1).wait()


def _moe_experts(h2t, sched, w_gate_up, b_gate_up, w_down, b_down, layer, n_out_rows):
    block_e, first, valid, src, dst, dst_prime = sched
    n_blocks = block_e.shape[0]
    D = D_MODEL
    M = MOE_ROWS
    L, E = w_gate_up.shape[:2]
    wmap = lambda i, be, fi, va: (layer, be[i], 0, 0)
    smem = lambda imap: pl.BlockSpec((1, 1, M), imap, memory_space=pltpu.SMEM)
    return pl.pallas_call(
        _moe_kernel,
        grid_spec=pltpu.PrefetchScalarGridSpec(
            num_scalar_prefetch=3, grid=(n_blocks,),
            in_specs=[smem(lambda i, be, fi, va: (0, 0, 0)), smem(lambda i, be, fi, va: (i + 1, 0, 0)),
                      smem(lambda i, be, fi, va: (i, 0, 0)), smem(lambda i, be, fi, va: (0, 0, 0)),
                      pl.BlockSpec(memory_space=pl.ANY),
                      pl.BlockSpec((None, None, D, 2 * D_EXPERT), wmap),
                      pl.BlockSpec((None, None, 1, 2 * D_EXPERT), wmap),
                      pl.BlockSpec((None, None, D_EXPERT, D), wmap),
                      pl.BlockSpec((None, None, 1, D), wmap)],
            out_specs=pl.BlockSpec(memory_space=pl.ANY),
            scratch_shapes=[pltpu.VMEM((D, 2 * D_EXPERT), BF16), pltpu.VMEM((D_EXPERT, D), BF16),
                            ] + [pltpu.VMEM((M * SUBLANES, LANES), F32)] * 4 + [pltpu.SemaphoreType.DMA(())] * 4),
        out_shape=jax.ShapeDtypeStruct((n_out_rows * SUBLANES, LANES), F32),
        compiler_params=pltpu.CompilerParams(dimension_semantics=("arbitrary",), vmem_limit_bytes=VMEM_LIMIT),
        name="moe_experts",
    )(block_e, first, valid, src, src, dst, dst_prime, h2t, w_gate_up, b_gate_up.reshape(L, E, 1, 2 * D_EXPERT), w_down,
      b_down.reshape(L, E, 1, D))


def _route(top_idx, rank, counts, n_blocks):
    N = top_idx.shape[0]
    padded = (counts + MOE_ROWS - 1) // MOE_ROWS * MOE_ROWS
    pad_end = jnp.cumsum(padded)
    pad_start = pad_end - padded
    dest = pad_start[top_idx] + rank
    M, K = MOE_ROWS, TOP_K
    blk_start = jnp.arange(n_blocks, dtype=I32) * M
    valid = blk_start < pad_end[-1]
    block_e = jnp.minimum((blk_start[:, None] >= pad_end[None, :]).astype(I32).sum(axis=1), N_EXPERTS - 1)
    order = jnp.argsort(dest.reshape(-1)).astype(I32)
    first_pos = jnp.cumsum(counts) - counts
    slot_id = jnp.arange(n_blocks * M, dtype=I32)
    e_slot = jnp.repeat(block_e, M)
    offset = slot_id - pad_start[e_slot]
    real = jnp.repeat(valid, M) & (offset < counts[e_slot])
    slot_asg = jnp.where(real, order[jnp.clip(first_pos[e_slot] + offset, 0, N * K - 1)], -1)
    tok = jnp.where(real, slot_asg // K, 0)
    spare = K * N + ((slot_id // M) % 2) * M + slot_id % M
    out_row = jnp.where(real, (slot_asg % K) * N + tok, spare)
    src = jnp.concatenate([tok.reshape(n_blocks, 1, M), jnp.zeros((1, 1, M), I32)], axis=0)
    lead = (K * N + M + jnp.arange(M, dtype=I32)).reshape(1, 1, M)
    dst = jnp.concatenate([lead, out_row.reshape(n_blocks, 1, M)], axis=0)
    dst_prime = (K * N + jnp.arange(M, dtype=I32)).reshape(1, 1, M)
    n_valid = pad_end[-1] // M
    block_e = block_e[jnp.minimum(jnp.arange(n_blocks, dtype=I32), n_valid - 1)]
    prev_e = jnp.concatenate([jnp.full((1,), -1, I32), block_e[:-1]])
    first = (valid & (block_e != prev_e)).astype(I32)
    return (block_e, first, valid.astype(I32), src, dst, dst_prime)


def _final_kernel(modidx_ref, x_ref, *refs, n_ctx_tiles):
    del modidx_ref
    y_refs, (tg_ref, pmod_ref, g_ref, octx_ref, olat_ref) = refs[:TOP_K], refs[TOP_K:]
    x = x_ref[...] + pmod_ref[0, 5:6, :] * _combine_experts(y_refs, tg_ref)
    out = x * lax.rsqrt(jnp.mean(x * x, axis=-1, keepdims=True) + EPS) * g_ref[...]
    is_ctx = pl.program_id(0) < n_ctx_tiles

    @pl.when(is_ctx)
    def _():
        octx_ref[...] = out

    @pl.when(jnp.logical_not(is_ctx))
    def _():
        olat_ref[...] = out


def _final_norm(x, y, pmod, modidx, g_final, n_ctx_tiles):
    N, D = x.shape
    tm = TOKEN_TILE
    row = lambda i, mi: (i, 0)
    return pl.pallas_call(
        functools.partial(_final_kernel, n_ctx_tiles=n_ctx_tiles),
        grid_spec=pltpu.PrefetchScalarGridSpec(
            num_scalar_prefetch=1, grid=(N // tm,),
            in_specs=[pl.BlockSpec((tm, D), row)]
            + [pl.BlockSpec((tm * SUBLANES, LANES), functools.partial(lambda k, i, mi: (k * (N // tm) + i, 0), k))
               for k in range(TOP_K)]
            + [pl.BlockSpec((tm, LANES), row),
               pl.BlockSpec((1, 6, D), lambda i, mi: (mi[i], 0, 0)), pl.BlockSpec((1, D), lambda i, mi: (0, 0))],
            out_specs=[pl.BlockSpec((tm, D), lambda i, mi: (jnp.minimum(i, n_ctx_tiles - 1), 0)),
                       pl.BlockSpec((tm, D), lambda i, mi: (jnp.maximum(i - n_ctx_tiles, 0), 0))]),
        out_shape=[jax.ShapeDtypeStruct((n_ctx_tiles * tm, D), F32),
                   jax.ShapeDtypeStruct((N - n_ctx_tiles * tm, D), F32)],
        compiler_params=pltpu.CompilerParams(dimension_semantics=("arbitrary",), vmem_limit_bytes=VMEM_LIMIT),
        name="final_norm",
    )(modidx, x, *([y[0]] * TOP_K), y[1], pmod, g_final)


def _rope_tables(seq_len):
    pos = np.arange(seq_len)
    rowp = (pos // GRID_W).astype(np.float32)
    colp = (pos % GRID_W).astype(np.float32)

    def tables(rdim, copies):
        quarter = rdim // 4
        inv_freq = jnp.asarray(ROPE_THETA, F32) ** (-jnp.arange(quarter, dtype=F32) / quarter)
        ar = jnp.asarray(rowp)[:, None] * inv_freq[None, :]
        ac = jnp.asarray(colp)[:, None] * inv_freq[None, :]
        cos = jnp.concatenate([jnp.cos(ar), jnp.cos(ar), jnp.cos(ac), jnp.cos(ac)], axis=1)
        sin = jnp.concatenate([-jnp.sin(ar), jnp.sin(ar), -jnp.sin(ac), jnp.sin(ac)], axis=1)
        return jnp.tile(cos, (1, copies)), jnp.tile(sin, (1, copies))

    c64, s64 = tables(HEAD_DIM, 2)
    c32, s32 = tables(MLA_ROPE, 4)
    lat = jnp.concatenate([c64, s64, c32, s32], axis=1).reshape(seq_len // TOKEN_TILE, TOKEN_TILE, 512)
    ident = jnp.concatenate([jnp.ones((TOKEN_TILE, 128), F32), jnp.zeros((TOKEN_TILE, 128), F32)] * 2, axis=1)
    return jnp.concatenate([ident[None], lat], axis=0)


def _block_ones(n, blk):
    r = np.arange(n) // blk
    return jnp.asarray((r[:, None] == r[None, :]).astype(np.float32), BF16)


def kernel(x_prompt, x_sample, cache_mla_ckv, cache_mla_krope, cache_gqa_k, cache_gqa_v, state_gla, c, c_ctx, w_mod, b_mod, g_attn_norm, g_ffn_norm, w_in, g_q_a, w_uq, g_kv_a, w_ukv, g_q_head, g_k_head, w_gk_fwd, b_gk_fwd, w_gk_bwd, b_gk_bwd, g_gla_out, w_out, w_router, b_router, w_gate_up, b_gate_up, w_down, b_down, g_final):
    B, T, D = x_prompt.shape
    BD, TD, _ = x_sample.shape
    L = w_in.shape[0]
    P = cache_mla_ckv.shape[2]
    tm = TOKEN_TILE
    n_ctx = B * T
    N = n_ctx + BD * TD
    assert T == tm and TD % ATTN_TQ == 0 and n_ctx % TD == 0 and BD + 1 <= 16
    n_ctx_tiles = n_ctx // tm
    nt = N // tm
    tiles_per_lat = TD // tm

    tile_ids = np.arange(nt)
    lat_tile = np.maximum(tile_ids - n_ctx_tiles, 0)
    is_lat = tile_ids >= n_ctx_tiles
    modidx = jnp.asarray(np.where(is_lat, 1 + lat_tile // tiles_per_lat, 0), I32)
    tabidx = jnp.asarray(np.where(is_lat, 1 + lat_tile % tiles_per_lat, 0), I32)
    seq_of_tile = np.where(is_lat, B + lat_tile // tiles_per_lat, tile_ids)
    first_fwd = np.where(is_lat, lat_tile % tiles_per_lat == 0, True)
    last_fwd = np.where(is_lat, lat_tile % tiles_per_lat == tiles_per_lat - 1, True)
    order_fwd = (jnp.asarray(tile_ids, I32), jnp.asarray(seq_of_tile, I32), jnp.asarray(first_fwd, I32))
    rev = tile_ids[::-1]
    order_bwd = (jnp.asarray(rev, I32), jnp.asarray(seq_of_tile[rev], I32), jnp.asarray(last_fwd[rev], I32))
    n_seq = B + BD

    cvecs = jnp.zeros((16, D), F32).at[0].set(c_ctx).at[1:1 + BD].set(c)
    mods = _modulation(cvecs, w_mod, b_mod).reshape(L, 16, 6, D)

    o = np.cumsum([0, Q_LORA, KV_LORA, MLA_ROPE, 512, 128, 128, 128, 128, 256, 16, 16, 256])
    seg = lambda j: w_in[:, :, o[j]:o[j + 1]]
    w_in_r = jnp.concatenate(
        [seg(0), seg(1), seg(3), seg(4), seg(5), seg(6), seg(7), seg(8), seg(11), seg(2), seg(2), seg(2), seg(2),
         seg(9), seg(10), jnp.zeros((L, D, 128 - 2 * GLA_GATE_RANK), F32)], axis=-1).astype(BF16)
    uq = w_uq.reshape(L, Q_LORA, MLA_HEADS, MLA_QK)
    ukv = w_ukv.reshape(L, KV_LORA, MLA_HEADS, MLA_NOPE + MLA_V)
    w_comb = _fold_q_weights(uq[..., :MLA_NOPE].transpose(0, 2, 1, 3), ukv[..., :MLA_NOPE].transpose(0, 2, 1, 3))
    q_pad = jnp.zeros((L, Q_LORA, 256 - KV_LORA - MLA_ROPE), F32)
    w_q2 = (jnp.concatenate([part for h in range(MLA_HEADS)
                             for part in (w_comb[:, h], uq[:, :, h, MLA_NOPE:], q_pad)], axis=-1)
            * (MLA_QK ** -0.5 * LOG2E)).astype(BF16)
    w_uv = ukv[..., MLA_NOPE:].transpose(0, 2, 1, 3)
    eye_h = jnp.eye(MLA_HEADS, dtype=F32)
    wuv_pad = jnp.einsum('lhkv,hg->lhkgv', w_uv, eye_h).reshape(L, MLA_HEADS, KV_LORA, MLA_HEADS * MLA_V).astype(BF16)
    w_gate = jnp.zeros((L, 128, 256), F32).at[:, 0:16, 0:128].set(w_gk_fwd).at[:, 16:32, 128:256].set(w_gk_bwd)
    w_gate = w_gate.astype(BF16)
    b_gate = jnp.concatenate([b_gk_fwd, b_gk_bwd], axis=-1)
    w_out_bf = w_out.astype(BF16)
    wr_pad = jnp.pad(w_router, ((0, 0), (0, 0), (0, LANES - N_EXPERTS)))
    wr_hi = wr_pad.astype(BF16)
    wr_lo = (wr_pad - wr_hi.astype(F32)).astype(BF16)
    br_pad = jnp.pad(b_router, ((0, 0), (0, LANES - N_EXPERTS)), constant_values=NEG_BIG)
    ones512 = _block_ones(512, HEAD_DIM)
    tri = jnp.asarray((np.arange(tm)[None, :] < np.arange(tm)[:, None]).astype(np.float32), BF16)
    head_expand = jnp.asarray((np.arange(128)[:, None] // GLA_DK == np.arange(256)[None, :] // GLA_DV)
                              .astype(np.float32), BF16)
    tab = _rope_tables(TD)

    x = jnp.concatenate([x_prompt.reshape(n_ctx, D), x_sample.reshape(BD * TD, D)], axis=0)
    y = None
    n_blocks = (N * TOP_K) // MOE_ROWS + N_EXPERTS
    assert (N * TOP_K) % MOE_ROWS == 0 and n_blocks % 2 == 0
    caches, states = [], []
    for l in range(L):
        lw = dict(g_attn=g_attn_norm[l][None], w_in_r=w_in_r[l], g_q_a=g_q_a[l][None], w_q2=w_q2[l],
                  g_kv_a=g_kv_a[l][None], g_qh=jnp.tile(g_q_head[l], GQA_HEADS)[None],
                  g_kh=jnp.tile(g_k_head[l], GQA_KV_HEADS)[None], ones512=ones512, w_gate=w_gate[l],
                  b_gate=b_gate[l][None], head_expand=head_expand, g_gla_out=jnp.tile(g_gla_out[l], GLA_HEADS)[None],
                  ones256=ones512[:256, :256], w_out=w_out_bf[l], g_ffn=g_ffn_norm[l][None], tri=tri,
                  w_router_hi=wr_hi[l], w_router_lo=wr_lo[l], b_router=br_pad[l][None])
        pmod = mods[l - 1] if l > 0 else None
        x, qm, kvc, qg, kdup, vdup, gin, cache = _pre_attn(x, y, pmod, mods[l], tab, modidx, tabidx, lw, n_ctx_tiles)
        caches.append(cache[:n_ctx])

        mla_ctx = jnp.concatenate([cache_mla_ckv[:, l]] + [cache_mla_krope[:, l]] * 4, axis=-1).astype(BF16)
        ck = cache_gqa_k[:, l]
        cv = cache_gqa_v[:, l]
        k_ctx = jnp.concatenate([ck[:, :, 0], ck[:, :, 0], ck[:, :, 1], ck[:, :, 1]], axis=-1).astype(BF16)
        ones_v = jnp.ones(cv.shape[:2] + (HEAD_DIM,), F32)
        v_ctx = jnp.concatenate([cv[:, :, 0], ones_v, cv[:, :, 1], ones_v], axis=-1).astype(BF16)
        o_mla = (_attention('mla', qm, kvc, None, None, wuv_pad[l], n_seq=B, seq_len=T, tile_offset=0),
                 _attention('mla', qm, kvc, None, (mla_ctx,), wuv_pad[l], n_seq=BD, seq_len=TD,
                            tile_offset=n_ctx // TD))
        o_gqa = (_attention('gqa', qg, kdup, vdup, None, None, n_seq=B, seq_len=T, tile_offset=0),
                 _attention('gqa', qg, kdup, vdup, (k_ctx, v_ctx), None, n_seq=BD, seq_len=TD,
                            tile_offset=n_ctx // TD))

        eye_g = jnp.eye(GLA_HEADS, dtype=F32)
        st_lat = jnp.einsum('bshde,hg->bshegd', state_gla[:, l], eye_g).reshape(BD, 2, 256, 128)
        zeros_ctx = jnp.zeros((B, 256, 128), F32)
        o_f, s_f = _gla(gin, jnp.concatenate([zeros_ctx, st_lat[:, 0]], axis=0), None, lw, order_fwd, reverse=False)
        o_gla, s_b = _gla(gin, jnp.concatenate([zeros_ctx, st_lat[:, 1]], axis=0), o_f, lw, order_bwd, reverse=True)
        states.append((s_f[:B], s_b[:B]))

        x1, h2, ti, tg, cnt = _post_attn(x, o_mla, o_gqa, o_gla, mods[l], modidx, lw, n_ctx_tiles)
        sched = _route(ti[:, :TOP_K], ti[:, TOP_K:2 * TOP_K], cnt[0, :N_EXPERTS].astype(I32), n_blocks)
        y = (_moe_experts(h2, sched, w_gate_up, b_gate_up, w_down, b_down, l, TOP_K * N + 2 * MOE_ROWS), tg)
        x = x1

    out_ctx, out_lat = _final_norm(x, y, mods[L - 1], modidx, g_final[None], n_ctx_tiles)
    y_prompt = out_ctx.reshape(B, T, D)
    y_sample = out_lat.reshape(BD, TD, D)
    cache_all = jnp.stack([cc.reshape(B, T, 512) for cc in caches], axis=1)
    new_ckv = cache_all[..., 0:128]
    new_k = cache_all[..., 128:256].reshape(B, L, T, GQA_KV_HEADS, HEAD_DIM)
    new_v = cache_all[..., 256:384].reshape(B, L, T, GQA_KV_HEADS, HEAD_DIM)
    new_krope = cache_all[..., 384:384 + MLA_ROPE]

    def unpack_state(st):
        s5 = st.reshape(B, GLA_HEADS, GLA_DV, GLA_HEADS, GLA_DK)
        diag = jnp.stack([s5[:, h, :, h, :] for h in range(GLA_HEADS)], axis=1)
        return diag.transpose(0, 1, 3, 2)

    new_state = jnp.stack([jnp.stack([unpack_state(sf), unpack_state(sb)], axis=1) for sf, sb in states], axis=1)
    return (y_prompt, y_sample, new_ckv, new_krope, new_k, new_v, new_state)
```

```python
import functools

import jax
import jax.numpy as jnp
import numpy as np
from jax import lax
from jax.experimental import pallas as pl
from jax.experimental.pallas import tpu as pltpu

F32 = jnp.float32
BF16 = jnp.bfloat16
I32 = jnp.int32

D_MODEL = 1024
GRID_W = 64
ROPE_THETA = 10000.0
EPS = 1e-6
MLA_HEADS = 4
Q_LORA = 256
KV_LORA = 128
MLA_NOPE = 64
MLA_ROPE = 32
MLA_V = 64
MLA_QK = MLA_NOPE + MLA_ROPE
GQA_HEADS = 8
GQA_KV_HEADS = 2
GQA_GROUP = GQA_HEADS // GQA_KV_HEADS
HEAD_DIM = 64
GLA_HEADS = 4
GLA_DK = 32
GLA_DV = 64
GLA_GATE_RANK = 16
GLA_GATE_NORM = 16.0
N_EXPERTS = 32
TOP_K = 4
D_EXPERT = D_MODEL
SWIGLU_LIMIT = 7.0
SWIGLU_ALPHA = 1.702

LANES = 128
SUBLANES = 8
TOKEN_TILE = 256
GLA_STEP = 16
MOE_ROWS = 512
ATTN_TQ = {'mla': 1024, 'gqa': 512}
ATTN_TK = 512
VMEM_LIMIT = 56 * 1024 * 1024
NEG_BIG = -1e30
LOG2E = 1.4426950408889634

C_CQ, C_CKV, C_GQ, C_GK, C_GV = 0, 256, 384, 896, 1024
C_LQ, C_LK, C_LV, C_LG, C_KR, C_GATE = 1152, 1280, 1408, 1664, 1920, 2048
IN_COLS = 2176


def _dot(a, b):
    return jnp.dot(a, b, preferred_element_type=F32)


def _dot_nt(a, b):
    return lax.dot_general(a, b, (((1,), (1,)), ((), ())), preferred_element_type=F32)


def _dot_tn(a, b):
    return lax.dot_general(a, b, (((0,), (0,)), ((), ())), preferred_element_type=F32)


def _split_dot(x, ones_bf):
    hi = x.astype(BF16)
    lo = (x - hi.astype(F32)).astype(BF16)
    return _dot(hi, ones_bf) + _dot(lo, ones_bf)


def _rope(x, cos, sin_signed, quarter):
    width = x.shape[-1]
    lane = lax.broadcasted_iota(I32, x.shape, 1)
    first = (lane % (2 * quarter)) < quarter
    partner = jnp.where(first, pltpu.roll(x, width - quarter, 1), pltpu.roll(x, quarter, 1))
    return x * cos + partner * sin_signed


def _mod_kernel(c_ref, w_ref, b_ref, o_ref):
    c = c_ref[...]
    a = c * jax.nn.sigmoid(c)
    o_ref[0] = jnp.dot(a, w_ref[0], precision=lax.Precision.HIGHEST, preferred_element_type=F32) + b_ref[0]


def _modulation(cvecs, w_mod, b_mod):
    L, D, D6 = w_mod.shape
    tn = 1536
    return pl.pallas_call(
        _mod_kernel,
        grid=(L, D6 // tn),
        in_specs=[pl.BlockSpec((16, D), lambda l, j: (0, 0)),
                  pl.BlockSpec((1, D, tn), lambda l, j: (l, 0, j)),
                  pl.BlockSpec((1, 1, tn), lambda l, j: (l, 0, j))],
        out_specs=pl.BlockSpec((1, 16, tn), lambda l, j: (l, 0, j)),
        out_shape=jax.ShapeDtypeStruct((L, 16, D6), F32),
        compiler_params=pltpu.CompilerParams(dimension_semantics=("arbitrary", "arbitrary"),
                                             vmem_limit_bytes=VMEM_LIMIT),
        name="modulation",
    )(cvecs, w_mod, b_mod.reshape(L, 1, D6))


def _fold_kernel(a_ref, b_ref, o_ref):
    o_ref[0, 0] = lax.dot_general(a_ref[0, 0], b_ref[0, 0], (((1,), (1,)), ((), ())),
                                  precision=lax.Precision.HIGHEST, preferred_element_type=F32)


def _fold_q_weights(wq_nope, wk_nope):
    L, H, A, K = wq_nope.shape
    B = wk_nope.shape[2]
    return pl.pallas_call(
        _fold_kernel,
        grid=(L, H),
        in_specs=[pl.BlockSpec((1, 1, A, K), lambda l, h: (l, h, 0, 0)),
                  pl.BlockSpec((1, 1, B, K), lambda l, h: (l, h, 0, 0))],
        out_specs=pl.BlockSpec((1, 1, A, B), lambda l, h: (l, h, 0, 0)),
        out_shape=jax.ShapeDtypeStruct((L, H, A, B), F32),
        name="fold_q_weights",
    )(wq_nope, wk_nope)


def _from_token_tiles(ref, rows):
    return jnp.concatenate([ref[pl.ds(c, rows, stride=SUBLANES), :] for c in range(D_MODEL // LANES)], axis=1)


def _to_token_tiles(ref, x):
    for c in range(D_MODEL // LANES):
        ref[pl.ds(c, x.shape[0], stride=SUBLANES), :] = x[:, c * LANES:(c + 1) * LANES]


def _combine_experts(y_refs, tg_ref):
    tg = tg_ref[...]
    acc = None
    for k in range(TOP_K):
        term = tg[:, k:k + 1] * _from_token_tiles(y_refs[k], tg.shape[0])
        acc = term if acc is None else acc + term
    return acc


def _pre_attn_kernel(modidx_ref, tabidx_ref, *refs, has_y):
    del modidx_ref, tabidx_ref
    if has_y:
        x_ref, y_ref, tg_ref, pmod_ref = refs[0], refs[1:1 + TOP_K], refs[1 + TOP_K], refs[2 + TOP_K]
        refs = refs[3 + TOP_K:]
    else:
        x_ref = refs[0]
        refs = refs[1:]
    (mod_ref, tab_ref, gattn_ref, win_ref, gqa_ref, wq2_ref, gkva_ref, gqh_ref, gkh_ref, ones_ref,
     wg_ref, bg_ref) = refs[:12]
    outs = refs[12:]
    if has_y:
        xres_ref = outs[0]
        outs = outs[1:]
    qm_ref, kvc_ref, qg_ref, kdup_ref, vdup_ref, gla_ref, cache_ref = outs

    x = x_ref[...]
    if has_y:
        x = x + pmod_ref[0, 5:6, :] * _combine_experts(y_ref, tg_ref)
        xres_ref[...] = x
    mod = mod_ref[0]
    shift1, scale1 = mod[0:1, :], mod[1:2, :]
    xn = x * lax.rsqrt(jnp.mean(x * x, axis=-1, keepdims=True) + EPS) * gattn_ref[...]
    h = xn * (1.0 + scale1) + shift1
    p = _dot(h.astype(BF16), win_ref[...])

    tab = tab_ref[0]
    cos64, sin64 = tab[:, 0:128], tab[:, 128:256]
    cos32, sin32 = tab[:, 256:384], tab[:, 384:512]
    ones = ones_ref[...]

    cq = p[:, C_CQ:C_CQ + Q_LORA]
    cqn = cq * lax.rsqrt(jnp.mean(cq * cq, axis=-1, keepdims=True) + EPS) * gqa_ref[...]
    qm = _dot(cqn.astype(BF16), wq2_ref[...])
    q_rope = _rope(jnp.concatenate([qm[:, h * 256 + 128:(h + 1) * 256] for h in range(MLA_HEADS)], axis=1),
                   jnp.concatenate([cos32] * MLA_HEADS, axis=1), jnp.concatenate([sin32] * MLA_HEADS, axis=1),
                   MLA_ROPE // 4)
    for h in range(MLA_HEADS):
        qm_ref[:, h * 256:h * 256 + 128] = qm[:, h * 256:h * 256 + 128].astype(BF16)
        qm_ref[:, h * 256 + 128:(h + 1) * 256] = q_rope[:, h * 128:(h + 1) * 128].astype(BF16)
    ckv_raw = p[:, C_CKV:C_CKV + KV_LORA]
    ckv = ckv_raw * lax.rsqrt(jnp.mean(ckv_raw * ckv_raw, axis=-1, keepdims=True) + EPS) * gkva_ref[...]
    kr4 = p[:, C_KR:C_KR + 128]
    kr4_rot = _rope(kr4, cos32, sin32, MLA_ROPE // 4)
    kvc_ref[:, 0:128] = ckv.astype(BF16)
    kvc_ref[:, 128:256] = kr4_rot.astype(BF16)

    gq = p[:, C_GQ:C_GQ + 512]
    ms_q = _split_dot(gq * gq, ones) * (1.0 / HEAD_DIM)
    qn = gq * lax.rsqrt(ms_q + EPS) * gqh_ref[...]
    cos_q = jnp.concatenate([cos64] * 4, axis=1)
    sin_q = jnp.concatenate([sin64] * 4, axis=1)
    qg = _rope(qn, cos_q, sin_q, HEAD_DIM // 4) * (HEAD_DIM ** -0.5 * LOG2E)
    qg_ref[...] = qg.astype(BF16)
    gk = p[:, C_GK:C_GK + 128]
    ms_k = _split_dot(gk * gk, ones[0:128, 0:128]) * (1.0 / HEAD_DIM)
    kn = gk * lax.rsqrt(ms_k + EPS) * gkh_ref[...]
    kg = _rope(kn, cos64, sin64, HEAD_DIM // 4)
    gv = p[:, C_GV:C_GV + 128]
    lane = lax.broadcasted_iota(I32, kg.shape, 1)
    low = lane < HEAD_DIM
    kg_sw = pltpu.roll(kg, HEAD_DIM, 1)
    gv_sw = pltpu.roll(gv, HEAD_DIM, 1)
    kdup_ref[:, 0:128] = jnp.where(low, kg, kg_sw).astype(BF16)
    kdup_ref[:, 128:256] = jnp.where(low, kg_sw, kg).astype(BF16)
    vdup_ref[:, 0:128] = jnp.where(low, gv, 1.0).astype(BF16)
    vdup_ref[:, 128:256] = jnp.where(low, gv_sw, 1.0).astype(BF16)

    z = _dot(p[:, C_GATE:C_GATE + 128].astype(BF16), wg_ref[...]) + bg_ref[...]
    logsig = jnp.minimum(z, 0.0) - jnp.log1p(jnp.exp(-jnp.abs(z)))
    gla_ref[:, 0:128] = p[:, C_LQ:C_LQ + 128] * (GLA_DK ** -0.5)
    gla_ref[:, 128:256] = p[:, C_LK:C_LK + 128]
    gla_ref[:, 256:512] = logsig * (1.0 / GLA_GATE_NORM)
    gla_ref[:, 512:768] = p[:, C_LV:C_LV + 256]
    gla_ref[:, 768:1024] = p[:, C_LG:C_LG + 256]

    cache_ref[:, 0:128] = ckv
    cache_ref[:, 128:256] = kn
    cache_ref[:, 256:384] = gv
    cache_ref[:, 384:512] = kr4


def _pre_attn(x, y, pmod, mod, tab, modidx, tabidx, lw, n_ctx_tiles):
    N, D = x.shape
    tm = TOKEN_TILE
    nt = N // tm
    has_y = y is not None
    row = lambda i, mi, ti: (i, 0)
    const2 = lambda i, mi, ti: (0, 0)
    in_specs = [pl.BlockSpec((tm, D), row)]
    args = [x]
    if has_y:
        yk, gates = y
        in_specs += [pl.BlockSpec((tm * SUBLANES, LANES), functools.partial(lambda k, i, mi, ti: (k * nt + i, 0), k))
                     for k in range(TOP_K)]
        in_specs += [pl.BlockSpec((tm, LANES), row), pl.BlockSpec((1, 6, D), lambda i, mi, ti: (mi[i], 0, 0))]
        args += [yk] * TOP_K + [gates, pmod]
    in_specs += [
        pl.BlockSpec((1, 6, D), lambda i, mi, ti: (mi[i], 0, 0)),
        pl.BlockSpec((1, tm, 512), lambda i, mi, ti: (ti[i], 0, 0)),
        pl.BlockSpec((1, D), const2),
        pl.BlockSpec((D, IN_COLS), const2),
        pl.BlockSpec((1, Q_LORA), const2),
        pl.BlockSpec((Q_LORA, 1024), const2),
        pl.BlockSpec((1, KV_LORA), const2),
        pl.BlockSpec((1, 512), const2),
        pl.BlockSpec((1, 128), const2),
        pl.BlockSpec((512, 512), const2),
        pl.BlockSpec((128, 256), const2),
        pl.BlockSpec((1, 256), const2),
    ]
    args += [mod, tab, lw['g_attn'], lw['w_in_r'], lw['g_q_a'], lw['w_q2'], lw['g_kv_a'], lw['g_qh'], lw['g_kh'],
             lw['ones512'], lw['w_gate'], lw['b_gate']]
    out_shape, out_specs = [], []
    if has_y:
        out_shape.append(jax.ShapeDtypeStruct((N, D), F32))
        out_specs.append(pl.BlockSpec((tm, D), row))
    for width, dt in ((1024, BF16), (256, BF16), (512, BF16), (256, BF16), (256, BF16), (1024, F32)):
        out_shape.append(jax.ShapeDtypeStruct((N, width), dt))
        out_specs.append(pl.BlockSpec((tm, width), row))
    out_shape.append(jax.ShapeDtypeStruct(((n_ctx_tiles + 1) * tm, 512), F32))
    out_specs.append(pl.BlockSpec((tm, 512), lambda i, mi, ti: (jnp.minimum(i, n_ctx_tiles), 0)))
    outs = pl.pallas_call(
        functools.partial(_pre_attn_kernel, has_y=has_y),
        grid_spec=pltpu.PrefetchScalarGridSpec(num_scalar_prefetch=2, grid=(nt,), in_specs=in_specs,
                                               out_specs=out_specs),
        out_shape=out_shape,
        compiler_params=pltpu.CompilerParams(dimension_semantics=("arbitrary",), vmem_limit_bytes=VMEM_LIMIT),
        name="pre_attn",
    )(modidx, tabidx, *args)
    if not has_y:
        outs = [x] + list(outs)
    return outs


def _softmax_step(qs, k, m_ref, c):
    s = _dot_nt(qs, k)
    m_old = m_ref[c]
    m_new = jnp.maximum(m_old, jnp.broadcast_to(jnp.max(s, axis=-1, keepdims=True), m_old.shape))
    alpha = jnp.exp2(m_old - m_new)
    p = jnp.exp2(s - jnp.concatenate([m_new] * (k.shape[0] // LANES), axis=1))
    m_ref[c] = m_new
    return alpha, p


def _mla_attn_kernel(*refs, has_ctx, n_chunks, tk):
    if has_ctx:
        q_ref, kc_ref, k_ref, wuv_ref, o_ref, m_ref, l_ref, acc_ref = refs
    else:
        q_ref, k_ref, wuv_ref, o_ref, m_ref, l_ref, acc_ref = refs
        kc_ref = None
    tq = q_ref.shape[0]
    m_ref[...] = jnp.full(m_ref.shape, NEG_BIG, F32)
    l_ref[...] = jnp.zeros(l_ref.shape, F32)
    acc_ref[...] = jnp.zeros(acc_ref.shape, F32)
    q = q_ref[...]
    qs = [jnp.concatenate([q[:, h * 256:(h + 1) * 256] for h in (2 * c, 2 * c + 1)], axis=0) for c in range(2)]

    def process(kblk):
        for c in range(2):
            alpha, p = _softmax_step(qs[c], kblk, m_ref, c)
            l_ref[c] = alpha * l_ref[c] + jnp.broadcast_to(jnp.sum(p, axis=-1, keepdims=True), alpha.shape)
            acc_ref[c] = alpha * acc_ref[c] + _dot(p.astype(BF16), kblk[:, 0:KV_LORA])

    if has_ctx:
        process(kc_ref[0])

    def body(j, carry):
        process(k_ref[pl.ds(pl.multiple_of(j * tk, tk), tk), :])
        return carry

    lax.fori_loop(0, n_chunks, body, 0)
    out = jnp.zeros(o_ref.shape, F32)
    for h in range(MLA_HEADS):
        rows = slice((h % 2) * tq, (h % 2 + 1) * tq)
        o_lat = acc_ref[h // 2, rows, :] / l_ref[h // 2, rows, :]
        out = out + _dot(o_lat.astype(BF16), wuv_ref[h])
    o_ref[...] = out.astype(o_ref.dtype)


def _gqa_attn_kernel(*refs, has_ctx, n_chunks, tk):
    if has_ctx:
        q_ref, kc_ref, vc_ref, k_ref, v_ref, o_ref, m_ref, acc_ref = refs
    else:
        q_ref, k_ref, v_ref, o_ref, m_ref, acc_ref = refs
        kc_ref = vc_ref = None
    tq = q_ref.shape[0]
    m_ref[...] = jnp.full(m_ref.shape, NEG_BIG, F32)
    acc_ref[...] = jnp.zeros(acc_ref.shape, F32)
    q = q_ref[...]
    lane = lax.broadcasted_iota(I32, (tq, LANES), 1)
    low = lane < HEAD_DIM
    qs = []
    for g in range(GQA_KV_HEADS):
        parts = []
        for h in range(g * GQA_GROUP, (g + 1) * GQA_GROUP):
            pair = q[:, (h // 2) * LANES:(h // 2 + 1) * LANES]
            keep = low if h % 2 == 0 else jnp.logical_not(low)
            parts.append(jnp.where(keep, pair, jnp.zeros_like(pair)))
        qs.append(jnp.concatenate(parts, axis=0))

    def process(kblk, vblk):
        for g in range(GQA_KV_HEADS):
            alpha, p = _softmax_step(qs[g], kblk[:, g * LANES:(g + 1) * LANES], m_ref, g)
            acc_ref[g] = alpha * acc_ref[g] + _dot(p.astype(BF16), vblk[:, g * LANES:(g + 1) * LANES])

    if has_ctx:
        process(kc_ref[0], vc_ref[0])

    def body(j, carry):
        sl = pl.ds(pl.multiple_of(j * tk, tk), tk)
        process(k_ref[sl, :], v_ref[sl, :])
        return carry

    lax.fori_loop(0, n_chunks, body, 0)
    for j in range(GQA_HEADS // 2):
        outs = []
        for h in (2 * j, 2 * j + 1):
            a = acc_ref[h // GQA_GROUP, (h % GQA_GROUP) * tq:(h % GQA_GROUP + 1) * tq, :]
            outs.append(a / jnp.where(low, pltpu.roll(a, HEAD_DIM, 1), a))
        o_ref[:, j * LANES:(j + 1) * LANES] = jnp.where(low, outs[0], pltpu.roll(outs[1], HEAD_DIM, 1)).astype(
            o_ref.dtype)


def _attention(kind, q, k, v, ctx, wuv, *, n_seq, seq_len, tile_offset):
    has_ctx = ctx is not None
    tq = min(ATTN_TQ[kind], seq_len)
    tk = min(ATTN_TK, seq_len)
    nq = seq_len // tq
    qw = q.shape[1]
    ow = 256 if kind == 'mla' else 512
    heads = MLA_HEADS if kind == 'mla' else GQA_HEADS
    qmap = lambda b, i: (tile_offset * nq + b * nq + i, 0)
    kmap = lambda b, i: (tile_offset + b, 0)
    cmap = lambda b, i: (b, 0, 0)
    in_specs = [pl.BlockSpec((tq, qw), qmap)]
    args = [q]
    if kind == 'mla':
        if has_ctx:
            in_specs.append(pl.BlockSpec((1,) + ctx[0].shape[1:], cmap))
            args.append(ctx[0])
        in_specs += [pl.BlockSpec((seq_len, 256), kmap), pl.BlockSpec((MLA_HEADS, 128, 256), lambda b, i: (0, 0, 0))]
        args += [k, wuv]
        body = _mla_attn_kernel
    else:
        if has_ctx:
            in_specs += [pl.BlockSpec((1,) + ctx[0].shape[1:], cmap), pl.BlockSpec((1,) + ctx[1].shape[1:], cmap)]
            args += [ctx[0], ctx[1]]
        in_specs += [pl.BlockSpec((seq_len, 256), kmap), pl.BlockSpec((seq_len, 256), kmap)]
        args += [k, v]
        body = _gqa_attn_kernel
    return pl.pallas_call(
        functools.partial(body, has_ctx=has_ctx, n_chunks=seq_len // tk, tk=tk),
        grid=(n_seq, nq),
        in_specs=in_specs,
        out_specs=pl.BlockSpec((tq, ow), lambda b, i: (b * nq + i, 0)),
        out_shape=jax.ShapeDtypeStruct((n_seq * seq_len, ow), BF16),
        scratch_shapes=[pltpu.VMEM((2, heads // 2 * tq, LANES), F32)] * (3 if kind == 'mla' else 2),
        compiler_params=pltpu.CompilerParams(dimension_semantics=("arbitrary", "arbitrary"),
                                             vmem_limit_bytes=VMEM_LIMIT),
        name=kind + ("_attn_latent" if has_ctx else "_attn_context"),
    )(*args)


def _gla_kernel(blk_ref, seq_ref, first_ref, *refs, reverse):
    del blk_ref, seq_ref
    if reverse:
        gin_ref, s0_ref, he_ref, of_ref, gout_ref, ones_ref, o_ref, sout_ref, st_ref = refs
    else:
        gin_ref, s0_ref, he_ref, o_ref, sout_ref, st_ref = refs
    i = pl.program_id(0)

    @pl.when(first_ref[i] == 1)
    def _():
        st_ref[...] = s0_ref[0]

    R, S = TOKEN_TILE, GLA_STEP
    ns = R // S
    q = gin_ref[:, 0:128]
    k = gin_ref[:, 128:256]
    g = gin_ref[:, 384:512] if reverse else gin_ref[:, 256:384]
    v = gin_ref[:, 512:768]
    pos = lax.broadcasted_iota(I32, (R, 128), 0) % S
    b = g
    for s in (1, 2, 4, 8):
        if reverse:
            b = b + jnp.where(pos < S - s, pltpu.roll(b, R - s, 0), 0.0)
        else:
            b = b + jnp.where(pos >= s, pltpu.roll(b, s, 0), 0.0)

    def step_row(a, j):
        w = a.shape[-1]
        a3 = a.reshape(ns, S, w)
        return jnp.broadcast_to(a3[:, j:j + 1, :], (ns, S, w)).reshape(R, w)

    b_edge = step_row(b, 0 if reverse else S - 1)
    qt = (q * jnp.exp(b)).astype(BF16)
    kt = (k * jnp.exp(b_edge - b)).astype(BF16)
    d_edge = jnp.exp(b_edge)
    vb = v.astype(BF16)
    he = he_ref[...]

    o_intra = jnp.zeros((R, 256), F32)
    for j in range(S):
        cond = (pos <= j) if reverse else (pos >= j)
        bj, kj, vj = step_row(b, j), step_row(k, j), step_row(v, j)
        t = q * kj * jnp.exp(jnp.where(cond, b - bj, NEG_BIG))
        o_intra = o_intra + _dot(t.astype(BF16), he) * vj

    own_head = (lax.broadcasted_iota(I32, (256, 128), 0) // GLA_DV) == (lax.broadcasted_iota(I32, (256, 128), 1) // GLA_DK)
    st = st_ref[...]
    parts = [None] * ns
    for t in (range(ns - 1, -1, -1) if reverse else range(ns)):
        rows = slice(t * S, (t + 1) * S)
        parts[t] = _dot_nt(qt[rows], st.astype(BF16)) + o_intra[rows]
        ut = _dot_tn(vb[rows], kt[rows])
        st = d_edge[t * S:t * S + 1, :] * st + jnp.where(own_head, ut, 0.0)
    st_ref[...] = st
    sout_ref[0] = st
    o = jnp.concatenate(parts, axis=0)
    if reverse:
        o = o + of_ref[...]
        ms = _split_dot(o * o, ones_ref[...]) * (1.0 / GLA_DV)
        on = o * lax.rsqrt(ms + EPS) * gout_ref[...]
        lg = gin_ref[:, 768:1024]
        o_ref[...] = (on * (lg * jax.nn.sigmoid(lg))).astype(o_ref.dtype)
    else:
        o_ref[...] = o


def _gla(gin, s0, o_f, lw, order, *, reverse):
    blk, seq, first = order
    N = gin.shape[0]
    tm = TOKEN_TILE
    n_seq = s0.shape[0]
    tile = lambda i, b, s, f: (b[i], 0)
    state = lambda i, b, s, f: (s[i], 0, 0)
    const2 = lambda i, b, s, f: (0, 0)
    in_specs = [pl.BlockSpec((tm, 1024), tile), pl.BlockSpec((1, 256, 128), state), pl.BlockSpec((128, 256), const2)]
    args = [gin, s0, lw['head_expand']]
    if reverse:
        in_specs += [pl.BlockSpec((tm, 256), tile), pl.BlockSpec((1, 256), const2), pl.BlockSpec((256, 256), const2)]
        args += [o_f, lw['g_gla_out'], lw['ones256']]
    return pl.pallas_call(
        functools.partial(_gla_kernel, reverse=reverse),
        grid_spec=pltpu.PrefetchScalarGridSpec(
            num_scalar_prefetch=3, grid=(N // tm,), in_specs=in_specs,
            out_specs=[pl.BlockSpec((tm, 256), tile), pl.BlockSpec((1, 256, 128), state)],
            scratch_shapes=[pltpu.VMEM((256, 128), F32)]),
        out_shape=[jax.ShapeDtypeStruct((N, 256), BF16 if reverse else F32),
                   jax.ShapeDtypeStruct((n_seq, 256, 128), F32)],
        compiler_params=pltpu.CompilerParams(dimension_semantics=("arbitrary",), vmem_limit_bytes=VMEM_LIMIT),
        name="gla_bwd" if reverse else "gla_fwd",
    )(blk, seq, first, *args)


def _post_attn_kernel(modidx_ref, x_ref, omc_ref, oml_ref, ogc_ref, ogl_ref, ol_ref, mod_ref, wout_ref, gffn_ref,
                      wrh_ref, wrl_ref, br_ref, tri_ref, x1_ref, h2_ref, ti_ref, tg_ref, cnt_ref, carry_ref, *,
                      n_ctx_tiles):
    del modidx_ref
    mod = mod_ref[0]
    gate1, shift2, scale2 = mod[2:3, :], mod[3:4, :], mod[4:5, :]
    is_ctx = pl.program_id(0) < n_ctx_tiles
    om = jnp.where(is_ctx, omc_ref[...], oml_ref[...])
    og = jnp.where(is_ctx, ogc_ref[...], ogl_ref[...])
    o = _dot(om, wout_ref[0:256, :]) + _dot(og, wout_ref[256:768, :]) + _dot(ol_ref[...], wout_ref[768:1024, :])
    x1 = x_ref[...] + gate1 * o
    x1_ref[...] = x1
    xn = x1 * lax.rsqrt(jnp.mean(x1 * x1, axis=-1, keepdims=True) + EPS) * gffn_ref[...]
    h2 = xn * (1.0 + scale2) + shift2
    _to_token_tiles(h2_ref, h2)
    hi = h2.astype(BF16)
    lo = (h2 - hi.astype(F32)).astype(BF16)
    wrh = wrh_ref[...]
    logits = _dot(hi, wrh) + _dot(hi, wrl_ref[...]) + _dot(lo, wrh) + br_ref[...]

    lane = lax.broadcasted_iota(I32, logits.shape, 1).astype(F32)
    work = logits
    vals, idxs = [], []
    for _ in range(TOP_K):
        m = jnp.max(work, axis=-1, keepdims=True)
        idx = jnp.min(jnp.where(work == m, lane, float(LANES)), axis=-1, keepdims=True)
        vals.append(m)
        idxs.append(idx)
        work = jnp.where(lane == idx, NEG_BIG * 4.0, work)
    es = [jnp.exp(vv - vals[0]) for vv in vals]
    denom = es[0] + es[1] + es[2] + es[3]
    ti = jnp.zeros(logits.shape, F32)
    tg = jnp.zeros(logits.shape, F32)
    for r in range(TOP_K):
        ti = jnp.where(lane == float(r), idxs[r], ti)
        tg = jnp.where(lane == float(r), es[r] / denom, tg)
    tg_ref[...] = tg

    @pl.when(pl.program_id(0) == 0)
    def _():
        carry_ref[...] = jnp.zeros(carry_ref.shape, F32)

    picks = [lane == idxs[r] for r in range(TOP_K)]
    onehot = sum(jnp.where(pk, 1.0, 0.0) for pk in picks)
    carry = carry_ref[...]
    before = carry + _dot(tri_ref[...], onehot.astype(BF16))
    for r in range(TOP_K):
        rank = jnp.sum(jnp.where(picks[r], before, 0.0), axis=-1, keepdims=True)
        ti = jnp.where(lane == float(TOP_K + r), rank, ti)
    ti_ref[...] = ti.astype(I32)
    carry = carry + jnp.sum(onehot, axis=0, keepdims=True)
    carry_ref[...] = carry
    cnt_ref[...] = jnp.broadcast_to(carry, cnt_ref.shape)


def _post_attn(x, o_mla, o_gqa, o_gla, mod, modidx, lw, n_ctx_tiles):
    N, D = x.shape
    tm = TOKEN_TILE
    row = lambda i, mi: (i, 0)
    ctx_row = lambda i, mi: (jnp.minimum(i, n_ctx_tiles - 1), 0)
    lat_row = lambda i, mi: (jnp.maximum(i - n_ctx_tiles, 0), 0)
    const2 = lambda i, mi: (0, 0)
    return pl.pallas_call(
        functools.partial(_post_attn_kernel, n_ctx_tiles=n_ctx_tiles),
        grid_spec=pltpu.PrefetchScalarGridSpec(
            num_scalar_prefetch=1, grid=(N // tm,),
            in_specs=[pl.BlockSpec((tm, D), row), pl.BlockSpec((tm, 256), ctx_row), pl.BlockSpec((tm, 256), lat_row),
                      pl.BlockSpec((tm, 512), ctx_row), pl.BlockSpec((tm, 512), lat_row),
                      pl.BlockSpec((tm, 256), row), pl.BlockSpec((1, 6, D), lambda i, mi: (mi[i], 0, 0)),
                      pl.BlockSpec((D, D), const2), pl.BlockSpec((1, D), const2),
                      pl.BlockSpec((D, LANES), const2), pl.BlockSpec((D, LANES), const2),
                      pl.BlockSpec((1, LANES), const2), pl.BlockSpec((tm, tm), const2)],
            out_specs=[pl.BlockSpec((tm, D), row), pl.BlockSpec((tm * SUBLANES, LANES), row),
                       pl.BlockSpec((tm, LANES), row), pl.BlockSpec((tm, LANES), row),
                       pl.BlockSpec((SUBLANES, LANES), const2)],
            scratch_shapes=[pltpu.VMEM((1, LANES), F32)]),
        out_shape=[jax.ShapeDtypeStruct((N, D), F32), jax.ShapeDtypeStruct((N * SUBLANES, LANES), F32),
                   jax.ShapeDtypeStruct((N, LANES), I32), jax.ShapeDtypeStruct((N, LANES), F32),
                   jax.ShapeDtypeStruct((SUBLANES, LANES), F32)],
        compiler_params=pltpu.CompilerParams(dimension_semantics=("arbitrary",), vmem_limit_bytes=VMEM_LIMIT),
        name="post_attn",
    )(modidx, x, o_mla[0], o_mla[1], o_gqa[0], o_gqa[1], o_gla, mod, lw['w_out'], lw['g_ffn'], lw['w_router_hi'],
      lw['w_router_lo'], lw['b_router'], lw['tri'])


def _moe_kernel(be_ref, first_ref, valid_ref, src0_ref, src_ref, dst_ref, dstp_ref, h_hbm, wgu_ref, bgu_ref, wd_ref, bd_ref,
                y_hbm, wgu_bf, wd_bf, xbuf0, xbuf1, ybuf0, ybuf1, xsem0, xsem1, ysem0, ysem1):
    del be_ref
    i = pl.program_id(0)
    M = MOE_ROWS

    def gather(idx_ref, r, xb, sem):
        tok = idx_ref[0, 0, r]
        return pltpu.make_async_copy(h_hbm.at[pl.ds(pl.multiple_of(tok * SUBLANES, SUBLANES), SUBLANES), :],
                                     xb.at[pl.ds(r * SUBLANES, SUBLANES), :], sem)

    def scatter(idx_ref, r, yb, sem):
        row = idx_ref[0, 0, r]
        return pltpu.make_async_copy(yb.at[pl.ds(r * SUBLANES, SUBLANES), :],
                                     y_hbm.at[pl.ds(pl.multiple_of(row * SUBLANES, SUBLANES), SUBLANES), :], sem)

    def all_rows_in(xb, sem):
        return pltpu.make_async_copy(h_hbm.at[pl.ds(0, M * SUBLANES), :], xb, sem)

    def all_rows_out(yb, sem):
        return pltpu.make_async_copy(yb, y_hbm.at[pl.ds(0, M * SUBLANES), :], sem)

    @pl.when(i == 0)
    def _():
        ybuf0[...] = jnp.zeros(ybuf0.shape, F32)
        ybuf1[...] = jnp.zeros(ybuf1.shape, F32)

        def prime(r, carry):
            gather(src0_ref, r, xbuf0, xsem0).start()
            scatter(dstp_ref, r, ybuf0, ysem0).start()
            return carry
        lax.fori_loop(0, M, prime, 0)

    @pl.when(first_ref[i] == 1)
    def _():
        wgu_bf[...] = wgu_ref[...].astype(BF16)
        wd_bf[...] = wd_ref[...].astype(BF16)

    def step(xa, xsa, xo, xso, ya, ysa, yo, yso):
        here = valid_ref[i] == 1
        prev = valid_ref[jnp.maximum(i - 1, 0)] == 1

        @pl.when(prev)
        def _():
            all_rows_in(xa, xsa).wait()

        @pl.when(here)
        def _():
            for r in range(M):
                scatter(dst_ref, r, yo, yso).start(priority=1)
                gather(src_ref, r, xo, xso).start()
            gu = _dot(_from_token_tiles(xa, M).astype(BF16), wgu_bf[...]) + bgu_ref[...]
            gate = jnp.minimum(gu[:, :D_EXPERT], SWIGLU_LIMIT)
            up = jnp.clip(gu[:, D_EXPERT:], -SWIGLU_LIMIT, SWIGLU_LIMIT)
            act = gate * jax.nn.sigmoid(SWIGLU_ALPHA * gate) * (up + 1.0)
            y = _dot(act.astype(BF16), wd_bf[...]) + bd_ref[...]
            all_rows_out(ya, ysa).wait()
            _to_token_tiles(ya, y)

        @pl.when(jnp.logical_and(prev, jnp.logical_not(here)))
        def _():
            def move(r, carry):
                scatter(dst_ref, r, yo, yso).start(priority=1)
                return carry
            lax.fori_loop(0, M, move, 0)

    @pl.when(i % 2 == 0)
    def _():
        step(xbuf0, xsem0, xbuf1, xsem1, ybuf0, ysem0, ybuf1, ysem1)

    @pl.when(i % 2 == 1)
    def _():
        step(xbuf1, xsem1, xbuf0, xsem0, ybuf1, ysem1, ybuf0, ysem0)

    @pl.when(i == pl.num_programs(0) - 1)
    def _():
        all_rows_out(ybuf0, ysem0).wait()
        all_rows_out(ybuf1, ysem1).wait()


def _moe_experts(h2t, sched, w_gate_up, b_gate_up, w_down, b_down, layer, n_out_rows):
    block_e, first, valid, src, dst, dst_prime = sched
    n_blocks = block_e.shape[0]
    D = D_MODEL
    M = MOE_ROWS
    L, E = w_gate_up.shape[:2]
    wmap = lambda i, be, fi, va: (layer, be[i], 0, 0)
    smem = lambda imap: pl.BlockSpec((1, 1, M), imap, memory_space=pltpu.SMEM)
    return pl.pallas_call(
        _moe_kernel,
        grid_spec=pltpu.PrefetchScalarGridSpec(
            num_scalar_prefetch=3, grid=(n_blocks,),
            in_specs=[smem(lambda i, be, fi, va: (0, 0, 0)), smem(lambda i, be, fi, va: (i + 1, 0, 0)),
                      smem(lambda i, be, fi, va: (i, 0, 0)), smem(lambda i, be, fi, va: (0, 0, 0)),
                      pl.BlockSpec(memory_space=pl.ANY),
                      pl.BlockSpec((None, None, D, 2 * D_EXPERT), wmap),
                      pl.BlockSpec((None, None, 1, 2 * D_EXPERT), wmap),
                      pl.BlockSpec((None, None, D_EXPERT, D), wmap),
                      pl.BlockSpec((None, None, 1, D), wmap)],
            out_specs=pl.BlockSpec(memory_space=pl.ANY),
            scratch_shapes=[pltpu.VMEM((D, 2 * D_EXPERT), BF16), pltpu.VMEM((D_EXPERT, D), BF16),
                            ] + [pltpu.VMEM((M * SUBLANES, LANES), F32)] * 4 + [pltpu.SemaphoreType.DMA(())] * 4),
        out_shape=jax.ShapeDtypeStruct((n_out_rows * SUBLANES, LANES), F32),
        compiler_params=pltpu.CompilerParams(dimension_semantics=("arbitrary",), vmem_limit_bytes=VMEM_LIMIT),
        name="moe_experts",
    )(block_e, first, valid, src, src, dst, dst_prime, h2t, w_gate_up, b_gate_up.reshape(L, E, 1, 2 * D_EXPERT), w_down,
      b_down.reshape(L, E, 1, D))


def _route(top_idx, rank, counts, n_blocks):
    N = top_idx.shape[0]
    padded = (counts + MOE_ROWS - 1) // MOE_ROWS * MOE_ROWS
    pad_end = jnp.cumsum(padded)
    pad_start = pad_end - padded
    dest = pad_start[top_idx] + rank
    M, K = MOE_ROWS, TOP_K
    blk_start = jnp.arange(n_blocks, dtype=I32) * M
    valid = blk_start < pad_end[-1]
    block_e = jnp.minimum((blk_start[:, None] >= pad_end[None, :]).astype(I32).sum(axis=1), N_EXPERTS - 1)
    order = jnp.argsort(dest.reshape(-1)).astype(I32)
    first_pos = jnp.cumsum(counts) - counts
    slot_id = jnp.arange(n_blocks * M, dtype=I32)
    e_slot = jnp.repeat(block_e, M)
    offset = slot_id - pad_start[e_slot]
    real = jnp.repeat(valid, M) & (offset < counts[e_slot])
    slot_asg = jnp.where(real, order[jnp.clip(first_pos[e_slot] + offset, 0, N * K - 1)], -1)
    tok = jnp.where(real, slot_asg // K, 0)
    spare = K * N + ((slot_id // M) % 2) * M + slot_id % M
    out_row = jnp.where(real, (slot_asg % K) * N + tok, spare)
    src = jnp.concatenate([tok.reshape(n_blocks, 1, M), jnp.zeros((1, 1, M), I32)], axis=0)
    lead = (K * N + M + jnp.arange(M, dtype=I32)).reshape(1, 1, M)
    dst = jnp.concatenate([lead, out_row.reshape(n_blocks, 1, M)], axis=0)
    dst_prime = (K * N + jnp.arange(M, dtype=I32)).reshape(1, 1, M)
    n_valid = pad_end[-1] // M
    block_e = block_e[jnp.minimum(jnp.arange(n_blocks, dtype=I32), n_valid - 1)]
    prev_e = jnp.concatenate([jnp.full((1,), -1, I32), block_e[:-1]])
    first = (valid & (block_e != prev_e)).astype(I32)
    return (block_e, first, valid.astype(I32), src, dst, dst_prime)


def _final_kernel(modidx_ref, x_ref, *refs, n_ctx_tiles):
    del modidx_ref
    y_refs, (tg_ref, pmod_ref, g_ref, octx_ref, olat_ref) = refs[:TOP_K], refs[TOP_K:]
    x = x_ref[...] + pmod_ref[0, 5:6, :] * _combine_experts(y_refs, tg_ref)
    out = x * lax.rsqrt(jnp.mean(x * x, axis=-1, keepdims=True) + EPS) * g_ref[...]
    is_ctx = pl.program_id(0) < n_ctx_tiles

    @pl.when(is_ctx)
    def _():
        octx_ref[...] = out

    @pl.when(jnp.logical_not(is_ctx))
    def _():
        olat_ref[...] = out


def _final_norm(x, y, pmod, modidx, g_final, n_ctx_tiles):
    N, D = x.shape
    tm = TOKEN_TILE
    row = lambda i, mi: (i, 0)
    return pl.pallas_call(
        functools.partial(_final_kernel, n_ctx_tiles=n_ctx_tiles),
        grid_spec=pltpu.PrefetchScalarGridSpec(
            num_scalar_prefetch=1, grid=(N // tm,),
            in_specs=[pl.BlockSpec((tm, D), row)]
            + [pl.BlockSpec((tm * SUBLANES, LANES), functools.partial(lambda k, i, mi: (k * (N // tm) + i, 0), k))
               for k in range(TOP_K)]
            + [pl.BlockSpec((tm, LANES), row),
               pl.BlockSpec((1, 6, D), lambda i, mi: (mi[i], 0, 0)), pl.BlockSpec((1, D), lambda i, mi: (0, 0))],
            out_specs=[pl.BlockSpec((tm, D), lambda i, mi: (jnp.minimum(i, n_ctx_tiles - 1), 0)),
                       pl.BlockSpec((tm, D), lambda i, mi: (jnp.maximum(i - n_ctx_tiles, 0), 0))]),
        out_shape=[jax.ShapeDtypeStruct((n_ctx_tiles * tm, D), F32),
                   jax.ShapeDtypeStruct((N - n_ctx_tiles * tm, D), F32)],
        compiler_params=pltpu.CompilerParams(dimension_semantics=("arbitrary",), vmem_limit_bytes=VMEM_LIMIT),
        name="final_norm",
    )(modidx, x, *([y[0]] * TOP_K), y[1], pmod, g_final)


def _rope_tables(seq_len):
    pos = np.arange(seq_len)
    rowp = (pos // GRID_W).astype(np.float32)
    colp = (pos % GRID_W).astype(np.float32)

    def tables(rdim, copies):
        quarter = rdim // 4
        inv_freq = jnp.asarray(ROPE_THETA, F32) ** (-jnp.arange(quarter, dtype=F32) / quarter)
        ar = jnp.asarray(rowp)[:, None] * inv_freq[None, :]
        ac = jnp.asarray(colp)[:, None] * inv_freq[None, :]
        cos = jnp.concatenate([jnp.cos(ar), jnp.cos(ar), jnp.cos(ac), jnp.cos(ac)], axis=1)
        sin = jnp.concatenate([-jnp.sin(ar), jnp.sin(ar), -jnp.sin(ac), jnp.sin(ac)], axis=1)
        return jnp.tile(cos, (1, copies)), jnp.tile(sin, (1, copies))

    c64, s64 = tables(HEAD_DIM, 2)
    c32, s32 = tables(MLA_ROPE, 4)
    lat = jnp.concatenate([c64, s64, c32, s32], axis=1).reshape(seq_len // TOKEN_TILE, TOKEN_TILE, 512)
    ident = jnp.concatenate([jnp.ones((TOKEN_TILE, 128), F32), jnp.zeros((TOKEN_TILE, 128), F32)] * 2, axis=1)
    return jnp.concatenate([ident[None], lat], axis=0)


def _block_ones(n, blk):
    r = np.arange(n) // blk
    return jnp.asarray((r[:, None] == r[None, :]).astype(np.float32), BF16)


def kernel(x_prompt, x_sample, cache_mla_ckv, cache_mla_krope, cache_gqa_k, cache_gqa_v, state_gla, c, c_ctx, w_mod, b_mod, g_attn_norm, g_ffn_norm, w_in, g_q_a, w_uq, g_kv_a, w_ukv, g_q_head, g_k_head, w_gk_fwd, b_gk_fwd, w_gk_bwd, b_gk_bwd, g_gla_out, w_out, w_router, b_router, w_gate_up, b_gate_up, w_down, b_down, g_final):
    B, T, D = x_prompt.shape
    BD, TD, _ = x_sample.shape
    L = w_in.shape[0]
    P = cache_mla_ckv.shape[2]
    tm = TOKEN_TILE
    n_ctx = B * T
    N = n_ctx + BD * TD
    assert T == tm and all(TD % min(t, TD) == 0 for t in ATTN_TQ.values()) and n_ctx % TD == 0 and BD + 1 <= 16
    n_ctx_tiles = n_ctx // tm
    nt = N // tm
    tiles_per_lat = TD // tm

    tile_ids = np.arange(nt)
    lat_tile = np.maximum(tile_ids - n_ctx_tiles, 0)
    is_lat = tile_ids >= n_ctx_tiles
    modidx = jnp.asarray(np.where(is_lat, 1 + lat_tile // tiles_per_lat, 0), I32)
    tabidx = jnp.asarray(np.where(is_lat, 1 + lat_tile % tiles_per_lat, 0), I32)
    seq_of_tile = np.where(is_lat, B + lat_tile // tiles_per_lat, tile_ids)
    first_fwd = np.where(is_lat, lat_tile % tiles_per_lat == 0, True)
    last_fwd = np.where(is_lat, lat_tile % tiles_per_lat == tiles_per_lat - 1, True)
    order_fwd = (jnp.asarray(tile_ids, I32), jnp.asarray(seq_of_tile, I32), jnp.asarray(first_fwd, I32))
    rev = tile_ids[::-1]
    order_bwd = (jnp.asarray(rev, I32), jnp.asarray(seq_of_tile[rev], I32), jnp.asarray(last_fwd[rev], I32))
    n_seq = B + BD

    cvecs = jnp.zeros((16, D), F32).at[0].set(c_ctx).at[1:1 + BD].set(c)
    mods = _modulation(cvecs, w_mod, b_mod).reshape(L, 16, 6, D)

    o = np.cumsum([0, Q_LORA, KV_LORA, MLA_ROPE, 512, 128, 128, 128, 128, 256, 16, 16, 256])
    seg = lambda j: w_in[:, :, o[j]:o[j + 1]]
    w_in_r = jnp.concatenate(
        [seg(0), seg(1), seg(3), seg(4), seg(5), seg(6), seg(7), seg(8), seg(11), seg(2), seg(2), seg(2), seg(2),
         seg(9), seg(10), jnp.zeros((L, D, 128 - 2 * GLA_GATE_RANK), F32)], axis=-1).astype(BF16)
    uq = w_uq.reshape(L, Q_LORA, MLA_HEADS, MLA_QK)
    ukv = w_ukv.reshape(L, KV_LORA, MLA_HEADS, MLA_NOPE + MLA_V)
    w_comb = _fold_q_weights(uq[..., :MLA_NOPE].transpose(0, 2, 1, 3), ukv[..., :MLA_NOPE].transpose(0, 2, 1, 3))
    q_pad = jnp.zeros((L, Q_LORA, 256 - KV_LORA - MLA_ROPE), F32)
    w_q2 = (jnp.concatenate([part for h in range(MLA_HEADS)
                             for part in (w_comb[:, h], uq[:, :, h, MLA_NOPE:], q_pad)], axis=-1)
            * (MLA_QK ** -0.5 * LOG2E)).astype(BF16)
    w_uv = ukv[..., MLA_NOPE:].transpose(0, 2, 1, 3)
    eye_h = jnp.eye(MLA_HEADS, dtype=F32)
    wuv_pad = jnp.einsum('lhkv,hg->lhkgv', w_uv, eye_h).reshape(L, MLA_HEADS, KV_LORA, MLA_HEADS * MLA_V).astype(BF16)
    w_gate = jnp.zeros((L, 128, 256), F32).at[:, 0:16, 0:128].set(w_gk_fwd).at[:, 16:32, 128:256].set(w_gk_bwd)
    w_gate = w_gate.astype(BF16)
    b_gate = jnp.concatenate([b_gk_fwd, b_gk_bwd], axis=-1)
    w_out_bf = w_out.astype(BF16)
    wr_pad = jnp.pad(w_router, ((0, 0), (0, 0), (0, LANES - N_EXPERTS)))
    wr_hi = wr_pad.astype(BF16)
    wr_lo = (wr_pad - wr_hi.astype(F32)).astype(BF16)
    br_pad = jnp.pad(b_router, ((0, 0), (0, LANES - N_EXPERTS)), constant_values=NEG_BIG)
    ones512 = _block_ones(512, HEAD_DIM)
    tri = jnp.asarray((np.arange(tm)[None, :] < np.arange(tm)[:, None]).astype(np.float32), BF16)
    head_expand = jnp.asarray((np.arange(128)[:, None] // GLA_DK == np.arange(256)[None, :] // GLA_DV)
                              .astype(np.float32), BF16)
    tab = _rope_tables(TD)

    x = jnp.concatenate([x_prompt.reshape(n_ctx, D), x_sample.reshape(BD * TD, D)], axis=0)
    y = None
    n_blocks = (N * TOP_K) // MOE_ROWS + N_EXPERTS
    assert (N * TOP_K) % MOE_ROWS == 0
    caches, states = [], []
    for l in range(L):
        lw = dict(g_attn=g_attn_norm[l][None], w_in_r=w_in_r[l], g_q_a=g_q_a[l][None], w_q2=w_q2[l],
                  g_kv_a=g_kv_a[l][None], g_qh=jnp.tile(g_q_head[l], GQA_HEADS)[None],
                  g_kh=jnp.tile(g_k_head[l], GQA_KV_HEADS)[None], ones512=ones512, w_gate=w_gate[l],
                  b_gate=b_gate[l][None], head_expand=head_expand, g_gla_out=jnp.tile(g_gla_out[l], GLA_HEADS)[None],
                  ones256=ones512[:256, :256], w_out=w_out_bf[l], g_ffn=g_ffn_norm[l][None], tri=tri,
                  w_router_hi=wr_hi[l], w_router_lo=wr_lo[l], b_router=br_pad[l][None])
        pmod = mods[l - 1] if l > 0 else None
        x, qm, kvc, qg, kdup, vdup, gin, cache = _pre_attn(x, y, pmod, mods[l], tab, modidx, tabidx, lw, n_ctx_tiles)
        caches.append(cache[:n_ctx])

        mla_ctx = jnp.concatenate([cache_mla_ckv[:, l]] + [cache_mla_krope[:, l]] * 4, axis=-1).astype(BF16)
        ck = cache_gqa_k[:, l]
        cv = cache_gqa_v[:, l]
        k_ctx = jnp.concatenate([ck[:, :, 0], ck[:, :, 0], ck[:, :, 1], ck[:, :, 1]], axis=-1).astype(BF16)
        ones_v = jnp.ones(cv.shape[:2] + (HEAD_DIM,), F32)
        v_ctx = jnp.concatenate([cv[:, :, 0], ones_v, cv[:, :, 1], ones_v], axis=-1).astype(BF16)
        o_mla = (_attention('mla', qm, kvc, None, None, wuv_pad[l], n_seq=B, seq_len=T, tile_offset=0),
                 _attention('mla', qm, kvc, None, (mla_ctx,), wuv_pad[l], n_seq=BD, seq_len=TD,
                            tile_offset=n_ctx // TD))
        o_gqa = (_attention('gqa', qg, kdup, vdup, None, None, n_seq=B, seq_len=T, tile_offset=0),
                 _attention('gqa', qg, kdup, vdup, (k_ctx, v_ctx), None, n_seq=BD, seq_len=TD,
                            tile_offset=n_ctx // TD))

        eye_g = jnp.eye(GLA_HEADS, dtype=F32)
        st_lat = jnp.einsum('bshde,hg->bshegd', state_gla[:, l], eye_g).reshape(BD, 2, 256, 128)
        zeros_ctx = jnp.zeros((B, 256, 128), F32)
        o_f, s_f = _gla(gin, jnp.concatenate([zeros_ctx, st_lat[:, 0]], axis=0), None, lw, order_fwd, reverse=False)
        o_gla, s_b = _gla(gin, jnp.concatenate([zeros_ctx, st_lat[:, 1]], axis=0), o_f, lw, order_bwd, reverse=True)
        states.append((s_f[:B], s_b[:B]))

        x1, h2, ti, tg, cnt = _post_attn(x, o_mla, o_gqa, o_gla, mods[l], modidx, lw, n_ctx_tiles)
        sched = _route(ti[:, :TOP_K], ti[:, TOP_K:2 * TOP_K], cnt[0, :N_EXPERTS].astype(I32), n_blocks)
        y = (_moe_experts(h2, sched, w_gate_up, b_gate_up, w_down, b_down, l, TOP_K * N + 2 * MOE_ROWS), tg)
        x = x1

    out_ctx, out_lat = _final_norm(x, y, mods[L - 1], modidx, g_final[None], n_ctx_tiles)
    y_prompt = out_ctx.reshape(B, T, D)
    y_sample = out_lat.reshape(BD, TD, D)
    cache_all = jnp.stack([cc.reshape(B, T, 512) for cc in caches], axis=1)
    new_ckv = cache_all[..., 0:128]
    new_k = cache_all[..., 128:256].reshape(B, L, T, GQA_KV_HEADS, HEAD_DIM)
    new_v = cache_all[..., 256:384].reshape(B, L, T, GQA_KV_HEADS, HEAD_DIM)
    new_krope = cache_all[..., 384:384 + MLA_ROPE]

    def unpack_state(st):
        s5 = st.reshape(B, GLA_HEADS, GLA_DV, GLA_HEADS, GLA_DK)
        diag = jnp.stack([s5[:, h, :, h, :] for h in range(GLA_HEADS)], axis=1)
        return diag.transpose(0, 1, 3, 2)

    new_state = jnp.stack([jnp.stack([unpack_state(sf), unpack_state(sb)], axis=1) for sf, sb in states], axis=1)
    return (y_prompt, y_sample, new_ckv, new_krope, new_k, new_v, new_state)
```

```python
import functools

import jax
import jax.numpy as jnp
import numpy as np
from jax import lax
from jax.experimental import pallas as pl
from jax.experimental.pallas import tpu as pltpu

F32 = jnp.float32
BF16 = jnp.bfloat16
I32 = jnp.int32

D_MODEL = 1024
GRID_W = 64
ROPE_THETA = 10000.0
EPS = 1e-6
MLA_HEADS = 4
Q_LORA = 256
KV_LORA = 128
MLA_NOPE = 64
MLA_ROPE = 32
MLA_V = 64
MLA_QK = MLA_NOPE + MLA_ROPE
GQA_HEADS = 8
GQA_KV_HEADS = 2
GQA_GROUP = GQA_HEADS // GQA_KV_HEADS
HEAD_DIM = 64
GLA_HEADS = 4
GLA_DK = 32
GLA_DV = 64
GLA_GATE_RANK = 16
GLA_GATE_NORM = 16.0
N_EXPERTS = 32
TOP_K = 4
D_EXPERT = D_MODEL
SWIGLU_LIMIT = 7.0
SWIGLU_ALPHA = 1.702

LANES = 128
SUBLANES = 8
TOKEN_TILE = 256
GLA_STEP = 16
MOE_ROWS = 512
ATTN_TQ = {'mla': 1024, 'gqa': 1024}
ATTN_TK = 512
VMEM_LIMIT = 56 * 1024 * 1024
NEG_BIG = -1e30
LOG2E = 1.4426950408889634

C_CQ, C_CKV, C_GQ, C_GK, C_GV = 0, 256, 384, 896, 1024
C_LQ, C_LK, C_LV, C_LG, C_KR, C_GATE = 1152, 1280, 1408, 1664, 1920, 2048
IN_COLS = 2176


def _dot(a, b):
    return jnp.dot(a, b, preferred_element_type=F32)


def _dot_nt(a, b):
    return lax.dot_general(a, b, (((1,), (1,)), ((), ())), preferred_element_type=F32)


def _dot_tn(a, b):
    return lax.dot_general(a, b, (((0,), (0,)), ((), ())), preferred_element_type=F32)


def _split_dot(x, ones_bf):
    hi = x.astype(BF16)
    lo = (x - hi.astype(F32)).astype(BF16)
    return _dot(hi, ones_bf) + _dot(lo, ones_bf)


def _rope(x, cos, sin_signed, quarter):
    width = x.shape[-1]
    lane = lax.broadcasted_iota(I32, x.shape, 1)
    first = (lane % (2 * quarter)) < quarter
    partner = jnp.where(first, pltpu.roll(x, width - quarter, 1), pltpu.roll(x, quarter, 1))
    return x * cos + partner * sin_signed


def _mod_kernel(c_ref, w_ref, b_ref, o_ref):
    c = c_ref[...]
    a = c * jax.nn.sigmoid(c)
    o_ref[0] = jnp.dot(a, w_ref[0], precision=lax.Precision.HIGHEST, preferred_element_type=F32) + b_ref[0]


def _modulation(cvecs, w_mod, b_mod):
    L, D, D6 = w_mod.shape
    tn = 1536
    return pl.pallas_call(
        _mod_kernel,
        grid=(L, D6 // tn),
        in_specs=[pl.BlockSpec((16, D), lambda l, j: (0, 0)),
                  pl.BlockSpec((1, D, tn), lambda l, j: (l, 0, j)),
                  pl.BlockSpec((1, 1, tn), lambda l, j: (l, 0, j))],
        out_specs=pl.BlockSpec((1, 16, tn), lambda l, j: (l, 0, j)),
        out_shape=jax.ShapeDtypeStruct((L, 16, D6), F32),
        compiler_params=pltpu.CompilerParams(dimension_semantics=("arbitrary", "arbitrary"),
                                             vmem_limit_bytes=VMEM_LIMIT),
        name="modulation",
    )(cvecs, w_mod, b_mod.reshape(L, 1, D6))


def _fold_kernel(a_ref, b_ref, o_ref):
    o_ref[0, 0] = lax.dot_general(a_ref[0, 0], b_ref[0, 0], (((1,), (1,)), ((), ())),
                                  precision=lax.Precision.HIGHEST, preferred_element_type=F32)


def _fold_q_weights(wq_nope, wk_nope):
    L, H, A, K = wq_nope.shape
    B = wk_nope.shape[2]
    return pl.pallas_call(
        _fold_kernel,
        grid=(L, H),
        in_specs=[pl.BlockSpec((1, 1, A, K), lambda l, h: (l, h, 0, 0)),
                  pl.BlockSpec((1, 1, B, K), lambda l, h: (l, h, 0, 0))],
        out_specs=pl.BlockSpec((1, 1, A, B), lambda l, h: (l, h, 0, 0)),
        out_shape=jax.ShapeDtypeStruct((L, H, A, B), F32),
        name="fold_q_weights",
    )(wq_nope, wk_nope)


def _from_token_tiles(ref, rows):
    return jnp.concatenate([ref[pl.ds(c, rows, stride=SUBLANES), :] for c in range(D_MODEL // LANES)], axis=1)


def _to_token_tiles(ref, x):
    for c in range(D_MODEL // LANES):
        ref[pl.ds(c, x.shape[0], stride=SUBLANES), :] = x[:, c * LANES:(c + 1) * LANES]


def _combine_experts(y_refs, tg_ref):
    tg = tg_ref[...]
    acc = None
    for k in range(TOP_K):
        term = tg[:, k:k + 1] * _from_token_tiles(y_refs[k], tg.shape[0])
        acc = term if acc is None else acc + term
    return acc


def _pre_attn_kernel(modidx_ref, tabidx_ref, *refs, has_y):
    del modidx_ref, tabidx_ref
    if has_y:
        x_ref, y_ref, tg_ref, pmod_ref = refs[0], refs[1:1 + TOP_K], refs[1 + TOP_K], refs[2 + TOP_K]
        refs = refs[3 + TOP_K:]
    else:
        x_ref = refs[0]
        refs = refs[1:]
    (mod_ref, tab_ref, gattn_ref, win_ref, gqa_ref, wq2_ref, gkva_ref, gqh_ref, gkh_ref, ones_ref,
     wg_ref, bg_ref) = refs[:12]
    outs = refs[12:]
    if has_y:
        xres_ref = outs[0]
        outs = outs[1:]
    qm_ref, kvc_ref, qg_ref, kdup_ref, vdup_ref, gla_ref, cache_ref = outs

    x = x_ref[...]
    if has_y:
        x = x + pmod_ref[0, 5:6, :] * _combine_experts(y_ref, tg_ref)
        xres_ref[...] = x
    mod = mod_ref[0]
    shift1, scale1 = mod[0:1, :], mod[1:2, :]
    xn = x * lax.rsqrt(jnp.mean(x * x, axis=-1, keepdims=True) + EPS) * gattn_ref[...]
    h = xn * (1.0 + scale1) + shift1
    p = _dot(h.astype(BF16), win_ref[...])

    tab = tab_ref[0]
    cos64, sin64 = tab[:, 0:128], tab[:, 128:256]
    cos32, sin32 = tab[:, 256:384], tab[:, 384:512]
    ones = ones_ref[...]

    cq = p[:, C_CQ:C_CQ + Q_LORA]
    cqn = cq * lax.rsqrt(jnp.mean(cq * cq, axis=-1, keepdims=True) + EPS) * gqa_ref[...]
    qm = _dot(cqn.astype(BF16), wq2_ref[...])
    q_rope = _rope(jnp.concatenate([qm[:, h * 256 + 128:(h + 1) * 256] for h in range(MLA_HEADS)], axis=1),
                   jnp.concatenate([cos32] * MLA_HEADS, axis=1), jnp.concatenate([sin32] * MLA_HEADS, axis=1),
                   MLA_ROPE // 4)
    for h in range(MLA_HEADS):
        qm_ref[:, h * 256:h * 256 + 128] = qm[:, h * 256:h * 256 + 128].astype(BF16)
        qm_ref[:, h * 256 + 128:(h + 1) * 256] = q_rope[:, h * 128:(h + 1) * 128].astype(BF16)
    ckv_raw = p[:, C_CKV:C_CKV + KV_LORA]
    ckv = ckv_raw * lax.rsqrt(jnp.mean(ckv_raw * ckv_raw, axis=-1, keepdims=True) + EPS) * gkva_ref[...]
    kr4 = p[:, C_KR:C_KR + 128]
    kr4_rot = _rope(kr4, cos32, sin32, MLA_ROPE // 4)
    kvc_ref[:, 0:128] = ckv.astype(BF16)
    kvc_ref[:, 128:256] = kr4_rot.astype(BF16)

    gq = p[:, C_GQ:C_GQ + 512]
    ms_q = _split_dot(gq * gq, ones) * (1.0 / HEAD_DIM)
    qn = gq * lax.rsqrt(ms_q + EPS) * gqh_ref[...]
    cos_q = jnp.concatenate([cos64] * 4, axis=1)
    sin_q = jnp.concatenate([sin64] * 4, axis=1)
    qg = _rope(qn, cos_q, sin_q, HEAD_DIM // 4) * (HEAD_DIM ** -0.5 * LOG2E)
    qg_ref[...] = qg.astype(BF16)
    gk = p[:, C_GK:C_GK + 128]
    ms_k = _split_dot(gk * gk, ones[0:128, 0:128]) * (1.0 / HEAD_DIM)
    kn = gk * lax.rsqrt(ms_k + EPS) * gkh_ref[...]
    kg = _rope(kn, cos64, sin64, HEAD_DIM // 4)
    gv = p[:, C_GV:C_GV + 128]
    lane = lax.broadcasted_iota(I32, kg.shape, 1)
    low = lane < HEAD_DIM
    kg_sw = pltpu.roll(kg, HEAD_DIM, 1)
    gv_sw = pltpu.roll(gv, HEAD_DIM, 1)
    kdup_ref[:, 0:128] = jnp.where(low, kg, kg_sw).astype(BF16)
    kdup_ref[:, 128:256] = jnp.where(low, kg_sw, kg).astype(BF16)
    vdup_ref[:, 0:128] = jnp.where(low, gv, 1.0).astype(BF16)
    vdup_ref[:, 128:256] = jnp.where(low, gv_sw, 1.0).astype(BF16)

    z = _dot(p[:, C_GATE:C_GATE + 128].astype(BF16), wg_ref[...]) + bg_ref[...]
    logsig = jnp.minimum(z, 0.0) - jnp.log1p(jnp.exp(-jnp.abs(z)))
    gla_ref[:, 0:128] = p[:, C_LQ:C_LQ + 128] * (GLA_DK ** -0.5)
    gla_ref[:, 128:256] = p[:, C_LK:C_LK + 128]
    gla_ref[:, 256:512] = logsig * (1.0 / GLA_GATE_NORM)
    gla_ref[:, 512:768] = p[:, C_LV:C_LV + 256]
    gla_ref[:, 768:1024] = p[:, C_LG:C_LG + 256]

    cache_ref[:, 0:128] = ckv
    cache_ref[:, 128:256] = kn
    cache_ref[:, 256:384] = gv
    cache_ref[:, 384:512] = kr4


def _pre_attn(x, y, pmod, mod, tab, modidx, tabidx, lw, n_ctx_tiles):
    N, D = x.shape
    tm = TOKEN_TILE
    nt = N // tm
    has_y = y is not None
    row = lambda i, mi, ti: (i, 0)
    const2 = lambda i, mi, ti: (0, 0)
    in_specs = [pl.BlockSpec((tm, D), row)]
    args = [x]
    if has_y:
        yk, gates = y
        in_specs += [pl.BlockSpec((tm * SUBLANES, LANES), functools.partial(lambda k, i, mi, ti: (k * nt + i, 0), k))
                     for k in range(TOP_K)]
        in_specs += [pl.BlockSpec((tm, LANES), row), pl.BlockSpec((1, 6, D), lambda i, mi, ti: (mi[i], 0, 0))]
        args += [yk] * TOP_K + [gates, pmod]
    in_specs += [
        pl.BlockSpec((1, 6, D), lambda i, mi, ti: (mi[i], 0, 0)),
        pl.BlockSpec((1, tm, 512), lambda i, mi, ti: (ti[i], 0, 0)),
        pl.BlockSpec((1, D), const2),
        pl.BlockSpec((D, IN_COLS), const2),
        pl.BlockSpec((1, Q_LORA), const2),
        pl.BlockSpec((Q_LORA, 1024), const2),
        pl.BlockSpec((1, KV_LORA), const2),
        pl.BlockSpec((1, 512), const2),
        pl.BlockSpec((1, 128), const2),
        pl.BlockSpec((512, 512), const2),
        pl.BlockSpec((128, 256), const2),
        pl.BlockSpec((1, 256), const2),
    ]
    args += [mod, tab, lw['g_attn'], lw['w_in_r'], lw['g_q_a'], lw['w_q2'], lw['g_kv_a'], lw['g_qh'], lw['g_kh'],
             lw['ones512'], lw['w_gate'], lw['b_gate']]
    out_shape, out_specs = [], []
    if has_y:
        out_shape.append(jax.ShapeDtypeStruct((N, D), F32))
        out_specs.append(pl.BlockSpec((tm, D), row))
    for width, dt in ((1024, BF16), (256, BF16), (512, BF16), (256, BF16), (256, BF16), (1024, F32)):
        out_shape.append(jax.ShapeDtypeStruct((N, width), dt))
        out_specs.append(pl.BlockSpec((tm, width), row))
    out_shape.append(jax.ShapeDtypeStruct(((n_ctx_tiles + 1) * tm, 512), F32))
    out_specs.append(pl.BlockSpec((tm, 512), lambda i, mi, ti: (jnp.minimum(i, n_ctx_tiles), 0)))
    outs = pl.pallas_call(
        functools.partial(_pre_attn_kernel, has_y=has_y),
        grid_spec=pltpu.PrefetchScalarGridSpec(num_scalar_prefetch=2, grid=(nt,), in_specs=in_specs,
                                               out_specs=out_specs),
        out_shape=out_shape,
        compiler_params=pltpu.CompilerParams(dimension_semantics=("arbitrary",), vmem_limit_bytes=VMEM_LIMIT),
        name="pre_attn",
    )(modidx, tabidx, *args)
    if not has_y:
        outs = [x] + list(outs)
    return outs


def _softmax_step(qs, k, m_ref, c):
    s = _dot_nt(qs, k)
    m_old = m_ref[c]
    m_new = jnp.maximum(m_old, jnp.broadcast_to(jnp.max(s, axis=-1, keepdims=True), m_old.shape))
    alpha = jnp.exp2(m_old - m_new)
    p = jnp.exp2(s - jnp.concatenate([m_new] * (k.shape[0] // LANES), axis=1))
    m_ref[c] = m_new
    return alpha, p


def _mla_attn_kernel(*refs, has_ctx, n_chunks, tk):
    if has_ctx:
        q_ref, kc_ref, k_ref, wuv_ref, o_ref, m_ref, l_ref, acc_ref = refs
    else:
        q_ref, k_ref, wuv_ref, o_ref, m_ref, l_ref, acc_ref = refs
        kc_ref = None
    tq = q_ref.shape[0]
    m_ref[...] = jnp.full(m_ref.shape, NEG_BIG, F32)
    l_ref[...] = jnp.zeros(l_ref.shape, F32)
    acc_ref[...] = jnp.zeros(acc_ref.shape, F32)
    q = q_ref[...]
    qs = [jnp.concatenate([q[:, h * 256:(h + 1) * 256] for h in (2 * c, 2 * c + 1)], axis=0) for c in range(2)]

    def process(kblk):
        for c in range(2):
            alpha, p = _softmax_step(qs[c], kblk, m_ref, c)
            l_ref[c] = alpha * l_ref[c] + jnp.broadcast_to(jnp.sum(p, axis=-1, keepdims=True), alpha.shape)
            acc_ref[c] = alpha * acc_ref[c] + _dot(p.astype(BF16), kblk[:, 0:KV_LORA])

    if has_ctx:
        process(kc_ref[0])

    def body(j, carry):
        process(k_ref[pl.ds(pl.multiple_of(j * tk, tk), tk), :])
        return carry

    lax.fori_loop(0, n_chunks, body, 0)
    out = jnp.zeros(o_ref.shape, F32)
    for h in range(MLA_HEADS):
        rows = slice((h % 2) * tq, (h % 2 + 1) * tq)
        o_lat = acc_ref[h // 2, rows, :] / l_ref[h // 2, rows, :]
        out = out + _dot(o_lat.astype(BF16), wuv_ref[h])
    o_ref[...] = out.astype(o_ref.dtype)


def _gqa_attn_kernel(*refs, has_ctx, n_chunks, tk):
    if has_ctx:
        q_ref, kc_ref, vc_ref, k_ref, v_ref, o_ref, m_ref, acc_ref = refs
    else:
        q_ref, k_ref, v_ref, o_ref, m_ref, acc_ref = refs
        kc_ref = vc_ref = None
    tq = q_ref.shape[0]
    m_ref[...] = jnp.full(m_ref.shape, NEG_BIG, F32)
    acc_ref[...] = jnp.zeros(acc_ref.shape, F32)
    q = q_ref[...]
    lane = lax.broadcasted_iota(I32, (tq, LANES), 1)
    low = lane < HEAD_DIM
    qs = []
    for g in range(GQA_KV_HEADS):
        parts = []
        for h in range(g * GQA_GROUP, (g + 1) * GQA_GROUP):
            pair = q[:, (h // 2) * LANES:(h // 2 + 1) * LANES]
            keep = low if h % 2 == 0 else jnp.logical_not(low)
            parts.append(jnp.where(keep, pair, jnp.zeros_like(pair)))
        qs.append(jnp.concatenate(parts, axis=0))

    def process(kblk, vblk):
        for g in range(GQA_KV_HEADS):
            alpha, p = _softmax_step(qs[g], kblk[:, g * LANES:(g + 1) * LANES], m_ref, g)
            acc_ref[g] = alpha * acc_ref[g] + _dot(p.astype(BF16), vblk[:, g * LANES:(g + 1) * LANES])

    if has_ctx:
        process(kc_ref[0], vc_ref[0])

    def body(j, carry):
        sl = pl.ds(pl.multiple_of(j * tk, tk), tk)
        process(k_ref[sl, :], v_ref[sl, :])
        return carry

    lax.fori_loop(0, n_chunks, body, 0)
    for j in range(GQA_HEADS // 2):
        outs = []
        for h in (2 * j, 2 * j + 1):
            a = acc_ref[h // GQA_GROUP, (h % GQA_GROUP) * tq:(h % GQA_GROUP + 1) * tq, :]
            outs.append(a / jnp.where(low, pltpu.roll(a, HEAD_DIM, 1), a))
        o_ref[:, j * LANES:(j + 1) * LANES] = jnp.where(low, outs[0], pltpu.roll(outs[1], HEAD_DIM, 1)).astype(
            o_ref.dtype)


def _attention(kind, q, k, v, ctx, wuv, *, n_seq, seq_len, tile_offset):
    has_ctx = ctx is not None
    tq = min(ATTN_TQ[kind], seq_len)
    tk = min(ATTN_TK, seq_len)
    nq = seq_len // tq
    qw = q.shape[1]
    ow = 256 if kind == 'mla' else 512
    heads = MLA_HEADS if kind == 'mla' else GQA_HEADS
    qmap = lambda b, i: (tile_offset * nq + b * nq + i, 0)
    kmap = lambda b, i: (tile_offset + b, 0)
    cmap = lambda b, i: (b, 0, 0)
    in_specs = [pl.BlockSpec((tq, qw), qmap)]
    args = [q]
    if kind == 'mla':
        if has_ctx:
            in_specs.append(pl.BlockSpec((1,) + ctx[0].shape[1:], cmap))
            args.append(ctx[0])
        in_specs += [pl.BlockSpec((seq_len, 256), kmap), pl.BlockSpec((MLA_HEADS, 128, 256), lambda b, i: (0, 0, 0))]
        args += [k, wuv]
        body = _mla_attn_kernel
    else:
        if has_ctx:
            in_specs += [pl.BlockSpec((1,) + ctx[0].shape[1:], cmap), pl.BlockSpec((1,) + ctx[1].shape[1:], cmap)]
            args += [ctx[0], ctx[1]]
        in_specs += [pl.BlockSpec((seq_len, 256), kmap), pl.BlockSpec((seq_len, 256), kmap)]
        args += [k, v]
        body = _gqa_attn_kernel
    return pl.pallas_call(
        functools.partial(body, has_ctx=has_ctx, n_chunks=seq_len // tk, tk=tk),
        grid=(n_seq, nq),
        in_specs=in_specs,
        out_specs=pl.BlockSpec((tq, ow), lambda b, i: (b * nq + i, 0)),
        out_shape=jax.ShapeDtypeStruct((n_seq * seq_len, ow), BF16),
        scratch_shapes=[pltpu.VMEM((2, heads // 2 * tq, LANES), F32)] * (3 if kind == 'mla' else 2),
        compiler_params=pltpu.CompilerParams(dimension_semantics=("arbitrary", "arbitrary"),
                                             vmem_limit_bytes=VMEM_LIMIT),
        name=kind + ("_attn_latent" if has_ctx else "_attn_context"),
    )(*args)


def _gla_kernel(blk_ref, seq_ref, first_ref, *refs, reverse):
    del blk_ref, seq_ref
    if reverse:
        gin_ref, s0_ref, he_ref, of_ref, gout_ref, ones_ref, o_ref, sout_ref, st_ref = refs
    else:
        gin_ref, s0_ref, he_ref, o_ref, sout_ref, st_ref = refs
    i = pl.program_id(0)

    @pl.when(first_ref[i] == 1)
    def _():
        st_ref[...] = s0_ref[0]

    R, S = TOKEN_TILE, GLA_STEP
    ns = R // S
    q = gin_ref[:, 0:128]
    k = gin_ref[:, 128:256]
    g = gin_ref[:, 384:512] if reverse else gin_ref[:, 256:384]
    v = gin_ref[:, 512:768]
    pos = lax.broadcasted_iota(I32, (R, 128), 0) % S
    b = g
    for s in (1, 2, 4, 8):
        if reverse:
            b = b + jnp.where(pos < S - s, pltpu.roll(b, R - s, 0), 0.0)
        else:
            b = b + jnp.where(pos >= s, pltpu.roll(b, s, 0), 0.0)

    def step_row(a, j):
        w = a.shape[-1]
        a3 = a.reshape(ns, S, w)
        return jnp.broadcast_to(a3[:, j:j + 1, :], (ns, S, w)).reshape(R, w)

    b_edge = step_row(b, 0 if reverse else S - 1)
    qt = (q * jnp.exp(b)).astype(BF16)
    kt = (k * jnp.exp(b_edge - b)).astype(BF16)
    d_edge = jnp.exp(b_edge)
    vb = v.astype(BF16)
    he = he_ref[...]

    o_intra = jnp.zeros((R, 256), F32)
    for j in range(S):
        cond = (pos <= j) if reverse else (pos >= j)
        bj, kj, vj = step_row(b, j), step_row(k, j), step_row(v, j)
        t = q * kj * jnp.exp(jnp.where(cond, b - bj, NEG_BIG))
        o_intra = o_intra + _dot(t.astype(BF16), he) * vj

    own_head = (lax.broadcasted_iota(I32, (256, 128), 0) // GLA_DV) == (lax.broadcasted_iota(I32, (256, 128), 1) // GLA_DK)
    st = st_ref[...]
    parts = [None] * ns
    for t in (range(ns - 1, -1, -1) if reverse else range(ns)):
        rows = slice(t * S, (t + 1) * S)
        parts[t] = _dot_nt(qt[rows], st.astype(BF16)) + o_intra[rows]
        ut = _dot_tn(vb[rows], kt[rows])
        st = d_edge[t * S:t * S + 1, :] * st + jnp.where(own_head, ut, 0.0)
    st_ref[...] = st
    sout_ref[0] = st
    o = jnp.concatenate(parts, axis=0)
    if reverse:
        o = o + of_ref[...]
        ms = _split_dot(o * o, ones_ref[...]) * (1.0 / GLA_DV)
        on = o * lax.rsqrt(ms + EPS) * gout_ref[...]
        lg = gin_ref[:, 768:1024]
        o_ref[...] = (on * (lg * jax.nn.sigmoid(lg))).astype(o_ref.dtype)
    else:
        o_ref[...] = o


def _gla(gin, s0, o_f, lw, order, *, reverse):
    blk, seq, first = order
    N = gin.shape[0]
    tm = TOKEN_TILE
    n_seq = s0.shape[0]
    tile = lambda i, b, s, f: (b[i], 0)
    state = lambda i, b, s, f: (s[i], 0, 0)
    const2 = lambda i, b, s, f: (0, 0)
    in_specs = [pl.BlockSpec((tm, 1024), tile), pl.BlockSpec((1, 256, 128), state), pl.BlockSpec((128, 256), const2)]
    args = [gin, s0, lw['head_expand']]
    if reverse:
        in_specs += [pl.BlockSpec((tm, 256), tile), pl.BlockSpec((1, 256), const2), pl.BlockSpec((256, 256), const2)]
        args += [o_f, lw['g_gla_out'], lw['ones256']]
    return pl.pallas_call(
        functools.partial(_gla_kernel, reverse=reverse),
        grid_spec=pltpu.PrefetchScalarGridSpec(
            num_scalar_prefetch=3, grid=(N // tm,), in_specs=in_specs,
            out_specs=[pl.BlockSpec((tm, 256), tile), pl.BlockSpec((1, 256, 128), state)],
            scratch_shapes=[pltpu.VMEM((256, 128), F32)]),
        out_shape=[jax.ShapeDtypeStruct((N, 256), BF16 if reverse else F32),
                   jax.ShapeDtypeStruct((n_seq, 256, 128), F32)],
        compiler_params=pltpu.CompilerParams(dimension_semantics=("arbitrary",), vmem_limit_bytes=VMEM_LIMIT),
        name="gla_bwd" if reverse else "gla_fwd",
    )(blk, seq, first, *args)


def _post_attn_kernel(modidx_ref, x_ref, omc_ref, oml_ref, ogc_ref, ogl_ref, ol_ref, mod_ref, wout_ref, gffn_ref,
                      wrh_ref, wrl_ref, br_ref, tri_ref, x1_ref, h2_ref, ti_ref, tg_ref, cnt_ref, carry_ref, *,
                      n_ctx_tiles):
    del modidx_ref
    mod = mod_ref[0]
    gate1, shift2, scale2 = mod[2:3, :], mod[3:4, :], mod[4:5, :]
    is_ctx = pl.program_id(0) < n_ctx_tiles
    om = jnp.where(is_ctx, omc_ref[...], oml_ref[...])
    og = jnp.where(is_ctx, ogc_ref[...], ogl_ref[...])
    o = _dot(om, wout_ref[0:256, :]) + _dot(og, wout_ref[256:768, :]) + _dot(ol_ref[...], wout_ref[768:1024, :])
    x1 = x_ref[...] + gate1 * o
    x1_ref[...] = x1
    xn = x1 * lax.rsqrt(jnp.mean(x1 * x1, axis=-1, keepdims=True) + EPS) * gffn_ref[...]
    h2 = xn * (1.0 + scale2) + shift2
    _to_token_tiles(h2_ref, h2)
    hi = h2.astype(BF16)
    lo = (h2 - hi.astype(F32)).astype(BF16)
    wrh = wrh_ref[...]
    logits = _dot(hi, wrh) + _dot(hi, wrl_ref[...]) + _dot(lo, wrh) + br_ref[...]

    lane = lax.broadcasted_iota(I32, logits.shape, 1).astype(F32)
    work = logits
    vals, idxs = [], []
    for _ in range(TOP_K):
        m = jnp.max(work, axis=-1, keepdims=True)
        idx = jnp.min(jnp.where(work == m, lane, float(LANES)), axis=-1, keepdims=True)
        vals.append(m)
        idxs.append(idx)
        work = jnp.where(lane == idx, NEG_BIG * 4.0, work)
    es = [jnp.exp(vv - vals[0]) for vv in vals]
    denom = es[0] + es[1] + es[2] + es[3]
    ti = jnp.zeros(logits.shape, F32)
    tg = jnp.zeros(logits.shape, F32)
    for r in range(TOP_K):
        ti = jnp.where(lane == float(r), idxs[r], ti)
        tg = jnp.where(lane == float(r), es[r] / denom, tg)
    tg_ref[...] = tg

    @pl.when(pl.program_id(0) == 0)
    def _():
        carry_ref[...] = jnp.zeros(carry_ref.shape, F32)

    picks = [lane == idxs[r] for r in range(TOP_K)]
    onehot = sum(jnp.where(pk, 1.0, 0.0) for pk in picks)
    carry = carry_ref[...]
    before = carry + _dot(tri_ref[...], onehot.astype(BF16))
    for r in range(TOP_K):
        rank = jnp.sum(jnp.where(picks[r], before, 0.0), axis=-1, keepdims=True)
        ti = jnp.where(lane == float(TOP_K + r), rank, ti)
    ti_ref[...] = ti.astype(I32)
    carry = carry + jnp.sum(onehot, axis=0, keepdims=True)
    carry_ref[...] = carry
    cnt_ref[...] = jnp.broadcast_to(carry, cnt_ref.shape)


def _post_attn(x, o_mla, o_gqa, o_gla, mod, modidx, lw, n_ctx_tiles):
    N, D = x.shape
    tm = TOKEN_TILE
    row = lambda i, mi: (i, 0)
    ctx_row = lambda i, mi: (jnp.minimum(i, n_ctx_tiles - 1), 0)
    lat_row = lambda i, mi: (jnp.maximum(i - n_ctx_tiles, 0), 0)
    const2 = lambda i, mi: (0, 0)
    return pl.pallas_call(
        functools.partial(_post_attn_kernel, n_ctx_tiles=n_ctx_tiles),
        grid_spec=pltpu.PrefetchScalarGridSpec(
            num_scalar_prefetch=1, grid=(N // tm,),
            in_specs=[pl.BlockSpec((tm, D), row), pl.BlockSpec((tm, 256), ctx_row), pl.BlockSpec((tm, 256), lat_row),
                      pl.BlockSpec((tm, 512), ctx_row), pl.BlockSpec((tm, 512), lat_row),
                      pl.BlockSpec((tm, 256), row), pl.BlockSpec((1, 6, D), lambda i, mi: (mi[i], 0, 0)),
                      pl.BlockSpec((D, D), const2), pl.BlockSpec((1, D), const2),
                      pl.BlockSpec((D, LANES), const2), pl.BlockSpec((D, LANES), const2),
                      pl.BlockSpec((1, LANES), const2), pl.BlockSpec((tm, tm), const2)],
            out_specs=[pl.BlockSpec((tm, D), row), pl.BlockSpec((tm * SUBLANES, LANES), row),
                       pl.BlockSpec((tm, LANES), row), pl.BlockSpec((tm, LANES), row),
                       pl.BlockSpec((SUBLANES, LANES), const2)],
            scratch_shapes=[pltpu.VMEM((1, LANES), F32)]),
        out_shape=[jax.ShapeDtypeStruct((N, D), F32), jax.ShapeDtypeStruct((N * SUBLANES, LANES), F32),
                   jax.ShapeDtypeStruct((N, LANES), I32), jax.ShapeDtypeStruct((N, LANES), F32),
                   jax.ShapeDtypeStruct((SUBLANES, LANES), F32)],
        compiler_params=pltpu.CompilerParams(dimension_semantics=("arbitrary",), vmem_limit_bytes=VMEM_LIMIT),
        name="post_attn",
    )(modidx, x, o_mla[0], o_mla[1], o_gqa[0], o_gqa[1], o_gla, mod, lw['w_out'], lw['g_ffn'], lw['w_router_hi'],
      lw['w_router_lo'], lw['b_router'], lw['tri'])


def _moe_kernel(be_ref, first_ref, valid_ref, src0_ref, src_ref, dst_ref, dstp_ref, h_hbm, wgu_ref, bgu_ref, wd_ref, bd_ref,
                y_hbm, wgu_bf, wd_bf, xbuf0, xbuf1, ybuf0, ybuf1, xsem0, xsem1, ysem0, ysem1):
    del be_ref
    i = pl.program_id(0)
    M = MOE_ROWS

    def gather(idx_ref, r, xb, sem):
        tok = idx_ref[0, 0, r]
        return pltpu.make_async_copy(h_hbm.at[pl.ds(pl.multiple_of(tok * SUBLANES, SUBLANES), SUBLANES), :],
                                     xb.at[pl.ds(r * SUBLANES, SUBLANES), :], sem)

    def scatter(idx_ref, r, yb, sem):
        row = idx_ref[0, 0, r]
        return pltpu.make_async_copy(yb.at[pl.ds(r * SUBLANES, SUBLANES), :],
                                     y_hbm.at[pl.ds(pl.multiple_of(row * SUBLANES, SUBLANES), SUBLANES), :], sem)

    def all_rows_in(xb, sem):
        return pltpu.make_async_copy(h_hbm.at[pl.ds(0, M * SUBLANES), :], xb, sem)

    def all_rows_out(yb, sem):
        return pltpu.make_async_copy(yb, y_hbm.at[pl.ds(0, M * SUBLANES), :], sem)

    @pl.when(i == 0)
    def _():
        ybuf0[...] = jnp.zeros(ybuf0.shape, F32)
        ybuf1[...] = jnp.zeros(ybuf1.shape, F32)

        def prime(r, carry):
            gather(src0_ref, r, xbuf0, xsem0).start()
            scatter(dstp_ref, r, ybuf0, ysem0).start()
            return carry
        lax.fori_loop(0, M, prime, 0)

    @pl.when(first_ref[i] == 1)
    def _():
        wgu_bf[...] = wgu_ref[...].astype(BF16)
        wd_bf[...] = wd_ref[...].astype(BF16)

    def step(xa, xsa, xo, xso, ya, ysa, yo, yso):
        here = valid_ref[i] == 1
        prev = valid_ref[jnp.maximum(i - 1, 0)] == 1

        @pl.when(prev)
        def _():
            all_rows_in(xa, xsa).wait()

        @pl.when(here)
        def _():
            for r in range(M):
                scatter(dst_ref, r, yo, yso).start(priority=1)
                gather(src_ref, r, xo, xso).start()
            gu = _dot(_from_token_tiles(xa, M).astype(BF16), wgu_bf[...]) + bgu_ref[...]
            gate = jnp.minimum(gu[:, :D_EXPERT], SWIGLU_LIMIT)
            up = jnp.clip(gu[:, D_EXPERT:], -SWIGLU_LIMIT, SWIGLU_LIMIT)
            act = gate * jax.nn.sigmoid(SWIGLU_ALPHA * gate) * (up + 1.0)
            y = _dot(act.astype(BF16), wd_bf[...]) + bd_ref[...]
            all_rows_out(ya, ysa).wait()
            _to_token_tiles(ya, y)

        @pl.when(jnp.logical_and(prev, jnp.logical_not(here)))
        def _():
            def move(r, carry):
                scatter(dst_ref, r, yo, yso).start(priority=1)
                return carry
            lax.fori_loop(0, M, move, 0)

    @pl.when(i % 2 == 0)
    def _():
        step(xbuf0, xsem0, xbuf1, xsem1, ybuf0, ysem0, ybuf1, ysem1)

    @pl.when(i % 2 == 1)
    def _():
        step(xbuf1, xsem1, xbuf0, xsem0, ybuf1, ysem1, ybuf0, ysem0)

    @pl.when(i == pl.num_programs(0) - 1)
    def _():
        all_rows_out(ybuf0, ysem0).wait()
        all_rows_out(ybuf1, ysem1).wait()


def _moe_experts(h2t, sched, w_gate_up, b_gate_up, w_down, b_down, layer, n_out_rows):
    block_e, first, valid, src, dst, dst_prime = sched
    n_blocks = block_e.shape[0]
    D = D_MODEL
    M = MOE_ROWS
    L, E = w_gate_up.shape[:2]
    wmap = lambda i, be, fi, va: (layer, be[i], 0, 0)
    smem = lambda imap: pl.BlockSpec((1, 1, M), imap, memory_space=pltpu.SMEM)
    return pl.pallas_call(
        _moe_kernel,
        grid_spec=pltpu.PrefetchScalarGridSpec(
            num_scalar_prefetch=3, grid=(n_blocks,),
            in_specs=[smem(lambda i, be, fi, va: (0, 0, 0)), smem(lambda i, be, fi, va: (i + 1, 0, 0)),
                      smem(lambda i, be, fi, va: (i, 0, 0)), smem(lambda i, be, fi, va: (0, 0, 0)),
                      pl.BlockSpec(memory_space=pl.ANY),
                      pl.BlockSpec((None, None, D, 2 * D_EXPERT), wmap),
                      pl.BlockSpec((None, None, 1, 2 * D_EXPERT), wmap),
                      pl.BlockSpec((None, None, D_EXPERT, D), wmap),
                      pl.BlockSpec((None, None, 1, D), wmap)],
            out_specs=pl.BlockSpec(memory_space=pl.ANY),
            scratch_shapes=[pltpu.VMEM((D, 2 * D_EXPERT), BF16), pltpu.VMEM((D_EXPERT, D), BF16),
                            ] + [pltpu.VMEM((M * SUBLANES, LANES), F32)] * 4 + [pltpu.SemaphoreType.DMA(())] * 4),
        out_shape=jax.ShapeDtypeStruct((n_out_rows * SUBLANES, LANES), F32),
        compiler_params=pltpu.CompilerParams(dimension_semantics=("arbitrary",), vmem_limit_bytes=VMEM_LIMIT),
        name="moe_experts",
    )(block_e, first, valid, src, src, dst, dst_prime, h2t, w_gate_up, b_gate_up.reshape(L, E, 1, 2 * D_EXPERT), w_down,
      b_down.reshape(L, E, 1, D))


def _route(top_idx, rank, counts, n_blocks):
    N = top_idx.shape[0]
    padded = (counts + MOE_ROWS - 1) // MOE_ROWS * MOE_ROWS
    pad_end = jnp.cumsum(padded)
    pad_start = pad_end - padded
    dest = pad_start[top_idx] + rank
    M, K = MOE_ROWS, TOP_K
    blk_start = jnp.arange(n_blocks, dtype=I32) * M
    valid = blk_start < pad_end[-1]
    block_e = jnp.minimum((blk_start[:, None] >= pad_end[None, :]).astype(I32).sum(axis=1), N_EXPERTS - 1)
    order = jnp.argsort(dest.reshape(-1)).astype(I32)
    first_pos = jnp.cumsum(counts) - counts
    slot_id = jnp.arange(n_blocks * M, dtype=I32)
    e_slot = jnp.repeat(block_e, M)
    offset = slot_id - pad_start[e_slot]
    real = jnp.repeat(valid, M) & (offset < counts[e_slot])
    slot_asg = jnp.where(real, order[jnp.clip(first_pos[e_slot] + offset, 0, N * K - 1)], -1)
    tok = jnp.where(real, slot_asg // K, 0)
    spare = K * N + ((slot_id // M) % 2) * M + slot_id % M
    out_row = jnp.where(real, (slot_asg % K) * N + tok, spare)
    src = jnp.concatenate([tok.reshape(n_blocks, 1, M), jnp.zeros((1, 1, M), I32)], axis=0)
    lead = (K * N + M + jnp.arange(M, dtype=I32)).reshape(1, 1, M)
    dst = jnp.concatenate([lead, out_row.reshape(n_blocks, 1, M)], axis=0)
    dst_prime = (K * N + jnp.arange(M, dtype=I32)).reshape(1, 1, M)
    n_valid = pad_end[-1] // M
    block_e = block_e[jnp.minimum(jnp.arange(n_blocks, dtype=I32), n_valid - 1)]
    prev_e = jnp.concatenate([jnp.full((1,), -1, I32), block_e[:-1]])
    first = (valid & (block_e != prev_e)).astype(I32)
    return (block_e, first, valid.astype(I32), src, dst, dst_prime)


def _final_kernel(modidx_ref, x_ref, *refs, n_ctx_tiles):
    del modidx_ref
    y_refs, (tg_ref, pmod_ref, g_ref, octx_ref, olat_ref) = refs[:TOP_K], refs[TOP_K:]
    x = x_ref[...] + pmod_ref[0, 5:6, :] * _combine_experts(y_refs, tg_ref)
    out = x * lax.rsqrt(jnp.mean(x * x, axis=-1, keepdims=True) + EPS) * g_ref[...]
    is_ctx = pl.program_id(0) < n_ctx_tiles

    @pl.when(is_ctx)
    def _():
        octx_ref[...] = out

    @pl.when(jnp.logical_not(is_ctx))
    def _():
        olat_ref[...] = out


def _final_norm(x, y, pmod, modidx, g_final, n_ctx_tiles):
    N, D = x.shape
    tm = TOKEN_TILE
    row = lambda i, mi: (i, 0)
    return pl.pallas_call(
        functools.partial(_final_kernel, n_ctx_tiles=n_ctx_tiles),
        grid_spec=pltpu.PrefetchScalarGridSpec(
            num_scalar_prefetch=1, grid=(N // tm,),
            in_specs=[pl.BlockSpec((tm, D), row)]
            + [pl.BlockSpec((tm * SUBLANES, LANES), functools.partial(lambda k, i, mi: (k * (N // tm) + i, 0), k))
               for k in range(TOP_K)]
            + [pl.BlockSpec((tm, LANES), row),
               pl.BlockSpec((1, 6, D), lambda i, mi: (mi[i], 0, 0)), pl.BlockSpec((1, D), lambda i, mi: (0, 0))],
            out_specs=[pl.BlockSpec((tm, D), lambda i, mi: (jnp.minimum(i, n_ctx_tiles - 1), 0)),
                       pl.BlockSpec((tm, D), lambda i, mi: (jnp.maximum(i - n_ctx_tiles, 0), 0))]),
        out_shape=[jax.ShapeDtypeStruct((n_ctx_tiles * tm, D), F32),
                   jax.ShapeDtypeStruct((N - n_ctx_tiles * tm, D), F32)],
        compiler_params=pltpu.CompilerParams(dimension_semantics=("arbitrary",), vmem_limit_bytes=VMEM_LIMIT),
        name="final_norm",
    )(modidx, x, *([y[0]] * TOP_K), y[1], pmod, g_final)


def _rope_tables(seq_len):
    pos = np.arange(seq_len)
    rowp = (pos // GRID_W).astype(np.float32)
    colp = (pos % GRID_W).astype(np.float32)

    def tables(rdim, copies):
        quarter = rdim // 4
        inv_freq = jnp.asarray(ROPE_THETA, F32) ** (-jnp.arange(quarter, dtype=F32) / quarter)
        ar = jnp.asarray(rowp)[:, None] * inv_freq[None, :]
        ac = jnp.asarray(colp)[:, None] * inv_freq[None, :]
        cos = jnp.concatenate([jnp.cos(ar), jnp.cos(ar), jnp.cos(ac), jnp.cos(ac)], axis=1)
        sin = jnp.concatenate([-jnp.sin(ar), jnp.sin(ar), -jnp.sin(ac), jnp.sin(ac)], axis=1)
        return jnp.tile(cos, (1, copies)), jnp.tile(sin, (1, copies))

    c64, s64 = tables(HEAD_DIM, 2)
    c32, s32 = tables(MLA_ROPE, 4)
    lat = jnp.concatenate([c64, s64, c32, s32], axis=1).reshape(seq_len // TOKEN_TILE, TOKEN_TILE, 512)
    ident = jnp.concatenate([jnp.ones((TOKEN_TILE, 128), F32), jnp.zeros((TOKEN_TILE, 128), F32)] * 2, axis=1)
    return jnp.concatenate([ident[None], lat], axis=0)


def _block_ones(n, blk):
    r = np.arange(n) // blk
    return jnp.asarray((r[:, None] == r[None, :]).astype(np.float32), BF16)


def kernel(x_prompt, x_sample, cache_mla_ckv, cache_mla_krope, cache_gqa_k, cache_gqa_v, state_gla, c, c_ctx, w_mod, b_mod, g_attn_norm, g_ffn_norm, w_in, g_q_a, w_uq, g_kv_a, w_ukv, g_q_head, g_k_head, w_gk_fwd, b_gk_fwd, w_gk_bwd, b_gk_bwd, g_gla_out, w_out, w_router, b_router, w_gate_up, b_gate_up, w_down, b_down, g_final):
    B, T, D = x_prompt.shape
    BD, TD, _ = x_sample.shape
    L = w_in.shape[0]
    P = cache_mla_ckv.shape[2]
    tm = TOKEN_TILE
    n_ctx = B * T
    N = n_ctx + BD * TD
    assert T == tm and all(TD % min(t, TD) == 0 for t in ATTN_TQ.values()) and n_ctx % TD == 0 and BD + 1 <= 16
    n_ctx_tiles = n_ctx // tm
    nt = N // tm
    tiles_per_lat = TD // tm

    tile_ids = np.arange(nt)
    lat_tile = np.maximum(tile_ids - n_ctx_tiles, 0)
    is_lat = tile_ids >= n_ctx_tiles
    modidx = jnp.asarray(np.where(is_lat, 1 + lat_tile // tiles_per_lat, 0), I32)
    tabidx = jnp.asarray(np.where(is_lat, 1 + lat_tile % tiles_per_lat, 0), I32)
    seq_of_tile = np.where(is_lat, B + lat_tile // tiles_per_lat, tile_ids)
    first_fwd = np.where(is_lat, lat_tile % tiles_per_lat == 0, True)
    last_fwd = np.where(is_lat, lat_tile % tiles_per_lat == tiles_per_lat - 1, True)
    order_fwd = (jnp.asarray(tile_ids, I32), jnp.asarray(seq_of_tile, I32), jnp.asarray(first_fwd, I32))
    rev = tile_ids[::-1]
    order_bwd = (jnp.asarray(rev, I32), jnp.asarray(seq_of_tile[rev], I32), jnp.asarray(last_fwd[rev], I32))
    n_seq = B + BD

    cvecs = jnp.zeros((16, D), F32).at[0].set(c_ctx).at[1:1 + BD].set(c)
    mods = _modulation(cvecs, w_mod, b_mod).reshape(L, 16, 6, D)

    o = np.cumsum([0, Q_LORA, KV_LORA, MLA_ROPE, 512, 128, 128, 128, 128, 256, 16, 16, 256])
    seg = lambda j: w_in[:, :, o[j]:o[j + 1]]
    w_in_r = jnp.concatenate(
        [seg(0), seg(1), seg(3), seg(4), seg(5), seg(6), seg(7), seg(8), seg(11), seg(2), seg(2), seg(2), seg(2),
         seg(9), seg(10), jnp.zeros((L, D, 128 - 2 * GLA_GATE_RANK), F32)], axis=-1).astype(BF16)
    uq = w_uq.reshape(L, Q_LORA, MLA_HEADS, MLA_QK)
    ukv = w_ukv.reshape(L, KV_LORA, MLA_HEADS, MLA_NOPE + MLA_V)
    w_comb = _fold_q_weights(uq[..., :MLA_NOPE].transpose(0, 2, 1, 3), ukv[..., :MLA_NOPE].transpose(0, 2, 1, 3))
    q_pad = jnp.zeros((L, Q_LORA, 256 - KV_LORA - MLA_ROPE), F32)
    w_q2 = (jnp.concatenate([part for h in range(MLA_HEADS)
                             for part in (w_comb[:, h], uq[:, :, h, MLA_NOPE:], q_pad)], axis=-1)
            * (MLA_QK ** -0.5 * LOG2E)).astype(BF16)
    w_uv = ukv[..., MLA_NOPE:].transpose(0, 2, 1, 3)
    eye_h = jnp.eye(MLA_HEADS, dtype=F32)
    wuv_pad = jnp.einsum('lhkv,hg->lhkgv', w_uv, eye_h).reshape(L, MLA_HEADS, KV_LORA, MLA_HEADS * MLA_V).astype(BF16)
    w_gate = jnp.zeros((L, 128, 256), F32).at[:, 0:16, 0:128].set(w_gk_fwd).at[:, 16:32, 128:256].set(w_gk_bwd)
    w_gate = w_gate.astype(BF16)
    b_gate = jnp.concatenate([b_gk_fwd, b_gk_bwd], axis=-1)
    w_out_bf = w_out.astype(BF16)
    wr_pad = jnp.pad(w_router, ((0, 0), (0, 0), (0, LANES - N_EXPERTS)))
    wr_hi = wr_pad.astype(BF16)
    wr_lo = (wr_pad - wr_hi.astype(F32)).astype(BF16)
    br_pad = jnp.pad(b_router, ((0, 0), (0, LANES - N_EXPERTS)), constant_values=NEG_BIG)
    ones512 = _block_ones(512, HEAD_DIM)
    tri = jnp.asarray((np.arange(tm)[None, :] < np.arange(tm)[:, None]).astype(np.float32), BF16)
    head_expand = jnp.asarray((np.arange(128)[:, None] // GLA_DK == np.arange(256)[None, :] // GLA_DV)
                              .astype(np.float32), BF16)
    tab = _rope_tables(TD)

    x = jnp.concatenate([x_prompt.reshape(n_ctx, D), x_sample.reshape(BD * TD, D)], axis=0)
    y = None
    n_blocks = (N * TOP_K) // MOE_ROWS + N_EXPERTS
    assert (N * TOP_K) % MOE_ROWS == 0
    caches, states = [], []
    for l in range(L):
        lw = dict(g_attn=g_attn_norm[l][None], w_in_r=w_in_r[l], g_q_a=g_q_a[l][None], w_q2=w_q2[l],
                  g_kv_a=g_kv_a[l][None], g_qh=jnp.tile(g_q_head[l], GQA_HEADS)[None],
                  g_kh=jnp.tile(g_k_head[l], GQA_KV_HEADS)[None], ones512=ones512, w_gate=w_gate[l],
                  b_gate=b_gate[l][None], head_expand=head_expand, g_gla_out=jnp.tile(g_gla_out[l], GLA_HEADS)[None],
                  ones256=ones512[:256, :256], w_out=w_out_bf[l], g_ffn=g_ffn_norm[l][None], tri=tri,
                  w_router_hi=wr_hi[l], w_router_lo=wr_lo[l], b_router=br_pad[l][None])
        pmod = mods[l - 1] if l > 0 else None
        x, qm, kvc, qg, kdup, vdup, gin, cache = _pre_attn(x, y, pmod, mods[l], tab, modidx, tabidx, lw, n_ctx_tiles)
        caches.append(cache[:n_ctx])

        mla_ctx = jnp.concatenate([cache_mla_ckv[:, l]] + [cache_mla_krope[:, l]] * 4, axis=-1).astype(BF16)
        ck = cache_gqa_k[:, l]
        cv = cache_gqa_v[:, l]
        k_ctx = jnp.concatenate([ck[:, :, 0], ck[:, :, 0], ck[:, :, 1], ck[:, :, 1]], axis=-1).astype(BF16)
        ones_v = jnp.ones(cv.shape[:2] + (HEAD_DIM,), F32)
        v_ctx = jnp.concatenate([cv[:, :, 0], ones_v, cv[:, :, 1], ones_v], axis=-1).astype(BF16)
        o_mla = (_attention('mla', qm, kvc, None, None, wuv_pad[l], n_seq=B, seq_len=T, tile_offset=0),
                 _attention('mla', qm, kvc, None, (mla_ctx,), wuv_pad[l], n_seq=BD, seq_len=TD,
                            tile_offset=n_ctx // TD))
        o_gqa = (_attention('gqa', qg, kdup, vdup, None, None, n_seq=B, seq_len=T, tile_offset=0),
                 _attention('gqa', qg, kdup, vdup, (k_ctx, v_ctx), None, n_seq=BD, seq_len=TD,
                            tile_offset=n_ctx // TD))

        eye_g = jnp.eye(GLA_HEADS, dtype=F32)
        st_lat = jnp.einsum('bshde,hg->bshegd', state_gla[:, l], eye_g).reshape(BD, 2, 256, 128)
        zeros_ctx = jnp.zeros((B, 256, 128), F32)
        o_f, s_f = _gla(gin, jnp.concatenate([zeros_ctx, st_lat[:, 0]], axis=0), None, lw, order_fwd, reverse=False)
        o_gla, s_b = _gla(gin, jnp.concatenate([zeros_ctx, st_lat[:, 1]], axis=0), o_f, lw, order_bwd, reverse=True)
        states.append((s_f[:B], s_b[:B]))

        x1, h2, ti, tg, cnt = _post_attn(x, o_mla, o_gqa, o_gla, mods[l], modidx, lw, n_ctx_tiles)
        sched = _route(ti[:, :TOP_K], ti[:, TOP_K:2 * TOP_K], cnt[0, :N_EXPERTS].astype(I32), n_blocks)
        y = (_moe_experts(h2, sched, w_gate_up, b_gate_up, w_down, b_down, l, TOP_K * N + 2 * MOE_ROWS), tg)
        x = x1

    out_ctx, out_lat = _final_norm(x, y, mods[L - 1], modidx, g_final[None], n_ctx_tiles)
    y_prompt = out_ctx.reshape(B, T, D)
    y_sample = out_lat.reshape(BD, TD, D)
    cache_all = jnp.stack([cc.reshape(B, T, 512) for cc in caches], axis=1)
    new_ckv = cache_all[..., 0:128]
    new_k = cache_all[..., 128:256].reshape(B, L, T, GQA_KV_HEADS, HEAD_DIM)
    new_v = cache_all[..., 256:384].reshape(B, L, T, GQA_KV_HEADS, HEAD_DIM)
    new_krope = cache_all[..., 384:384 + MLA_ROPE]

    def unpack_state(st):
        s5 = st.reshape(B, GLA_HEADS, GLA_DV, GLA_HEADS, GLA_DK)
        diag = jnp.stack([s5[:, h, :, h, :] for h in range(GLA_HEADS)], axis=1)
        return diag.transpose(0, 1, 3, 2)

    new_state = jnp.stack([jnp.stack([unpack_state(sf), unpack_state(sb)], axis=1) for sf, sb in states], axis=1)
    return (y_prompt, y_sample, new_ckv, new_krope, new_k, new_v, new_state)
```
